```python
import jax, jax.numpy as jnp
from jax import lax
import numpy as np

D_MODEL = 4096
BATCH = 4
SEQ = 2048
DEPTH = 1
DEC_BATCH = 128
DEC_SEQ = 4
PAST_LEN = 16384
PAGE_SIZE = 128

MIX_WIDTH = D_MODEL
D_RNN = MIX_WIDTH // 2
RG_BLOCK = 256
RG_HEADS = D_RNN // RG_BLOCK
RG_C = 8.0
D_SSM = MIX_WIDTH - D_RNN
SSD_HEADDIM = 64
SSD_HEADS = D_SSM // SSD_HEADDIM
SSD_GROUPS = 4
D_STATE = 128
SSD_CHUNK = 128
CONV_K = 4
D_XBC = D_SSM + 2 * SSD_GROUPS * D_STATE
D_FF = 3 * D_MODEL
FFN_CONV_K = 3
IN_COLS = 2 * D_RNN + D_SSM + D_XBC + SSD_HEADS
EPS = 1e-6

kernel_name = 'hybrid_rglru_ssd_convglu_step'


def rmsnorm(x, g):
    x32 = x.astype(jnp.float32)
    y = x32 * lax.rsqrt(jnp.mean(x32 * x32, axis=-1, keepdims=True) + EPS)
    return (y * g.astype(jnp.float32)).astype(x.dtype)


def causal_dwconv(u, w, b, buf):
    K = w.shape[0]
    L = u.shape[1]
    ext = jnp.concatenate([buf.astype(u.dtype), u], axis=1)
    out = b + ext[:, 0:L] * w[0]
    for k in range(1, K):
        out = out + ext[:, k:k + L] * w[k]
    return out, ext[:, L:]


def rglru(xc, h0, wa, ba, wi, bi, lam):
    b, L, Dr = xc.shape
    f32 = jnp.float32
    xh = xc.reshape(b, L, RG_HEADS, RG_BLOCK)
    gate_r = jax.nn.sigmoid(jnp.einsum('blhi,hij->blhj', xh, wa) + ba).reshape(b, L, Dr).astype(f32)
    gate_i = jax.nn.sigmoid(jnp.einsum('blhi,hij->blhj', xh, wi) + bi).reshape(b, L, Dr).astype(f32)
    log_a = -RG_C * gate_r * jax.nn.softplus(-lam.astype(f32))
    a = jnp.exp(log_a)
    bx = jnp.sqrt(-jnp.expm1(2.0 * log_a)) * (gate_i * xc.astype(f32))
    bx = bx.at[:, 0].add(a[:, 0] * h0.astype(f32))

    def combine(left, right):
        a1, b1 = left
        a2, b2 = right
        return a1 * a2, a2 * b1 + b2

    _, h = lax.associative_scan(combine, (a, bx), axis=1)
    return h, h[:, -1]


def ssd_chunked(x, dt, A, Bm, Cm, h0):
    f32 = jnp.float32
    b, L, H, P = x.shape
    G, N = Bm.shape[2], Bm.shape[3]
    R = H // G
    Q = min(SSD_CHUNK, L)
    nc = -(-L // Q)
    pad = nc * Q - L
    x = x.astype(f32)
    Bm = Bm.astype(f32)
    Cm = Cm.astype(f32)
    if pad:
        x = jnp.pad(x, ((0, 0), (0, pad), (0, 0), (0, 0)))
        dt = jnp.pad(dt, ((0, 0), (0, pad), (0, 0)))
        Bm = jnp.pad(Bm, ((0, 0), (0, pad), (0, 0), (0, 0)))
        Cm = jnp.pad(Cm, ((0, 0), (0, pad), (0, 0), (0, 0)))
    xdt = (x * dt[..., None]).reshape(b, nc, Q, G, R, P)
    Bc = Bm.reshape(b, nc, Q, G, N)
    Cc = Cm.reshape(b, nc, Q, G, N)
    cum = jnp.cumsum((dt * A).reshape(b, nc, Q, G, R), axis=2)
    seg = cum[:, :, :, None] - cum[:, :, None, :]
    causal = jnp.tril(jnp.ones((Q, Q), dtype=bool))[None, None, :, :, None, None]
    Ldec = jnp.exp(jnp.where(causal, seg, -jnp.inf))
    CB = jnp.einsum('bctgn,bcsgn->bctsg', Cc, Bc)
    y = jnp.einsum('bctsg,bctsgr,bcsgrp->bctgrp', CB, Ldec, xdt)
    decay_end = jnp.exp(cum[:, :, -1:] - cum)
    states = jnp.einsum('bcsgn,bcsgr,bcsgrp->bcgrpn', Bc, decay_end, xdt)
    chunk_decay = jnp.exp(cum[:, :, -1])

    def step(h, inp):
        st, dec = inp
        return h * dec[..., None, None] + st, h

    h_final, h_prev = lax.scan(step, h0.astype(f32).reshape(b, G, R, P, N),
                               (jnp.moveaxis(states, 1, 0), jnp.moveaxis(chunk_decay, 1, 0)))
    h_prev = jnp.moveaxis(h_prev, 0, 1)
    y = y + jnp.einsum('bctgn,bcgrpn,bctgr->bctgrp', Cc, h_prev, jnp.exp(cum))
    y = y.reshape(b, nc * Q, H, P)[:, :L]
    return y, h_final.reshape(b, H, P, N)


def layer(x, h_rg, buf_rg, h_ssm, buf_ssd, buf_ff, p):
    b, L, _ = x.shape
    f32 = jnp.float32
    h = rmsnorm(x, p['g_mix'])
    proj = h @ p['w_in']
    rg_x, rg_gate, z, xbc, dt_raw = jnp.split(
        proj, [D_RNN, 2 * D_RNN, 2 * D_RNN + D_SSM, 2 * D_RNN + D_SSM + D_XBC], axis=-1)

    rg_xc, buf_rg_new = causal_dwconv(rg_x, p['rg_conv_w'], p['rg_conv_b'], buf_rg)
    rg_h, h_rg_new = rglru(rg_xc, h_rg, p['rg_gate_a_w'], p['rg_gate_a_b'],
                           p['rg_gate_i_w'], p['rg_gate_i_b'], p['rg_lambda'])
    rg_out = rmsnorm((rg_h * jax.nn.gelu(rg_gate.astype(f32))).astype(x.dtype), p['g_rg_out'])

    xbc_c, buf_ssd_new = causal_dwconv(xbc, p['ssd_conv_w'], p['ssd_conv_b'], buf_ssd)
    xbc_c = jax.nn.silu(xbc_c)
    xs, Bm, Cm = jnp.split(xbc_c, [D_SSM, D_SSM + SSD_GROUPS * D_STATE], axis=-1)
    xs = xs.reshape(b, L, SSD_HEADS, SSD_HEADDIM)
    dt = jax.nn.softplus(dt_raw.astype(f32) + p['ssd_dt_bias'].astype(f32))
    A = -jnp.exp(p['ssd_A_log'].astype(f32))
    y, h_ssm_new = ssd_chunked(xs, dt, A, Bm.reshape(b, L, SSD_GROUPS, D_STATE),
                               Cm.reshape(b, L, SSD_GROUPS, D_STATE), h_ssm)
    y = y + p['ssd_D'].astype(f32)[:, None] * xs.astype(f32)
    u = y.reshape(b, L, D_SSM) * jax.nn.silu(z.astype(f32))
    ug = u.reshape(b, L, SSD_GROUPS, D_SSM // SSD_GROUPS)
    ug = ug * lax.rsqrt(jnp.mean(ug * ug, axis=-1, keepdims=True) + EPS)
    ssd_out = (ug.reshape(b, L, D_SSM) * p['g_ssd_norm'].astype(f32)).astype(x.dtype)

    x = x + jnp.concatenate([rg_out, ssd_out], axis=-1) @ p['w_out']

    h2 = rmsnorm(x, p['g_ffn'])
    gate, val = jnp.split(h2 @ p['w_ffn_up'], [D_FF], axis=-1)
    gate_c, buf_ff_new = causal_dwconv(gate, p['ffn_conv_w'], p['ffn_conv_b'], buf_ff)
    x = x + (jax.nn.gelu(gate_c) * val) @ p['w_ffn_down']
    dt_out = x.dtype
    return (x, h_rg_new.astype(dt_out), buf_rg_new, h_ssm_new.astype(dt_out), buf_ssd_new, buf_ff_new)


def setup_inputs(seed: int = 0) -> dict:
    key = jax.random.key(seed)
    ks = jax.random.split(key, 40)
    f32 = jnp.float32
    nrm = lambda k, s, sc: jax.random.normal(k, s, f32) * sc
    a0 = jax.random.uniform(ks[10], (DEPTH, D_RNN), f32, 0.9, 0.999)
    s0 = a0 ** (1.0 / RG_C)
    rg_lambda = jnp.log(s0 / (1.0 - s0))
    dt0 = jnp.exp(jax.random.uniform(ks[14], (DEPTH, SSD_HEADS), f32, np.log(1e-3), np.log(1e-1)))
    ssd_dt_bias = dt0 + jnp.log(-jnp.expm1(-dt0))
    return {
        'x_prompt': nrm(ks[0], (BATCH, SEQ, D_MODEL), 1.0),
        'x_sample': nrm(ks[1], (DEC_BATCH, DEC_SEQ, D_MODEL), 1.0),
        'state_rglru_h': nrm(ks[2], (DEPTH, DEC_BATCH, D_RNN), 0.5),
        'state_rglru_conv': nrm(ks[3], (DEPTH, DEC_BATCH, CONV_K - 1, D_RNN), 1.0),
        'state_ssd_h': nrm(ks[4], (DEPTH, DEC_BATCH, SSD_HEADS, SSD_HEADDIM, D_STATE), 0.1),
        'state_ssd_conv': nrm(ks[5], (DEPTH, DEC_BATCH, CONV_K - 1, D_XBC), 1.0),
        'state_ffn_conv': nrm(ks[6], (DEPTH, DEC_BATCH, FFN_CONV_K - 1, D_FF), 1.0),
        'g_mix': 1.0 + nrm(ks[7], (DEPTH, D_MODEL), 0.02),
        'w_in': nrm(ks[8], (DEPTH, D_MODEL, IN_COLS), D_MODEL ** -0.5),
        'rg_conv_w': nrm(ks[9], (DEPTH, CONV_K, D_RNN), CONV_K ** -0.5),
        'rg_conv_b': nrm(ks[11], (DEPTH, D_RNN), 0.02),
        'rg_gate_a_w': nrm(ks[12], (DEPTH, RG_HEADS, RG_BLOCK, RG_BLOCK), RG_BLOCK ** -0.5),
        'rg_gate_a_b': nrm(ks[13], (DEPTH, RG_HEADS, RG_BLOCK), 0.02),
        'rg_gate_i_w': nrm(ks[15], (DEPTH, RG_HEADS, RG_BLOCK, RG_BLOCK), RG_BLOCK ** -0.5),
        'rg_gate_i_b': nrm(ks[16], (DEPTH, RG_HEADS, RG_BLOCK), 0.02),
        'rg_lambda': rg_lambda,
        'g_rg_out': 1.0 + nrm(ks[17], (DEPTH, D_RNN), 0.02),
        'ssd_conv_w': nrm(ks[18], (DEPTH, CONV_K, D_XBC), CONV_K ** -0.5),
        'ssd_conv_b': nrm(ks[19], (DEPTH, D_XBC), 0.02),
        'ssd_dt_bias': ssd_dt_bias,
        'ssd_A_log': jnp.log(jax.random.uniform(ks[20], (DEPTH, SSD_HEADS), f32, 1.0, 16.0)),
        'ssd_D': 1.0 + nrm(ks[21], (DEPTH, SSD_HEADS), 0.02),
        'g_ssd_norm': 1.0 + nrm(ks[22], (DEPTH, D_SSM), 0.02),
        'w_out': nrm(ks[23], (DEPTH, MIX_WIDTH, D_MODEL), MIX_WIDTH ** -0.5),
        'g_ffn': 1.0 + nrm(ks[24], (DEPTH, D_MODEL), 0.02),
        'w_ffn_up': nrm(ks[25], (DEPTH, D_MODEL, 2 * D_FF), D_MODEL ** -0.5),
        'ffn_conv_w': nrm(ks[26], (DEPTH, FFN_CONV_K, D_FF), FFN_CONV_K ** -0.5),
        'ffn_conv_b': nrm(ks[27], (DEPTH, D_FF), 0.02),
        'w_ffn_down': nrm(ks[28], (DEPTH, D_FF, D_MODEL), D_FF ** -0.5),
        'g_final': 1.0 + nrm(ks[29], (D_MODEL,), 0.02),
    }


def reference(x_prompt, x_sample, state_rglru_h, state_rglru_conv, state_ssd_h, state_ssd_conv,
              state_ffn_conv, g_mix, w_in, rg_conv_w, rg_conv_b, rg_gate_a_w, rg_gate_a_b,
              rg_gate_i_w, rg_gate_i_b, rg_lambda, g_rg_out, ssd_conv_w, ssd_conv_b, ssd_dt_bias,
              ssd_A_log, ssd_D, g_ssd_norm, w_out, g_ffn, w_ffn_up, ffn_conv_w, ffn_conv_b,
              w_ffn_down, g_final):
    yp = x_prompt
    ys = x_sample
    bp = x_prompt.shape[0]
    dtp = x_prompt.dtype
    p_new = [[], [], [], [], []]
    s_new = [[], [], [], [], []]
    for l in range(DEPTH):
        p = {
            'g_mix': g_mix[l], 'w_in': w_in[l], 'rg_conv_w': rg_conv_w[l], 'rg_conv_b': rg_conv_b[l],
            'rg_gate_a_w': rg_gate_a_w[l], 'rg_gate_a_b': rg_gate_a_b[l],
            'rg_gate_i_w': rg_gate_i_w[l], 'rg_gate_i_b': rg_gate_i_b[l],
            'rg_lambda': rg_lambda[l], 'g_rg_out': g_rg_out[l],
            'ssd_conv_w': ssd_conv_w[l], 'ssd_conv_b': ssd_conv_b[l], 'ssd_dt_bias': ssd_dt_bias[l],
            'ssd_A_log': ssd_A_log[l], 'ssd_D': ssd_D[l], 'g_ssd_norm': g_ssd_norm[l],
            'w_out': w_out[l], 'g_ffn': g_ffn[l], 'w_ffn_up': w_ffn_up[l],
            'ffn_conv_w': ffn_conv_w[l], 'ffn_conv_b': ffn_conv_b[l], 'w_ffn_down': w_ffn_down[l],
        }
        outp = layer(yp,
                     jnp.zeros((bp, D_RNN), dtp),
                     jnp.zeros((bp, CONV_K - 1, D_RNN), dtp),
                     jnp.zeros((bp, SSD_HEADS, SSD_HEADDIM, D_STATE), dtp),
                     jnp.zeros((bp, CONV_K - 1, D_XBC), dtp),
                     jnp.zeros((bp, FFN_CONV_K - 1, D_FF), dtp), p)
        outs = layer(ys, state_rglru_h[l], state_rglru_conv[l], state_ssd_h[l],
                     state_ssd_conv[l], state_ffn_conv[l], p)
        yp = outp[0]
        ys = outs[0]
        for j in range(5):
            p_new[j].append(outp[j + 1])
            s_new[j].append(outs[j + 1])
    y_prompt = rmsnorm(yp, g_final)
    y_sample = rmsnorm(ys, g_final)
    return (y_prompt, y_sample,
            jnp.stack(p_new[0]), jnp.stack(p_new[1]), jnp.stack(p_new[2]), jnp.stack(p_new[3]), jnp.stack(p_new[4]),
            jnp.stack(s_new[0]), jnp.stack(s_new[1]), jnp.stack(s_new[2]), jnp.stack(s_new[3]), jnp.stack(s_new[4]))
```

```python
import functools

import jax
import jax.numpy as jnp
from jax import lax
from jax.experimental import pallas as pl
from jax.experimental.pallas import tpu as pltpu

F32 = jnp.float32
BF16 = jnp.bfloat16

EPS = 1e-6
RG_C = 8.0
SSD_CHUNK = 128
SSD_GROUPS = 4
LANES = 128
SUBLANES = 8
VMEM_LIMIT = 56 * 2**20


def _params(n_axes):
    return pltpu.CompilerParams(dimension_semantics=("arbitrary",) * n_axes,
                                vmem_limit_bytes=VMEM_LIMIT)


def _dot(a, b):
    return jnp.dot(a, b, preferred_element_type=F32)


def _rms_rows(x, g):
    ms = jnp.mean(x * x, axis=-1, keepdims=True)
    return (x * lax.rsqrt(ms + EPS)) * g


def _split3(x):
    hi = x.astype(BF16)
    r = x - hi.astype(F32)
    mid = r.astype(BF16)
    lo = (r - mid.astype(F32)).astype(BF16)
    return hi, mid, lo


def _dot_sel_right(x, sel):
    hi, mid, lo = _split3(x)
    return (_dot(hi, sel) + _dot(mid, sel)) + _dot(lo, sel)


def _dot_sel_left(sel, x):
    hi, mid, lo = _split3(x)
    return (_dot(sel, hi) + _dot(sel, mid)) + _dot(sel, lo)


def _inproj_kernel(x_ref, g_ref, w_ref, wdt_ref, o_ref, dt_ref, hn_ref):
    @pl.when(pl.program_id(1) == 0)
    def _():
        hn = _rms_rows(x_ref[...], g_ref[...]).astype(BF16)
        hn_ref[...] = hn
        dt_ref[...] = _dot(hn, wdt_ref[...])

    o_ref[...] = _dot(hn_ref[...], w_ref[...])


def _in_proj(x2d, g, w_b, wdt_b, n_main, *, tm, tn):
    m, d = x2d.shape
    return pl.pallas_call(
        _inproj_kernel,
        grid=(m // tm, n_main // tn),
        in_specs=[pl.BlockSpec((tm, d), lambda i, j: (i, 0)),
                  pl.BlockSpec((1, d), lambda i, j: (0, 0)),
                  pl.BlockSpec((d, tn), lambda i, j: (0, j)),
                  pl.BlockSpec((d, LANES), lambda i, j: (0, 0))],
        out_specs=[pl.BlockSpec((tm, tn), lambda i, j: (i, j)),
                   pl.BlockSpec((tm, LANES), lambda i, j: (i, 0))],
        out_shape=[jax.ShapeDtypeStruct((m, n_main), F32),
                   jax.ShapeDtypeStruct((m, LANES), F32)],
        scratch_shapes=[pltpu.VMEM((tm, d), BF16)],
        compiler_params=_params(2),
        name="in_proj",
    )(x2d, g, w_b, wdt_b)


def _outproj_kernel(rg_ref, ssd_ref, wt_ref, wb_ref, x_ref, g_ref, x1_ref, h2_ref, row_ref, *, nj, tn):
    j = pl.program_id(1)
    x1 = x_ref[...] + (_dot(rg_ref[...], wt_ref[...]) + _dot(ssd_ref[...], wb_ref[...]))
    x1_ref[...] = x1
    row_ref[j] = x1

    @pl.when(j == nj - 1)
    def _():
        ss = jnp.sum(row_ref[0] * row_ref[0], axis=-1, keepdims=True)
        for jj in range(1, nj):
            ss = ss + jnp.sum(row_ref[jj] * row_ref[jj], axis=-1, keepdims=True)
        inv = lax.rsqrt(ss / (nj * tn) + EPS)
        for jj in range(nj):
            h2_ref[:, jj * tn:(jj + 1) * tn] = (
                (row_ref[jj] * inv) * g_ref[:, jj * tn:(jj + 1) * tn]).astype(BF16)


def _out_proj(rg, ssd, w_b, x2d, g, *, tm, tn):
    m, d = x2d.shape
    k_half = rg.shape[1]
    nj = d // tn
    return pl.pallas_call(
        functools.partial(_outproj_kernel, nj=nj, tn=tn),
        grid=(m // tm, nj),
        in_specs=[pl.BlockSpec((tm, k_half), lambda i, j: (i, 0)),
                  pl.BlockSpec((tm, k_half), lambda i, j: (i, 0)),
                  pl.BlockSpec((k_half, tn), lambda i, j: (0, j)),
                  pl.BlockSpec((k_half, tn), lambda i, j: (1, j)),
                  pl.BlockSpec((tm, tn), lambda i, j: (i, j)),
                  pl.BlockSpec((1, d), lambda i, j: (0, 0))],
        out_specs=[pl.BlockSpec((tm, tn), lambda i, j: (i, j)),
                   pl.BlockSpec((tm, d), lambda i, j: (i, 0))],
        out_shape=[jax.ShapeDtypeStruct((m, d), F32),
                   jax.ShapeDtypeStruct((m, d), BF16)],
        scratch_shapes=[pltpu.VMEM((nj, tm, tn), F32)],
        compiler_params=_params(2),
        name="out_proj",
    )(rg, ssd, w_b, w_b, x2d, g)


def _glu(gate_c, val):
    return (jax.nn.gelu(gate_c) * val).astype(BF16)


def _ffn_up_prompt_kernel(h2_ref, wg_ref, wv_ref, cw_ref, cb_ref, act_ref, gst_ref, carry_ref,
                          *, tiles_per_seq):
    i = pl.program_id(0)
    j = pl.program_id(1)
    gate = _dot(h2_ref[...], wg_ref[...])
    val = _dot(h2_ref[...], wv_ref[...])
    tm, tn = gate.shape
    w0, w1, w2 = cw_ref[0:1, :], cw_ref[1:2, :], cw_ref[2:3, :]
    cb = cb_ref[...]
    g1 = pltpu.roll(gate, 1, 0)
    g2 = pltpu.roll(gate, 2, 0)
    act_ref[...] = _glu(((cb + g2 * w0) + g1 * w1) + gate * w2, val)
    @pl.when((i % tiles_per_seq) == 0)
    def _():
        carry_ref[j] = jnp.zeros((SUBLANES, tn), F32)

    prev = carry_ref[j]
    rows = lax.broadcasted_iota(jnp.int32, (SUBLANES, tn), 0)
    t1 = jnp.where(rows < 1, pltpu.roll(prev, 1, 0), g1[0:SUBLANES])
    t2 = jnp.where(rows < 2, pltpu.roll(prev, 2, 0), g2[0:SUBLANES])
    act_ref[0:SUBLANES, :] = _glu(((cb + t2 * w0) + t1 * w1) + gate[0:SUBLANES] * w2,
                                  val[0:SUBLANES])
    last = gate[tm - SUBLANES:tm]
    carry_ref[j] = last
    gst_ref[...] = last


def _ffn_up_prompt(h2, w_b, cw, cb, d_ff, seq, *, tm, tn):
    m, d = h2.shape
    nj = d_ff // tn
    return pl.pallas_call(
        functools.partial(_ffn_up_prompt_kernel, tiles_per_seq=seq // tm),
        grid=(m // tm, nj),
        in_specs=[pl.BlockSpec((tm, d), lambda i, j: (i, 0)),
                  pl.BlockSpec((d, tn), lambda i, j: (0, j)),
                  pl.BlockSpec((d, tn), lambda i, j: (0, j + nj)),
                  pl.BlockSpec((3, tn), lambda i, j: (0, j)),
                  pl.BlockSpec((1, tn), lambda i, j: (0, j))],
        out_specs=[pl.BlockSpec((tm, tn), lambda i, j: (i, j)),
                   pl.BlockSpec((None, SUBLANES, tn), lambda i, j: (i, 0, j))],
        out_shape=[jax.ShapeDtypeStruct((m, d_ff), BF16),
                   jax.ShapeDtypeStruct((m // tm, SUBLANES, d_ff), F32)],
        scratch_shapes=[pltpu.VMEM((nj, SUBLANES, tn), F32)],
        compiler_params=_params(2),
        name="ffn_up_prompt",
    )(h2, w_b, w_b, cw, cb)


def _ffn_up_sample_kernel(h2_ref, wg_ref, wv_ref, cw_ref, cb_ref, st_ref, act_ref, nst_ref, *, nb, nt):
    gate = _dot(h2_ref[...], wg_ref[...])
    val = _dot(h2_ref[...], wv_ref[...])
    w0, w1, w2 = cw_ref[0:1, :], cw_ref[1:2, :], cw_ref[2:3, :]
    cb = cb_ref[...]
    ext = [st_ref[0:nb, :], st_ref[nb:2 * nb, :]] + [gate[t * nb:(t + 1) * nb] for t in range(nt)]
    for t in range(nt):
        gc = ((cb + ext[t] * w0) + ext[t + 1] * w1) + ext[t + 2] * w2
        act_ref[t * nb:(t + 1) * nb, :] = _glu(gc, val[t * nb:(t + 1) * nb])
    nst_ref[0:nb, :] = ext[nt]
    nst_ref[nb:2 * nb, :] = ext[nt + 1]


def _ffn_up_sample(h2, w_b, cw, cb, st, d_ff, nb, *, tn):
    m, d = h2.shape
    nj = d_ff // tn
    return pl.pallas_call(
        functools.partial(_ffn_up_sample_kernel, nb=nb, nt=m // nb),
        grid=(nj,),
        in_specs=[pl.BlockSpec((m, d), lambda j: (0, 0)),
                  pl.BlockSpec((d, tn), lambda j: (0, j)),
                  pl.BlockSpec((d, tn), lambda j: (0, j + nj)),
                  pl.BlockSpec((3, tn), lambda j: (0, j)),
                  pl.BlockSpec((1, tn), lambda j: (0, j)),
                  pl.BlockSpec((2 * nb, tn), lambda j: (0, j))],
        out_specs=[pl.BlockSpec((m, tn), lambda j: (0, j)),
                   pl.BlockSpec((2 * nb, tn), lambda j: (0, j))],
        out_shape=[jax.ShapeDtypeStruct((m, d_ff), BF16),
                   jax.ShapeDtypeStruct((2 * nb, d_ff), F32)],
        compiler_params=_params(1),
        name="ffn_up_sample",
    )(h2, w_b, w_b, cw, cb, st)


def _ffn_down_kernel(act_ref, w_ref, x1_ref, g_ref, y_ref, *, nk, final_norm):
    k = pl.program_id(1)

    @pl.when(k == 0)
    def _():
        y_ref[...] = x1_ref[...]

    y_ref[...] += _dot(act_ref[...], w_ref[...])

    if final_norm:
        @pl.when(k == nk - 1)
        def _():
            y_ref[...] = _rms_rows(y_ref[...], g_ref[...])


def _ffn_down(act, w_b, x1, g, final_norm, *, tm, tk):
    m, d = x1.shape
    nk = act.shape[1] // tk
    return pl.pallas_call(
        functools.partial(_ffn_down_kernel, nk=nk, final_norm=final_norm),
        grid=(m // tm, nk),
        in_specs=[pl.BlockSpec((tm, tk), lambda i, k: (i, k)),
                  pl.BlockSpec((tk, d), lambda i, k: (k, 0)),
                  pl.BlockSpec((tm, d), lambda i, k: (i, 0)),
                  pl.BlockSpec((1, d), lambda i, k: (0, 0))],
        out_specs=pl.BlockSpec((tm, d), lambda i, k: (i, 0)),
        out_shape=jax.ShapeDtypeStruct((m, d), F32),
        compiler_params=_params(2),
        name="ffn_down",
    )(act, w_b, x1, g)


def _softplus(x):
    return jnp.maximum(x, 0.0) + jnp.log1p(jnp.exp(-jnp.abs(x)))


def _rglru_coeffs(xc, wa_ref, ba_ref, wi_ref, bi_ref, lam_ref):
    xb = xc.astype(BF16)
    heads, blk = wa_ref.shape[0], wa_ref.shape[1]
    ga = jnp.concatenate([_dot(xb[:, h * blk:(h + 1) * blk], wa_ref[h]) for h in range(heads)], axis=1)
    gi = jnp.concatenate([_dot(xb[:, h * blk:(h + 1) * blk], wi_ref[h]) for h in range(heads)], axis=1)
    gate_r = jax.nn.sigmoid(ga + ba_ref[...])
    gate_i = jax.nn.sigmoid(gi + bi_ref[...])
    log_a = (-RG_C * gate_r) * _softplus(-lam_ref[...])
    a = jnp.exp(log_a)
    bx = jnp.sqrt(-jnp.tanh(log_a) * (a * a + 1.0)) * (gate_i * xc)
    return a, bx


def _rg_out(h, gate, g_ref):
    return _rms_rows(h * jax.nn.gelu(gate), g_ref[...]).astype(BF16)


def _conv_tile(x, prev, w_ref, b_ref, k_taps):
    rows = lax.broadcasted_iota(jnp.int32, (SUBLANES, x.shape[1]), 0)
    out = b_ref[...]
    top = b_ref[...]
    for k in range(k_taps):
        s = k_taps - 1 - k
        wk = w_ref[k:k + 1, :]
        if s == 0:
            r, t8 = x, x[0:SUBLANES]
        else:
            r = pltpu.roll(x, s, 0)
            t8 = jnp.where(rows < s, pltpu.roll(prev, s, 0), r[0:SUBLANES])
        out = out + r * wk
        top = top + t8 * wk
    return jnp.concatenate([top, out[SUBLANES:]], axis=0)


def _rglru_prompt_kernel(x_ref, gate_ref, cw_ref, cb_ref, wa_ref, ba_ref, wi_ref, bi_ref, lam_ref,
                         g_ref, out_ref, hlast_ref, xprev_ref, hc_ref, a_s, b_s, h_s, *, k_taps):
    c = pl.program_id(1)

    @pl.when(c == 0)
    def _():
        xprev_ref[...] = jnp.zeros_like(xprev_ref)
        hc_ref[...] = jnp.zeros_like(hc_ref)

    x = x_ref[...]
    t_len, width = x.shape
    xc = _conv_tile(x, xprev_ref[...], cw_ref, cb_ref, k_taps)
    xprev_ref[...] = x[t_len - SUBLANES:t_len]
    a, b = _rglru_coeffs(xc, wa_ref, ba_ref, wi_ref, bi_ref, lam_ref)
    pos = lax.broadcasted_iota(jnp.int32, (t_len, 1), 0) % SUBLANES
    for s in (1, 2, 4):
        a_sh = pltpu.roll(a, s, 0)
        b_sh = pltpu.roll(b, s, 0)
        m = pos >= s
        b = jnp.where(m, a * b_sh + b, b)
        a = jnp.where(m, a * a_sh, a)
    a_s[...] = a
    b_s[...] = b
    h = hc_ref[...]
    for grp in range(t_len // SUBLANES):
        lo, hi = grp * SUBLANES, (grp + 1) * SUBLANES
        hl = jnp.broadcast_to(h[SUBLANES - 1:SUBLANES, :], (SUBLANES, width))
        h = a_s[lo:hi, :] * hl + b_s[lo:hi, :]
        h_s[lo:hi, :] = h
    hc_ref[...] = h
    hlast_ref[...] = h
    out_ref[...] = _rg_out(h_s[...], gate_ref[...], g_ref)


def _rglru_prompt(proj, nb, seq, d_rnn, cw, cb, wa, ba, wi, bi, lam, g, *, tt):
    nc = seq // tt
    heads, blk = wa.shape[0], wa.shape[1]
    vec = pl.BlockSpec((1, d_rnn), lambda b, c: (0, 0))
    gate_w = pl.BlockSpec((heads, blk, blk), lambda b, c: (0, 0, 0))
    return pl.pallas_call(
        functools.partial(_rglru_prompt_kernel, k_taps=cw.shape[0]),
        grid=(nb, nc),
        in_specs=[pl.BlockSpec((tt, d_rnn), lambda b, c: (b * nc + c, 0)),
                  pl.BlockSpec((tt, d_rnn), lambda b, c: (b * nc + c, 1)),
                  pl.BlockSpec((cw.shape[0], d_rnn), lambda b, c: (0, 0)),
                  vec, gate_w, vec, gate_w, vec, vec, vec],
        out_specs=[pl.BlockSpec((tt, d_rnn), lambda b, c: (b * nc + c, 0)),
                   pl.BlockSpec((None, SUBLANES, d_rnn), lambda b, c: (b, 0, 0))],
        out_shape=[jax.ShapeDtypeStruct((nb * seq, d_rnn), BF16),
                   jax.ShapeDtypeStruct((nb, SUBLANES, d_rnn), F32)],
        scratch_shapes=[pltpu.VMEM((SUBLANES, d_rnn), F32), pltpu.VMEM((SUBLANES, d_rnn), F32),
                        pltpu.VMEM((tt, d_rnn), F32), pltpu.VMEM((tt, d_rnn), F32),
                        pltpu.VMEM((tt, d_rnn), F32)],
        compiler_params=_params(2),
        name="rglru_prompt",
    )(proj, proj, cw, cb, wa, ba, wi, bi, lam, g)


def _conv_slabs(st_ref, x_ref, w_ref, b_ref):
    k_taps = w_ref.shape[0]
    ext = [st_ref[s] for s in range(k_taps - 1)] + [x_ref[t] for t in range(x_ref.shape[0])]
    convs = []
    for t in range(x_ref.shape[0]):
        acc = b_ref[...]
        for k in range(k_taps):
            acc = acc + ext[t + k] * w_ref[k:k + 1, :]
        convs.append(acc)
    return jnp.concatenate(convs, axis=0)


def _rglru_sample_kernel(x_ref, gate_ref, st_ref, h0_ref, cw_ref, cb_ref, wa_ref, ba_ref, wi_ref,
                         bi_ref, lam_ref, g_ref, out_ref, hlast_ref, h_s):
    nt, bb, width = x_ref.shape
    xc = _conv_slabs(st_ref, x_ref, cw_ref, cb_ref)
    a, b = _rglru_coeffs(xc, wa_ref, ba_ref, wi_ref, bi_ref, lam_ref)
    h = h0_ref[...]
    for t in range(nt):
        h = a[t * bb:(t + 1) * bb] * h + b[t * bb:(t + 1) * bb]
        h_s[t * bb:(t + 1) * bb, :] = h
    hlast_ref[...] = h
    gate = gate_ref[...].reshape(nt * bb, width)
    out_ref[...] = _rg_out(h_s[...], gate, g_ref).reshape(nt, bb, width)


def _rglru_sample(proj3, st3, h0, d_rnn, cw, cb, wa, ba, wi, bi, lam, g, *, bb):
    nt, nb, _ = proj3.shape
    heads, blk = wa.shape[0], wa.shape[1]
    k_taps = cw.shape[0]
    vec = pl.BlockSpec((1, d_rnn), lambda i: (0, 0))
    gate_w = pl.BlockSpec((heads, blk, blk), lambda i: (0, 0, 0))
    return pl.pallas_call(
        _rglru_sample_kernel,
        grid=(nb // bb,),
        in_specs=[pl.BlockSpec((nt, bb, d_rnn), lambda i: (0, i, 0)),
                  pl.BlockSpec((nt, bb, d_rnn), lambda i: (0, i, 1)),
                  pl.BlockSpec((k_taps - 1, bb, d_rnn), lambda i: (0, i, 0)),
                  pl.BlockSpec((bb, d_rnn), lambda i: (i, 0)),
                  pl.BlockSpec((k_taps, d_rnn), lambda i: (0, 0)),
                  vec, gate_w, vec, gate_w, vec, vec, vec],
        out_specs=[pl.BlockSpec((nt, bb, d_rnn), lambda i: (0, i, 0)),
                   pl.BlockSpec((bb, d_rnn), lambda i: (i, 0))],
        out_shape=[jax.ShapeDtypeStruct((nt, nb, d_rnn), BF16),
                   jax.ShapeDtypeStruct((nb, d_rnn), F32)],
        scratch_shapes=[pltpu.VMEM((nt * bb, d_rnn), F32)],
        compiler_params=_params(1),
        name="rglru_sample",
    )(proj3, proj3, st3, h0, cw, cb, wa, ba, wi, bi, lam, g)


def _silu(x):
    return x * jax.nn.sigmoid(x)


def _group_norm_out(y, z, gn_ref, groups):
    u = y * _silu(z)
    gw = u.shape[1] // groups
    outs = []
    for g in range(groups):
        ug = u[:, g * gw:(g + 1) * gw]
        ms = jnp.mean(ug * ug, axis=-1, keepdims=True)
        outs.append((ug * lax.rsqrt(ms + EPS)) * gn_ref[:, g * gw:(g + 1) * gw])
    return jnp.concatenate(outs, axis=1).astype(BF16)


def _ssd_prompt_kernel(z_ref, xbc_ref, dt_ref, cw_ref, cb_ref, dtb_ref, alog_ref, dexp_ref, gn_ref,
                       e_ref, out_ref, hfin_ref, carry_ref, s_ref, y_s,
                       *, k_taps, d_ssm, n_state, groups, headdim, nc):
    c = pl.program_id(1)

    @pl.when(c == 0)
    def _():
        carry_ref[...] = jnp.zeros_like(carry_ref)
        s_ref[...] = jnp.zeros_like(s_ref)

    xbc = xbc_ref[...]
    q = xbc.shape[0]
    xc = _silu(_conv_tile(xbc, carry_ref[...], cw_ref, cb_ref, k_taps))
    carry_ref[...] = xbc[q - SUBLANES:q]
    gn_w = groups * n_state
    xs = xc[:, :d_ssm]
    bm = xc[:, d_ssm:d_ssm + gn_w]
    cm = xc[:, d_ssm + gn_w:d_ssm + 2 * gn_w]

    dt = _softplus(dt_ref[...] + dtb_ref[...])
    dta = dt * (-jnp.exp(alog_ref[...]))
    row = lax.broadcasted_iota(jnp.int32, (q, q), 0)
    col = lax.broadcasted_iota(jnp.int32, (q, q), 1)
    causal = row >= col
    tril = jnp.where(causal, 1.0, 0.0).astype(BF16)
    cum = _dot_sel_left(tril, dta)
    cum_t = cum.T
    e = e_ref[...]
    dt_e = _dot_sel_right(dt, e)
    cum_e = _dot_sel_right(cum, e)
    ecum_e = jnp.exp(cum_e)
    xdt = xs * dt_e
    cum_last = cum_e[q - 1:q, :]
    xdd = (xdt * jnp.exp(cum_last - cum_e)).astype(BF16)
    chunk_decay = jnp.exp(cum_last)

    heads_per_group = d_ssm // headdim // groups
    pair_w = 2 * headdim
    gw = heads_per_group * headdim
    lane = lax.broadcasted_iota(jnp.int32, (q, pair_w), 1)
    for g in range(groups):
        cg = cm[:, g * n_state:(g + 1) * n_state].astype(BF16)
        bg32 = bm[:, g * n_state:(g + 1) * n_state]
        bg = bg32.astype(BF16)
        cb_g = lax.dot_general(cg, bg, (((1,), (1,)), ((), ())), preferred_element_type=F32)
        s_old = s_ref[:, g * gw:(g + 1) * gw]
        y_s[:, g * gw:(g + 1) * gw] = _dot(cg, s_old.astype(BF16)) * ecum_e[:, g * gw:(g + 1) * gw]
        st_g = _dot(bg32.T.astype(BF16), xdd[:, g * gw:(g + 1) * gw])
        s_ref[:, g * gw:(g + 1) * gw] = s_old * chunk_decay[:, g * gw:(g + 1) * gw] + st_g
        for pr in range(heads_per_group // 2):
            h0 = g * heads_per_group + 2 * pr
            ms = []
            for h in (h0, h0 + 1):
                seg = (jnp.broadcast_to(cum[:, h:h + 1], (q, q))
                       - jnp.broadcast_to(cum_t[h:h + 1, :], (q, q)))
                ms.append(cb_g * jnp.exp(jnp.where(causal, seg, -jnp.inf)))
            lo_c = h0 * headdim
            xp = xdt[:, lo_c:lo_c + pair_w]
            lhs = jnp.concatenate(ms, axis=1).astype(BF16)
            rhs = jnp.concatenate([jnp.where(lane < headdim, xp, 0.0),
                                   jnp.where(lane >= headdim, xp, 0.0)], axis=0).astype(BF16)
            y_s[:, lo_c:lo_c + pair_w] += _dot(lhs, rhs)

    y = y_s[...] + dexp_ref[...] * xs
    out_ref[...] = _group_norm_out(y, z_ref[...], gn_ref, groups)

    @pl.when(c == nc - 1)
    def _():
        for blk in range(d_ssm // LANES):
            hfin_ref[blk * LANES:(blk + 1) * LANES, :] = s_ref[:, blk * LANES:(blk + 1) * LANES].T


def _ssd_prompt(proj, dt_raw, nb, seq, d_ssm, d_xbc, n_state, headdim, cw, cb, dtb, alog, dexp, gn, e):
    q = SSD_CHUNK
    nc = seq // q
    z_blk = (proj.shape[1] - d_xbc - d_ssm) // d_ssm
    x_blk = (proj.shape[1] - d_xbc) // d_xbc
    assert z_blk * d_ssm + d_ssm + d_xbc == proj.shape[1] and x_blk * d_xbc + d_xbc == proj.shape[1]
    one = lambda w: pl.BlockSpec((1, w), lambda b, c: (0, 0))
    return pl.pallas_call(
        functools.partial(_ssd_prompt_kernel, k_taps=cw.shape[0], d_ssm=d_ssm, n_state=n_state,
                          groups=SSD_GROUPS, headdim=headdim, nc=nc),
        grid=(nb, nc),
        in_specs=[pl.BlockSpec((q, d_ssm), lambda b, c: (b * nc + c, z_blk)),
                  pl.BlockSpec((q, d_xbc), lambda b, c: (b * nc + c, x_blk)),
                  pl.BlockSpec((q, LANES), lambda b, c: (b * nc + c, 0)),
                  pl.BlockSpec((cw.shape[0], d_xbc), lambda b, c: (0, 0)),
                  one(d_xbc), one(LANES), one(LANES), one(d_ssm), one(d_ssm),
                  pl.BlockSpec((LANES, d_ssm), lambda b, c: (0, 0))],
        out_specs=[pl.BlockSpec((q, d_ssm), lambda b, c: (b * nc + c, 0)),
                   pl.BlockSpec((None, d_ssm, n_state), lambda b, c: (b, 0, 0))],
        out_shape=[jax.ShapeDtypeStruct((nb * seq, d_ssm), BF16),
                   jax.ShapeDtypeStruct((nb, d_ssm, n_state), F32)],
        scratch_shapes=[pltpu.VMEM((SUBLANES, d_xbc), F32), pltpu.VMEM((n_state, d_ssm), F32),
                        pltpu.VMEM((q, d_ssm), F32)],
        compiler_params=_params(2),
        name="ssd_prompt",
    )(proj, proj, dt_raw, cw, cb, dtb, alog, dexp, gn, e)


def _ssd_sample_pre_kernel(xbc_ref, st_ref, dt_ref, cw_ref, cb_ref, dtb_ref, alog_ref, dexp_ref,
                           e_ref, gsel_ref, ydp_ref, ecum_ref, xdd_ref, bc_ref, cd_ref,
                           *, d_ssm, n_state, groups):
    nt, bb, _ = xbc_ref.shape
    xc = _silu(_conv_slabs(st_ref, xbc_ref, cw_ref, cb_ref))
    gn_w = groups * n_state
    xs = xc[:, :d_ssm]
    bm = xc[:, d_ssm:d_ssm + gn_w]
    cm = xc[:, d_ssm + gn_w:d_ssm + 2 * gn_w]
    bc_ref[...] = xc[:, d_ssm:d_ssm + 2 * gn_w].reshape(nt, bb, 2 * gn_w)

    dt = _softplus(dt_ref[...].reshape(nt * bb, LANES) + dtb_ref[...])
    dta = dt * (-jnp.exp(alog_ref[...]))
    cums = [dta[0:bb]]
    for t in range(1, nt):
        cums.append(cums[-1] + dta[t * bb:(t + 1) * bb])
    cum = jnp.concatenate(cums, axis=0)
    cd_ref[...] = jnp.exp(cums[-1])
    e = e_ref[...]
    dt_e = _dot_sel_right(dt, e)
    cum_e = _dot_sel_right(cum, e)
    ecum_ref[...] = jnp.exp(cum_e).reshape(nt, bb, d_ssm)
    xdt = xs * dt_e
    sl = lambda v, t: v[t * bb:(t + 1) * bb]
    cum_last = sl(cum_e, nt - 1)
    gsel = gsel_ref[...]
    for t in range(nt):
        xdd_ref[t] = sl(xdt, t) * jnp.exp(cum_last - sl(cum_e, t))
        acc = dexp_ref[...] * sl(xs, t)
        for s in range(t + 1):
            cb_e = _dot_sel_right(sl(cm, t) * sl(bm, s), gsel)
            acc = acc + (cb_e * jnp.exp(sl(cum_e, t) - sl(cum_e, s))) * sl(xdt, s)
        ydp_ref[t] = acc


def _ssd_sample_pre(proj3, st3, dt3, d_ssm, d_xbc, n_state, cw, cb, dtb, alog, dexp, e, gsel, *, bb):
    nt, nb, n_main = proj3.shape
    k_taps = cw.shape[0]
    x_blk = (n_main - d_xbc) // d_xbc
    gn_w = SSD_GROUPS * n_state
    full = lambda r, w: pl.BlockSpec((r, w), lambda i: (0, 0))
    slab = lambda n, w: pl.BlockSpec((n, bb, w), lambda i: (0, i, 0))
    f32 = lambda *s: jax.ShapeDtypeStruct(s, F32)
    return pl.pallas_call(
        functools.partial(_ssd_sample_pre_kernel, d_ssm=d_ssm, n_state=n_state, groups=SSD_GROUPS),
        grid=(nb // bb,),
        in_specs=[pl.BlockSpec((nt, bb, d_xbc), lambda i: (0, i, x_blk)),
                  slab(k_taps - 1, d_xbc), slab(nt, LANES), full(k_taps, d_xbc),
                  full(1, d_xbc), full(1, LANES), full(1, LANES), full(1, d_ssm),
                  full(LANES, d_ssm), full(gn_w, d_ssm)],
        out_specs=[slab(nt, d_ssm), slab(nt, d_ssm), slab(nt, d_ssm), slab(nt, 2 * gn_w),
                   pl.BlockSpec((bb, LANES), lambda i: (i, 0))],
        out_shape=[f32(nt, nb, d_ssm), f32(nt, nb, d_ssm), f32(nt, nb, d_ssm),
                   f32(nt, nb, 2 * gn_w), f32(nb, LANES)],
        compiler_params=_params(1),
        name="ssd_sample_pre",
    )(proj3, st3, dt3, cw, cb, dtb, alog, dexp, e, gsel)


def _ssd_sample_state_kernel(s0_ref, c_ref, b_ref, xdd_ref, cd_ref, snew_ref, yoff_ref,
                             *, bt, groups, n_state, headdim):
    heads = s0_ref.shape[1]
    hpg = heads // groups
    gw = hpg * headdim

    def body(b, carry):
        cd_row = cd_ref[pl.ds(b, 1), :]
        cb = c_ref[b].astype(BF16)
        bb = b_ref[b].astype(BF16)
        xb = xdd_ref[b].astype(BF16)
        for g in range(groups):
            s0 = s0_ref[b, g * hpg:(g + 1) * hpg].reshape(gw, n_state)
            yoff = lax.dot_general(cb[:, g * n_state:(g + 1) * n_state], s0.astype(BF16),
                                   (((1,), (1,)), ((), ())), preferred_element_type=F32)
            yoff_ref[b, :, g * gw:(g + 1) * gw] = yoff
            upd = lax.dot_general(xb[:, g * gw:(g + 1) * gw], bb[:, g * n_state:(g + 1) * n_state],
                                  (((0,), (0,)), ((), ())), preferred_element_type=F32)
            dec = jnp.concatenate(
                [jnp.broadcast_to(cd_row[:, h:h + 1], (headdim, n_state))
                 for h in range(g * hpg, (g + 1) * hpg)], axis=0)
            snew_ref[b, g * hpg:(g + 1) * hpg] = (s0 * dec + upd).reshape(hpg, headdim, n_state)
        return carry

    lax.fori_loop(0, bt, body, 0)


def _ssd_sample_state(s0, c_b, b_b, xdd_b, cd, *, bt):
    nb, heads, headdim, n_state = s0.shape
    nt = c_b.shape[1]
    gn_w = c_b.shape[2]
    d_ssm = xdd_b.shape[2]
    return pl.pallas_call(
        functools.partial(_ssd_sample_state_kernel, bt=bt, groups=SSD_GROUPS, n_state=n_state,
                          headdim=headdim),
        grid=(nb // bt,),
        in_specs=[pl.BlockSpec((bt, heads, headdim, n_state), lambda i: (i, 0, 0, 0)),
                  pl.BlockSpec((bt, nt, gn_w), lambda i: (i, 0, 0)),
                  pl.BlockSpec((bt, nt, gn_w), lambda i: (i, 0, 0)),
                  pl.BlockSpec((bt, nt, d_ssm), lambda i: (i, 0, 0)),
                  pl.BlockSpec((bt, LANES), lambda i: (i, 0))],
        out_specs=[pl.BlockSpec((bt, heads, headdim, n_state), lambda i: (i, 0, 0, 0)),
                   pl.BlockSpec((bt, nt, d_ssm), lambda i: (i, 0, 0))],
        out_shape=[jax.ShapeDtypeStruct(s0.shape, F32),
                   jax.ShapeDtypeStruct((nb, nt, d_ssm), F32)],
        compiler_params=_params(1),
        name="ssd_sample_state",
    )(s0, c_b, b_b, xdd_b, cd)


def _ssd_sample_post_kernel(ydp_ref, yoff_ref, ecum_ref, z_ref, gn_ref, out_ref, *, groups):
    nt, bb, d_ssm = ydp_ref.shape
    y = (ydp_ref[...] + yoff_ref[...] * ecum_ref[...]).reshape(nt * bb, d_ssm)
    z = z_ref[...].reshape(nt * bb, d_ssm)
    out_ref[...] = _group_norm_out(y, z, gn_ref, groups).reshape(nt, bb, d_ssm)


def _ssd_sample_post(ydp, yoff, ecum, proj3, gn, d_ssm, d_xbc, *, bb):
    nt, nb, n_main = proj3.shape
    z_blk = (n_main - d_xbc - d_ssm) // d_ssm
    slab = pl.BlockSpec((nt, bb, d_ssm), lambda i: (0, i, 0))
    return pl.pallas_call(
        functools.partial(_ssd_sample_post_kernel, groups=SSD_GROUPS),
        grid=(nb // bb,),
        in_specs=[slab, slab, slab, pl.BlockSpec((nt, bb, d_ssm), lambda i: (0, i, z_blk)),
                  pl.BlockSpec((1, d_ssm), lambda i: (0, 0))],
        out_specs=slab,
        out_shape=jax.ShapeDtypeStruct((nt, nb, d_ssm), BF16),
        compiler_params=_params(1),
        name="ssd_sample_post",
    )(ydp, yoff, ecum, proj3, gn)


def _pad_lanes(v):
    return jnp.pad(v.astype(F32), (0, LANES - v.shape[0])).reshape(1, LANES)


def kernel(x_prompt, x_sample, state_rglru_h, state_rglru_conv, state_ssd_h, state_ssd_conv, state_ffn_conv, g_mix, w_in, rg_conv_w, rg_conv_b, rg_gate_a_w, rg_gate_a_b, rg_gate_i_w, rg_gate_i_b, rg_lambda, g_rg_out, ssd_conv_w, ssd_conv_b, ssd_dt_bias, ssd_A_log, ssd_D, g_ssd_norm, w_out, g_ffn, w_ffn_up, ffn_conv_w, ffn_conv_b, w_ffn_down, g_final):
    depth = g_mix.shape[0]
    pb, seq, d_model = x_prompt.shape
    sb, dec_seq, _ = x_sample.shape
    d_rnn = rg_lambda.shape[1]
    d_ssm = g_ssd_norm.shape[1]
    d_xbc = ssd_conv_w.shape[2]
    d_ff = ffn_conv_w.shape[2]
    heads = ssd_A_log.shape[1]
    headdim = d_ssm // heads
    n_state = state_ssd_h.shape[-1]
    n_main = 2 * d_rnn + d_ssm + d_xbc
    k_rg = rg_conv_w.shape[1]
    k_ssd = ssd_conv_w.shape[1]
    k_ffn = ffn_conv_w.shape[1]
    gn_w = SSD_GROUPS * n_state
    assert heads <= LANES and n_state == LANES and k_ffn == 3 and dec_seq >= max(k_rg, k_ssd) - 1

    hid = lax.broadcasted_iota(jnp.int32, (LANES, d_ssm), 0)
    cid = lax.broadcasted_iota(jnp.int32, (LANES, d_ssm), 1)
    e_sel = (hid == cid // headdim).astype(BF16)
    gid = lax.broadcasted_iota(jnp.int32, (gn_w, d_ssm), 0) // n_state
    cgid = lax.broadcasted_iota(jnp.int32, (gn_w, d_ssm), 1) // (d_ssm // SSD_GROUPS)
    g_sel = (gid == cgid).astype(BF16)

    yp = x_prompt.reshape(pb * seq, d_model)
    ys = jnp.transpose(x_sample, (1, 0, 2)).reshape(dec_seq * sb, d_model)
    p_new = [[], [], [], [], []]
    s_new = [[], [], [], [], []]
    row = lambda v: v.reshape(1, -1)
    swap = lambda a3: jnp.transpose(a3, (1, 0, 2))
    bb = 32

    for l in range(depth):
        w_in_b = w_in[l].astype(BF16)
        wdt_b = jnp.pad(w_in[l][:, n_main:], ((0, 0), (0, LANES - heads))).astype(BF16)
        w_out_b = w_out[l].astype(BF16)
        w_up_b = w_ffn_up[l].astype(BF16)
        w_down_b = w_ffn_down[l].astype(BF16)
        wa_b = rg_gate_a_w[l].astype(BF16)
        wi_b = rg_gate_i_w[l].astype(BF16)
        rg_args = (rg_conv_w[l], row(rg_conv_b[l]), wa_b, row(rg_gate_a_b[l]), wi_b,
                   row(rg_gate_i_b[l]), row(rg_lambda[l]), row(g_rg_out[l]))
        dtb = _pad_lanes(ssd_dt_bias[l])
        alog = _pad_lanes(ssd_A_log[l])
        dexp = jnp.repeat(ssd_D[l].astype(F32), headdim).reshape(1, d_ssm)
        gn = row(g_ssd_norm[l])

        proj_p, dt_p = _in_proj(yp, row(g_mix[l]), w_in_b, wdt_b, n_main, tm=512, tn=512)
        rg_p, hl_p = _rglru_prompt(proj_p, pb, seq, d_rnn, *rg_args, tt=256)
        ssd_p, hfin_p = _ssd_prompt(proj_p, dt_p, pb, seq, d_ssm, d_xbc, n_state, headdim,
                                    ssd_conv_w[l], row(ssd_conv_b[l]), dtb, alog, dexp, gn, e_sel)
        x1_p, h2_p = _out_proj(rg_p, ssd_p, w_out_b, yp, row(g_ffn[l]), tm=512, tn=512)
        act_p, gst_p = _ffn_up_prompt(h2_p, w_up_b, ffn_conv_w[l], row(ffn_conv_b[l]), d_ff, seq,
                                      tm=1024, tn=512)
        proj_p3 = proj_p.reshape(pb, seq, n_main)
        p_new[0].append(hl_p[:, SUBLANES - 1, :])
        p_new[1].append(proj_p3[:, seq - (k_rg - 1):, :d_rnn])
        p_new[2].append(hfin_p.reshape(pb, heads, headdim, n_state))
        p_new[3].append(proj_p3[:, seq - (k_ssd - 1):, n_main - d_xbc:])
        tiles = seq // 1024
        p_new[4].append(gst_p[tiles - 1::tiles, SUBLANES - (k_ffn - 1):, :])

        ms = ys.shape[0]
        proj_s, dt_s = _in_proj(ys, row(g_mix[l]), w_in_b, wdt_b, n_main, tm=ms, tn=512)
        proj_s3 = proj_s.reshape(dec_seq, sb, n_main)
        rg_s, hl_s = _rglru_sample(proj_s3, swap(state_rglru_conv[l]), state_rglru_h[l], d_rnn,
                                   *rg_args, bb=bb)
        ydp, ecum, xdd, bc, cd = _ssd_sample_pre(
            proj_s3, swap(state_ssd_conv[l]), dt_s.reshape(dec_seq, sb, LANES), d_ssm, d_xbc, n_state,
            ssd_conv_w[l], row(ssd_conv_b[l]), dtb, alog, dexp, e_sel, g_sel, bb=bb)
        bc_b = swap(bc)
        snew, yoff_b = _ssd_sample_state(state_ssd_h[l], bc_b[:, :, gn_w:], bc_b[:, :, :gn_w],
                                         swap(xdd), cd, bt=8)
        ssd_s = _ssd_sample_post(ydp, swap(yoff_b), ecum, proj_s3, gn, d_ssm, d_xbc, bb=bb)
        x1_s, h2_s = _out_proj(rg_s.reshape(ms, d_rnn), ssd_s.reshape(ms, d_ssm), w_out_b, ys,
                               row(g_ffn[l]), tm=ms, tn=512)
        act_s, gst_s = _ffn_up_sample(h2_s, w_up_b, ffn_conv_w[l], row(ffn_conv_b[l]),
                                      swap(state_ffn_conv[l]).reshape(-1, d_ff), d_ff, sb, tn=512)
        s_new[0].append(hl_s)
        s_new[1].append(swap(proj_s3[dec_seq - (k_rg - 1):, :, :d_rnn]))
        s_new[2].append(snew)
        s_new[3].append(swap(proj_s3[dec_seq - (k_ssd - 1):, :, n_main - d_xbc:]))
        s_new[4].append(swap(gst_s.reshape(k_ffn - 1, sb, d_ff)))

        last = l == depth - 1
        yp = _ffn_down(act_p, w_down_b, x1_p, row(g_final), last, tm=512, tk=512)
        ys = _ffn_down(act_s, w_down_b, x1_s, row(g_final), last, tm=ms, tk=512)

    y_prompt = yp.reshape(pb, seq, d_model)
    y_sample = jnp.transpose(ys.reshape(dec_seq, sb, d_model), (1, 0, 2))
    return (y_prompt, y_sample,
            jnp.stack(p_new[0]), jnp.stack(p_new[1]), jnp.stack(p_new[2]), jnp.stack(p_new[3]),
            jnp.stack(p_new[4]),
            jnp.stack(s_new[0]), jnp.stack(s_new[1]), jnp.stack(s_new[2]), jnp.stack(s_new[3]),
            jnp.stack(s_new[4]))
```

```python
import functools

import jax
import jax.numpy as jnp
from jax import lax
from jax.experimental import pallas as pl
from jax.experimental.pallas import tpu as pltpu

F32 = jnp.float32
BF16 = jnp.bfloat16

EPS = 1e-6
RG_C = 8.0
SSD_CHUNK = 128
SSD_GROUPS = 4
LANES = 128
SUBLANES = 8
VMEM_LIMIT = 56 * 2**20

TM = 512
TM_UP = 1024
TN_IN = 1024
TN_OUT = 1024
TN_UP = 512
TN_DOWN = 256


def _params(n_axes):
    return pltpu.CompilerParams(dimension_semantics=("arbitrary",) * n_axes,
                                vmem_limit_bytes=VMEM_LIMIT)


def _dot(a, b):
    return jnp.dot(a, b, preferred_element_type=F32)


def _rms_rows(x, g):
    ms = jnp.mean(x * x, axis=-1, keepdims=True)
    return (x * lax.rsqrt(ms + EPS)) * g


def _split3(x):
    hi = x.astype(BF16)
    r = x - hi.astype(F32)
    mid = r.astype(BF16)
    lo = (r - mid.astype(F32)).astype(BF16)
    return hi, mid, lo


def _dot_sel_right(x, sel):
    hi, mid, lo = _split3(x)
    return (_dot(hi, sel) + _dot(mid, sel)) + _dot(lo, sel)


def _dot_sel_left(sel, x):
    hi, mid, lo = _split3(x)
    return (_dot(sel, hi) + _dot(sel, mid)) + _dot(sel, lo)


def _inproj_kernel(x_ref, g_ref, w_ref, wdt_ref, o_ref, dt_ref, hn_ref):
    @pl.when(pl.program_id(1) == 0)
    def _():
        hn = _rms_rows(x_ref[...], g_ref[...]).astype(BF16)
        hn_ref[...] = hn
        dt_ref[...] = _dot(hn, wdt_ref[...])

    o_ref[...] = _dot(hn_ref[...], w_ref[...])


def _col_tiles(w, tn):
    k, n = w.shape
    return jnp.transpose(w.astype(BF16).reshape(k, n // tn, tn), (1, 0, 2))


def _in_proj(x2d, g, w_t, wdt_b, *, tm):
    m, d = x2d.shape
    nj, _, tn = w_t.shape
    return pl.pallas_call(
        _inproj_kernel,
        grid=(m // tm, nj),
        in_specs=[pl.BlockSpec((tm, d), lambda i, j: (i, 0)),
                  pl.BlockSpec((1, d), lambda i, j: (0, 0)),
                  pl.BlockSpec((None, d, tn), lambda i, j: (j, 0, 0)),
                  pl.BlockSpec((d, LANES), lambda i, j: (0, 0))],
        out_specs=[pl.BlockSpec((tm, tn), lambda i, j: (i, j)),
                   pl.BlockSpec((tm, LANES), lambda i, j: (i, 0))],
        out_shape=[jax.ShapeDtypeStruct((m, nj * tn), F32),
                   jax.ShapeDtypeStruct((m, LANES), F32)],
        scratch_shapes=[pltpu.VMEM((tm, d), BF16)],
        compiler_params=_params(2),
        name="in_proj",
    )(x2d, g, w_t, wdt_b)


def _outproj_kernel(rg_ref, ssd_ref, wt_ref, wb_ref, x_ref, g_ref, x1_ref, h2_ref, row_ref, *, nj, tn):
    j = pl.program_id(1)
    x1 = x_ref[...] + (_dot(rg_ref[...], wt_ref[...]) + _dot(ssd_ref[...], wb_ref[...]))
    x1_ref[...] = x1
    row_ref[j] = x1

    @pl.when(j == nj - 1)
    def _():
        ss = jnp.sum(row_ref[0] * row_ref[0], axis=-1, keepdims=True)
        for jj in range(1, nj):
            ss = ss + jnp.sum(row_ref[jj] * row_ref[jj], axis=-1, keepdims=True)
        inv = lax.rsqrt(ss / (nj * tn) + EPS)
        for jj in range(nj):
            h2_ref[:, jj * tn:(jj + 1) * tn] = (
                (row_ref[jj] * inv) * g_ref[:, jj * tn:(jj + 1) * tn]).astype(BF16)


def _out_proj(rg, ssd, w_t, x2d, g, *, tm):
    m, d = x2d.shape
    k_half = rg.shape[1]
    nj, _, tn = w_t.shape
    return pl.pallas_call(
        functools.partial(_outproj_kernel, nj=nj, tn=tn),
        grid=(m // tm, nj),
        in_specs=[pl.BlockSpec((tm, k_half), lambda i, j: (i, 0)),
                  pl.BlockSpec((tm, k_half), lambda i, j: (i, 0)),
                  pl.BlockSpec((None, k_half, tn), lambda i, j: (j, 0, 0)),
                  pl.BlockSpec((None, k_half, tn), lambda i, j: (j, 1, 0)),
                  pl.BlockSpec((tm, tn), lambda i, j: (i, j)),
                  pl.BlockSpec((1, d), lambda i, j: (0, 0))],
        out_specs=[pl.BlockSpec((tm, tn), lambda i, j: (i, j)),
                   pl.BlockSpec((tm, d), lambda i, j: (i, 0))],
        out_shape=[jax.ShapeDtypeStruct((m, d), F32),
                   jax.ShapeDtypeStruct((m, d), BF16)],
        scratch_shapes=[pltpu.VMEM((nj, tm, tn), F32)],
        compiler_params=_params(2),
        name="out_proj",
    )(rg, ssd, w_t, w_t, x2d, g)


def _glu(gate_c, val):
    return (jax.nn.gelu(gate_c) * val).astype(BF16)


def _ffn_up_prompt_kernel(h2_ref, wg_ref, wv_ref, cw_ref, cb_ref, act_ref, gst_ref, carry_ref,
                          *, tiles_per_seq):
    i = pl.program_id(0)
    j = pl.program_id(1)
    gate = _dot(h2_ref[...], wg_ref[...])
    val = _dot(h2_ref[...], wv_ref[...])
    tm, tn = gate.shape
    w0, w1, w2 = cw_ref[0:1, :], cw_ref[1:2, :], cw_ref[2:3, :]
    cb = cb_ref[...]
    g1 = pltpu.roll(gate, 1, 0)
    g2 = pltpu.roll(gate, 2, 0)
    act_ref[...] = _glu(((cb + g2 * w0) + g1 * w1) + gate * w2, val)
    @pl.when((i % tiles_per_seq) == 0)
    def _():
        carry_ref[j] = jnp.zeros((SUBLANES, tn), F32)

    prev = carry_ref[j]
    rows = lax.broadcasted_iota(jnp.int32, (SUBLANES, tn), 0)
    t1 = jnp.where(rows < 1, pltpu.roll(prev, 1, 0), g1[0:SUBLANES])
    t2 = jnp.where(rows < 2, pltpu.roll(prev, 2, 0), g2[0:SUBLANES])
    act_ref[0:SUBLANES, :] = _glu(((cb + t2 * w0) + t1 * w1) + gate[0:SUBLANES] * w2,
                                  val[0:SUBLANES])
    last = gate[tm - SUBLANES:tm]
    carry_ref[j] = last
    gst_ref[...] = last


def _ffn_up_prompt(h2, w_t, cw, cb, seq, *, tm):
    m, d = h2.shape
    nj, tn = w_t.shape[0] // 2, w_t.shape[2]
    d_ff = nj * tn
    return pl.pallas_call(
        functools.partial(_ffn_up_prompt_kernel, tiles_per_seq=seq // tm),
        grid=(m // tm, nj),
        in_specs=[pl.BlockSpec((tm, d), lambda i, j: (i, 0)),
                  pl.BlockSpec((None, d, tn), lambda i, j: (j, 0, 0)),
                  pl.BlockSpec((None, d, tn), lambda i, j: (j + nj, 0, 0)),
                  pl.BlockSpec((3, tn), lambda i, j: (0, j)),
                  pl.BlockSpec((1, tn), lambda i, j: (0, j))],
        out_specs=[pl.BlockSpec((tm, tn), lambda i, j: (i, j)),
                   pl.BlockSpec((None, SUBLANES, tn), lambda i, j: (i, 0, j))],
        out_shape=[jax.ShapeDtypeStruct((m, d_ff), BF16),
                   jax.ShapeDtypeStruct((m // tm, SUBLANES, d_ff), F32)],
        scratch_shapes=[pltpu.VMEM((nj, SUBLANES, tn), F32)],
        compiler_params=_params(2),
        name="ffn_up_prompt",
    )(h2, w_t, w_t, cw, cb)


def _ffn_up_sample_kernel(h2_ref, wg_ref, wv_ref, cw_ref, cb_ref, st_ref, act_ref, nst_ref, *, nb, nt):
    gate = _dot(h2_ref[...], wg_ref[...])
    val = _dot(h2_ref[...], wv_ref[...])
    w0, w1, w2 = cw_ref[0:1, :], cw_ref[1:2, :], cw_ref[2:3, :]
    cb = cb_ref[...]
    ext = [st_ref[0:nb, :], st_ref[nb:2 * nb, :]] + [gate[t * nb:(t + 1) * nb] for t in range(nt)]
    for t in range(nt):
        gc = ((cb + ext[t] * w0) + ext[t + 1] * w1) + ext[t + 2] * w2
        act_ref[t * nb:(t + 1) * nb, :] = _glu(gc, val[t * nb:(t + 1) * nb])
    nst_ref[0:nb, :] = ext[nt]
    nst_ref[nb:2 * nb, :] = ext[nt + 1]


def _ffn_up_sample(h2, w_t, cw, cb, st, nb):
    m, d = h2.shape
    nj, tn = w_t.shape[0] // 2, w_t.shape[2]
    d_ff = nj * tn
    return pl.pallas_call(
        functools.partial(_ffn_up_sample_kernel, nb=nb, nt=m // nb),
        grid=(nj,),
        in_specs=[pl.BlockSpec((m, d), lambda j: (0, 0)),
                  pl.BlockSpec((None, d, tn), lambda j: (j, 0, 0)),
                  pl.BlockSpec((None, d, tn), lambda j: (j + nj, 0, 0)),
                  pl.BlockSpec((3, tn), lambda j: (0, j)),
                  pl.BlockSpec((1, tn), lambda j: (0, j)),
                  pl.BlockSpec((2 * nb, tn), lambda j: (0, j))],
        out_specs=[pl.BlockSpec((m, tn), lambda j: (0, j)),
                   pl.BlockSpec((2 * nb, tn), lambda j: (0, j))],
        out_shape=[jax.ShapeDtypeStruct((m, d_ff), BF16),
                   jax.ShapeDtypeStruct((2 * nb, d_ff), F32)],
        compiler_params=_params(1),
        name="ffn_up_sample",
    )(h2, w_t, w_t, cw, cb, st)


def _ffn_down_kernel(act_ref, w_ref, x1_ref, g_ref, y_hbm, rows_ref, sem, *, ni, nj, tm, tn, final_norm):
    i = pl.program_id(0)
    j = pl.program_id(1)

    def rows_out(tile):
        return pltpu.make_async_copy(rows_ref, y_hbm.at[pl.ds(tile * tm, tm)], sem)

    res = x1_ref[...] + _dot(act_ref[...], w_ref[...])

    @pl.when((j == 0) & (i > 0))
    def _():
        rows_out(i - 1).wait()

    rows_ref[:, pl.ds(pl.multiple_of(j * tn, tn), tn)] = res

    @pl.when(j == nj - 1)
    def _():
        if final_norm:
            rows_ref[...] = _rms_rows(rows_ref[...], g_ref[...])
        rows_out(i).start()

        @pl.when(i == ni - 1)
        def _():
            rows_out(i).wait()


def _ffn_down(act, w_t, x1, g, final_norm, *, tm):
    m, d = x1.shape
    nj, d_ff, tn = w_t.shape
    ni = m // tm
    return pl.pallas_call(
        functools.partial(_ffn_down_kernel, ni=ni, nj=nj, tm=tm, tn=tn, final_norm=final_norm),
        grid=(ni, nj),
        in_specs=[pl.BlockSpec((tm, d_ff), lambda i, j: (i, 0)),
                  pl.BlockSpec((None, d_ff, tn), lambda i, j: (j, 0, 0)),
                  pl.BlockSpec((tm, tn), lambda i, j: (i, j)),
                  pl.BlockSpec((1, d), lambda i, j: (0, 0))],
        out_specs=pl.BlockSpec(memory_space=pl.ANY),
        out_shape=jax.ShapeDtypeStruct((m, d), F32),
        scratch_shapes=[pltpu.VMEM((tm, d), F32), pltpu.SemaphoreType.DMA(())],
        compiler_params=_params(2),
        name="ffn_down",
    )(act, w_t, x1, g)


def _softplus(x):
    return jnp.maximum(x, 0.0) + jnp.log1p(jnp.exp(-jnp.abs(x)))


def _rglru_coeffs(xc, wa_ref, ba_ref, wi_ref, bi_ref, lam_ref):
    xb = xc.astype(BF16)
    heads, blk = wa_ref.shape[0], wa_ref.shape[1]
    ga = jnp.concatenate([_dot(xb[:, h * blk:(h + 1) * blk], wa_ref[h]) for h in range(heads)], axis=1)
    gi = jnp.concatenate([_dot(xb[:, h * blk:(h + 1) * blk], wi_ref[h]) for h in range(heads)], axis=1)
    gate_r = jax.nn.sigmoid(ga + ba_ref[...])
    gate_i = jax.nn.sigmoid(gi + bi_ref[...])
    log_a = (-RG_C * gate_r) * _softplus(-lam_ref[...])
    a = jnp.exp(log_a)
    bx = jnp.sqrt(-jnp.tanh(log_a) * (a * a + 1.0)) * (gate_i * xc)
    return a, bx


def _rg_out(h, gate, g_ref):
    return _rms_rows(h * jax.nn.gelu(gate), g_ref[...]).astype(BF16)


def _conv_tile(x, prev, w_ref, b_ref, k_taps):
    rows = lax.broadcasted_iota(jnp.int32, (SUBLANES, x.shape[1]), 0)
    out = b_ref[...]
    top = b_ref[...]
    for k in range(k_taps):
        s = k_taps - 1 - k
        wk = w_ref[k:k + 1, :]
        if s == 0:
            r, t8 = x, x[0:SUBLANES]
        else:
            r = pltpu.roll(x, s, 0)
            t8 = jnp.where(rows < s, pltpu.roll(prev, s, 0), r[0:SUBLANES])
        out = out + r * wk
        top = top + t8 * wk
    return jnp.concatenate([top, out[SUBLANES:]], axis=0)


def _rglru_prompt_kernel(x_ref, gate_ref, cw_ref, cb_ref, wa_ref, ba_ref, wi_ref, bi_ref, lam_ref,
                         g_ref, out_ref, hlast_ref, xprev_ref, hc_ref, a_s, b_s, h_s, *, k_taps):
    c = pl.program_id(1)

    @pl.when(c == 0)
    def _():
        xprev_ref[...] = jnp.zeros_like(xprev_ref)
        hc_ref[...] = jnp.zeros_like(hc_ref)

    x = x_ref[...]
    t_len, width = x.shape
    xc = _conv_tile(x, xprev_ref[...], cw_ref, cb_ref, k_taps)
    xprev_ref[...] = x[t_len - SUBLANES:t_len]
    a, b = _rglru_coeffs(xc, wa_ref, ba_ref, wi_ref, bi_ref, lam_ref)
    pos = lax.broadcasted_iota(jnp.int32, (t_len, 1), 0) % SUBLANES
    for s in (1, 2, 4):
        a_sh = pltpu.roll(a, s, 0)
        b_sh = pltpu.roll(b, s, 0)
        m = pos >= s
        b = jnp.where(m, a * b_sh + b, b)
        a = jnp.where(m, a * a_sh, a)
    a_s[...] = a
    b_s[...] = b
    h = hc_ref[...]
    for grp in range(t_len // SUBLANES):
        lo, hi = grp * SUBLANES, (grp + 1) * SUBLANES
        hl = jnp.broadcast_to(h[SUBLANES - 1:SUBLANES, :], (SUBLANES, width))
        h = a_s[lo:hi, :] * hl + b_s[lo:hi, :]
        h_s[lo:hi, :] = h
    hc_ref[...] = h
    hlast_ref[...] = h
    out_ref[...] = _rg_out(h_s[...], gate_ref[...], g_ref)


def _rglru_prompt(proj, nb, seq, d_rnn, cw, cb, wa, ba, wi, bi, lam, g, *, tt):
    nc = seq // tt
    heads, blk = wa.shape[0], wa.shape[1]
    vec = pl.BlockSpec((1, d_rnn), lambda b, c: (0, 0))
    gate_w = pl.BlockSpec((heads, blk, blk), lambda b, c: (0, 0, 0))
    return pl.pallas_call(
        functools.partial(_rglru_prompt_kernel, k_taps=cw.shape[0]),
        grid=(nb, nc),
        in_specs=[pl.BlockSpec((tt, d_rnn), lambda b, c: (b * nc + c, 0)),
                  pl.BlockSpec((tt, d_rnn), lambda b, c: (b * nc + c, 1)),
                  pl.BlockSpec((cw.shape[0], d_rnn), lambda b, c: (0, 0)),
                  vec, gate_w, vec, gate_w, vec, vec, vec],
        out_specs=[pl.BlockSpec((tt, d_rnn), lambda b, c: (b * nc + c, 0)),
                   pl.BlockSpec((None, SUBLANES, d_rnn), lambda b, c: (b, 0, 0))],
        out_shape=[jax.ShapeDtypeStruct((nb * seq, d_rnn), BF16),
                   jax.ShapeDtypeStruct((nb, SUBLANES, d_rnn), F32)],
        scratch_shapes=[pltpu.VMEM((SUBLANES, d_rnn), F32), pltpu.VMEM((SUBLANES, d_rnn), F32),
                        pltpu.VMEM((tt, d_rnn), F32), pltpu.VMEM((tt, d_rnn), F32),
                        pltpu.VMEM((tt, d_rnn), F32)],
        compiler_params=_params(2),
        name="rglru_prompt",
    )(proj, proj, cw, cb, wa, ba, wi, bi, lam, g)


def _conv_slabs(st_ref, x_ref, w_ref, b_ref):
    k_taps = w_ref.shape[0]
    ext = [st_ref[s] for s in range(k_taps - 1)] + [x_ref[t] for t in range(x_ref.shape[0])]
    convs = []
    for t in range(x_ref.shape[0]):
        acc = b_ref[...]
        for k in range(k_taps):
            acc = acc + ext[t + k] * w_ref[k:k + 1, :]
        convs.append(acc)
    return jnp.concatenate(convs, axis=0)


def _rglru_sample_kernel(x_ref, gate_ref, st_ref, h0_ref, cw_ref, cb_ref, wa_ref, ba_ref, wi_ref,
                         bi_ref, lam_ref, g_ref, out_ref, hlast_ref, h_s):
    nt, bb, width = x_ref.shape
    xc = _conv_slabs(st_ref, x_ref, cw_ref, cb_ref)
    a, b = _rglru_coeffs(xc, wa_ref, ba_ref, wi_ref, bi_ref, lam_ref)
    h = h0_ref[...]
    for t in range(nt):
        h = a[t * bb:(t + 1) * bb] * h + b[t * bb:(t + 1) * bb]
        h_s[t * bb:(t + 1) * bb, :] = h
    hlast_ref[...] = h
    gate = gate_ref[...].reshape(nt * bb, width)
    out_ref[...] = _rg_out(h_s[...], gate, g_ref).reshape(nt, bb, width)


def _rglru_sample(proj3, st3, h0, d_rnn, cw, cb, wa, ba, wi, bi, lam, g, *, bb):
    nt, nb, _ = proj3.shape
    heads, blk = wa.shape[0], wa.shape[1]
    k_taps = cw.shape[0]
    vec = pl.BlockSpec((1, d_rnn), lambda i: (0, 0))
    gate_w = pl.BlockSpec((heads, blk, blk), lambda i: (0, 0, 0))
    return pl.pallas_call(
        _rglru_sample_kernel,
        grid=(nb // bb,),
        in_specs=[pl.BlockSpec((nt, bb, d_rnn), lambda i: (0, i, 0)),
                  pl.BlockSpec((nt, bb, d_rnn), lambda i: (0, i, 1)),
                  pl.BlockSpec((k_taps - 1, bb, d_rnn), lambda i: (0, i, 0)),
                  pl.BlockSpec((bb, d_rnn), lambda i: (i, 0)),
                  pl.BlockSpec((k_taps, d_rnn), lambda i: (0, 0)),
                  vec, gate_w, vec, gate_w, vec, vec, vec],
        out_specs=[pl.BlockSpec((nt, bb, d_rnn), lambda i: (0, i, 0)),
                   pl.BlockSpec((bb, d_rnn), lambda i: (i, 0))],
        out_shape=[jax.ShapeDtypeStruct((nt, nb, d_rnn), BF16),
                   jax.ShapeDtypeStruct((nb, d_rnn), F32)],
        scratch_shapes=[pltpu.VMEM((nt * bb, d_rnn), F32)],
        compiler_params=_params(1),
        name="rglru_sample",
    )(proj3, proj3, st3, h0, cw, cb, wa, ba, wi, bi, lam, g)


def _silu(x):
    return x * jax.nn.sigmoid(x)


def _group_norm_out(y, z, gn_ref, groups):
    u = y * _silu(z)
    gw = u.shape[1] // groups
    outs = []
    for g in range(groups):
        ug = u[:, g * gw:(g + 1) * gw]
        ms = jnp.mean(ug * ug, axis=-1, keepdims=True)
        outs.append((ug * lax.rsqrt(ms + EPS)) * gn_ref[:, g * gw:(g + 1) * gw])
    return jnp.concatenate(outs, axis=1).astype(BF16)


def _ssd_prompt_kernel(z_ref, xbc_ref, dt_ref, cw_ref, cb_ref, dtb_ref, alog_ref, dexp_ref, gn_ref,
                       e_ref, out_ref, hfin_ref, carry_ref, s_ref, y_s,
                       *, k_taps, d_ssm, n_state, groups, headdim, nc):
    c = pl.program_id(1)

    @pl.when(c == 0)
    def _():
        carry_ref[...] = jnp.zeros_like(carry_ref)
        s_ref[...] = jnp.zeros_like(s_ref)

    xbc = xbc_ref[...]
    q = xbc.shape[0]
    xc = _silu(_conv_tile(xbc, carry_ref[...], cw_ref, cb_ref, k_taps))
    carry_ref[...] = xbc[q - SUBLANES:q]
    gn_w = groups * n_state
    xs = xc[:, :d_ssm]
    bm = xc[:, d_ssm:d_ssm + gn_w]
    cm = xc[:, d_ssm + gn_w:d_ssm + 2 * gn_w]

    dt = _softplus(dt_ref[...] + dtb_ref[...])
    dta = dt * (-jnp.exp(alog_ref[...]))
    row = lax.broadcasted_iota(jnp.int32, (q, q), 0)
    col = lax.broadcasted_iota(jnp.int32, (q, q), 1)
    causal = row >= col
    tril = jnp.where(causal, 1.0, 0.0).astype(BF16)
    cum = _dot_sel_left(tril, dta)
    cum_t = cum.T
    e = e_ref[...]
    dt_e = _dot_sel_right(dt, e)
    cum_e = _dot_sel_right(cum, e)
    ecum_e = jnp.exp(cum_e)
    xdt = xs * dt_e
    cum_last = cum_e[q - 1:q, :]
    xdd = (xdt * jnp.exp(cum_last - cum_e)).astype(BF16)
    chunk_decay = jnp.exp(cum_last)

    heads_per_group = d_ssm // headdim // groups
    pair_w = 2 * headdim
    gw = heads_per_group * headdim
    lane = lax.broadcasted_iota(jnp.int32, (q, pair_w), 1)
    for g in range(groups):
        cg = cm[:, g * n_state:(g + 1) * n_state].astype(BF16)
        bg32 = bm[:, g * n_state:(g + 1) * n_state]
        bg = bg32.astype(BF16)
        cb_g = lax.dot_general(cg, bg, (((1,), (1,)), ((), ())), preferred_element_type=F32)
        s_old = s_ref[:, g * gw:(g + 1) * gw]
        y_s[:, g * gw:(g + 1) * gw] = _dot(cg, s_old.astype(BF16)) * ecum_e[:, g * gw:(g + 1) * gw]
        st_g = _dot(bg32.T.astype(BF16), xdd[:, g * gw:(g + 1) * gw])
        s_ref[:, g * gw:(g + 1) * gw] = s_old * chunk_decay[:, g * gw:(g + 1) * gw] + st_g
        for pr in range(heads_per_group // 2):
            h0 = g * heads_per_group + 2 * pr
            ms = []
            for h in (h0, h0 + 1):
                seg = (jnp.broadcast_to(cum[:, h:h + 1], (q, q))
                       - jnp.broadcast_to(cum_t[h:h + 1, :], (q, q)))
                ms.append(cb_g * jnp.exp(jnp.where(causal, seg, -jnp.inf)))
            lo_c = h0 * headdim
            xp = xdt[:, lo_c:lo_c + pair_w]
            lhs = jnp.concatenate(ms, axis=1).astype(BF16)
            rhs = jnp.concatenate([jnp.where(lane < headdim, xp, 0.0),
                                   jnp.where(lane >= headdim, xp, 0.0)], axis=0).astype(BF16)
            y_s[:, lo_c:lo_c + pair_w] += _dot(lhs, rhs)

    y = y_s[...] + dexp_ref[...] * xs
    out_ref[...] = _group_norm_out(y, z_ref[...], gn_ref, groups)

    @pl.when(c == nc - 1)
    def _():
        for blk in range(d_ssm // LANES):
            hfin_ref[blk * LANES:(blk + 1) * LANES, :] = s_ref[:, blk * LANES:(blk + 1) * LANES].T


def _ssd_prompt(proj, dt_raw, nb, seq, d_ssm, d_xbc, n_state, headdim, cw, cb, dtb, alog, dexp, gn, e):
    q = SSD_CHUNK
    nc = seq // q
    z_blk = (proj.shape[1] - d_xbc - d_ssm) // d_ssm
    x_blk = (proj.shape[1] - d_xbc) // d_xbc
    assert z_blk * d_ssm + d_ssm + d_xbc == proj.shape[1] and x_blk * d_xbc + d_xbc == proj.shape[1]
    one = lambda w: pl.BlockSpec((1, w), lambda b, c: (0, 0))
    return pl.pallas_call(
        functools.partial(_ssd_prompt_kernel, k_taps=cw.shape[0], d_ssm=d_ssm, n_state=n_state,
                          groups=SSD_GROUPS, headdim=headdim, nc=nc),
        grid=(nb, nc),
        in_specs=[pl.BlockSpec((q, d_ssm), lambda b, c: (b * nc + c, z_blk)),
                  pl.BlockSpec((q, d_xbc), lambda b, c: (b * nc + c, x_blk)),
                  pl.BlockSpec((q, LANES), lambda b, c: (b * nc + c, 0)),
                  pl.BlockSpec((cw.shape[0], d_xbc), lambda b, c: (0, 0)),
                  one(d_xbc), one(LANES), one(LANES), one(d_ssm), one(d_ssm),
                  pl.BlockSpec((LANES, d_ssm), lambda b, c: (0, 0))],
        out_specs=[pl.BlockSpec((q, d_ssm), lambda b, c: (b * nc + c, 0)),
                   pl.BlockSpec((None, d_ssm, n_state), lambda b, c: (b, 0, 0))],
        out_shape=[jax.ShapeDtypeStruct((nb * seq, d_ssm), BF16),
                   jax.ShapeDtypeStruct((nb, d_ssm, n_state), F32)],
        scratch_shapes=[pltpu.VMEM((SUBLANES, d_xbc), F32), pltpu.VMEM((n_state, d_ssm), F32),
                        pltpu.VMEM((q, d_ssm), F32)],
        compiler_params=_params(2),
        name="ssd_prompt",
    )(proj, proj, dt_raw, cw, cb, dtb, alog, dexp, gn, e)


def _ssd_sample_pre_kernel(xbc_ref, st_ref, dt_ref, cw_ref, cb_ref, dtb_ref, alog_ref, dexp_ref,
                           e_ref, gsel_ref, ydp_ref, ecum_ref, xdd_ref, bc_ref, cd_ref,
                           *, d_ssm, n_state, groups):
    nt, bb, _ = xbc_ref.shape
    xc = _silu(_conv_slabs(st_ref, xbc_ref, cw_ref, cb_ref))
    gn_w = groups * n_state
    xs = xc[:, :d_ssm]
    bm = xc[:, d_ssm:d_ssm + gn_w]
    cm = xc[:, d_ssm + gn_w:d_ssm + 2 * gn_w]
    bc_ref[...] = xc[:, d_ssm:d_ssm + 2 * gn_w].reshape(nt, bb, 2 * gn_w)

    dt = _softplus(dt_ref[...].reshape(nt * bb, LANES) + dtb_ref[...])
    dta = dt * (-jnp.exp(alog_ref[...]))
    cums = [dta[0:bb]]
    for t in range(1, nt):
        cums.append(cums[-1] + dta[t * bb:(t + 1) * bb])
    cum = jnp.concatenate(cums, axis=0)
    cd_ref[...] = jnp.exp(cums[-1])
    e = e_ref[...]
    dt_e = _dot_sel_right(dt, e)
    cum_e = _dot_sel_right(cum, e)
    ecum_ref[...] = jnp.exp(cum_e).reshape(nt, bb, d_ssm)
    xdt = xs * dt_e
    sl = lambda v, t: v[t * bb:(t + 1) * bb]
    cum_last = sl(cum_e, nt - 1)
    gsel = gsel_ref[...]
    for t in range(nt):
        xdd_ref[t] = sl(xdt, t) * jnp.exp(cum_last - sl(cum_e, t))
        acc = dexp_ref[...] * sl(xs, t)
        for s in range(t + 1):
            cb_e = _dot_sel_right(sl(cm, t) * sl(bm, s), gsel)
            acc = acc + (cb_e * jnp.exp(sl(cum_e, t) - sl(cum_e, s))) * sl(xdt, s)
        ydp_ref[t] = acc


def _ssd_sample_pre(proj3, st3, dt3, d_ssm, d_xbc, n_state, cw, cb, dtb, alog, dexp, e, gsel, *, bb):
    nt, nb, n_main = proj3.shape
    k_taps = cw.shape[0]
    x_blk = (n_main - d_xbc) // d_xbc
    gn_w = SSD_GROUPS * n_state
    full = lambda r, w: pl.BlockSpec((r, w), lambda i: (0, 0))
    slab = lambda n, w: pl.BlockSpec((n, bb, w), lambda i: (0, i, 0))
    f32 = lambda *s: jax.ShapeDtypeStruct(s, F32)
    return pl.pallas_call(
        functools.partial(_ssd_sample_pre_kernel, d_ssm=d_ssm, n_state=n_state, groups=SSD_GROUPS),
        grid=(nb // bb,),
        in_specs=[pl.BlockSpec((nt, bb, d_xbc), lambda i: (0, i, x_blk)),
                  slab(k_taps - 1, d_xbc), slab(nt, LANES), full(k_taps, d_xbc),
                  full(1, d_xbc), full(1, LANES), full(1, LANES), full(1, d_ssm),
                  full(LANES, d_ssm), full(gn_w, d_ssm)],
        out_specs=[slab(nt, d_ssm), slab(nt, d_ssm), slab(nt, d_ssm), slab(nt, 2 * gn_w),
                   pl.BlockSpec((bb, LANES), lambda i: (i, 0))],
        out_shape=[f32(nt, nb, d_ssm), f32(nt, nb, d_ssm), f32(nt, nb, d_ssm),
                   f32(nt, nb, 2 * gn_w), f32(nb, LANES)],
        compiler_params=_params(1),
        name="ssd_sample_pre",
    )(proj3, st3, dt3, cw, cb, dtb, alog, dexp, e, gsel)


def _ssd_sample_state_kernel(s0_ref, c_ref, b_ref, xdd_ref, cd_ref, snew_ref, yoff_ref,
                             *, bt, groups, n_state, headdim):
    heads = s0_ref.shape[1]
    hpg = heads // groups
    gw = hpg * headdim

    def body(b, carry):
        cd_row = cd_ref[pl.ds(b, 1), :]
        cb = c_ref[b].astype(BF16)
        bb = b_ref[b].astype(BF16)
        xb = xdd_ref[b].astype(BF16)
        for g in range(groups):
            s0 = s0_ref[b, g * hpg:(g + 1) * hpg].reshape(gw, n_state)
            yoff = lax.dot_general(cb[:, g * n_state:(g + 1) * n_state], s0.astype(BF16),
                                   (((1,), (1,)), ((), ())), preferred_element_type=F32)
            yoff_ref[b, :, g * gw:(g + 1) * gw] = yoff
            upd = lax.dot_general(xb[:, g * gw:(g + 1) * gw], bb[:, g * n_state:(g + 1) * n_state],
                                  (((0,), (0,)), ((), ())), preferred_element_type=F32)
            dec = jnp.concatenate(
                [jnp.broadcast_to(cd_row[:, h:h + 1], (headdim, n_state))
                 for h in range(g * hpg, (g + 1) * hpg)], axis=0)
            snew_ref[b, g * hpg:(g + 1) * hpg] = (s0 * dec + upd).reshape(hpg, headdim, n_state)
        return carry

    lax.fori_loop(0, bt, body, 0)


def _ssd_sample_state(s0, c_b, b_b, xdd_b, cd, *, bt):
    nb, heads, headdim, n_state = s0.shape
    nt = c_b.shape[1]
    gn_w = c_b.shape[2]
    d_ssm = xdd_b.shape[2]
    return pl.pallas_call(
        functools.partial(_ssd_sample_state_kernel, bt=bt, groups=SSD_GROUPS, n_state=n_state,
                          headdim=headdim),
        grid=(nb // bt,),
        in_specs=[pl.BlockSpec((bt, heads, headdim, n_state), lambda i: (i, 0, 0, 0)),
                  pl.BlockSpec((bt, nt, gn_w), lambda i: (i, 0, 0)),
                  pl.BlockSpec((bt, nt, gn_w), lambda i: (i, 0, 0)),
                  pl.BlockSpec((bt, nt, d_ssm), lambda i: (i, 0, 0)),
                  pl.BlockSpec((bt, LANES), lambda i: (i, 0))],
        out_specs=[pl.BlockSpec((bt, heads, headdim, n_state), lambda i: (i, 0, 0, 0)),
                   pl.BlockSpec((bt, nt, d_ssm), lambda i: (i, 0, 0))],
        out_shape=[jax.ShapeDtypeStruct(s0.shape, F32),
                   jax.ShapeDtypeStruct((nb, nt, d_ssm), F32)],
        compiler_params=_params(1),
        name="ssd_sample_state",
    )(s0, c_b, b_b, xdd_b, cd)


def _ssd_sample_post_kernel(ydp_ref, yoff_ref, ecum_ref, z_ref, gn_ref, out_ref, *, groups):
    nt, bb, d_ssm = ydp_ref.shape
    y = (ydp_ref[...] + yoff_ref[...] * ecum_ref[...]).reshape(nt * bb, d_ssm)
    z = z_ref[...].reshape(nt * bb, d_ssm)
    out_ref[...] = _group_norm_out(y, z, gn_ref, groups).reshape(nt, bb, d_ssm)


def _ssd_sample_post(ydp, yoff, ecum, proj3, gn, d_ssm, d_xbc, *, bb):
    nt, nb, n_main = proj3.shape
    z_blk = (n_main - d_xbc - d_ssm) // d_ssm
    slab = pl.BlockSpec((nt, bb, d_ssm), lambda i: (0, i, 0))
    return pl.pallas_call(
        functools.partial(_ssd_sample_post_kernel, groups=SSD_GROUPS),
        grid=(nb // bb,),
        in_specs=[slab, slab, slab, pl.BlockSpec((nt, bb, d_ssm), lambda i: (0, i, z_blk)),
                  pl.BlockSpec((1, d_ssm), lambda i: (0, 0))],
        out_specs=slab,
        out_shape=jax.ShapeDtypeStruct((nt, nb, d_ssm), BF16),
        compiler_params=_params(1),
        name="ssd_sample_post",
    )(ydp, yoff, ecum, proj3, gn)


def _pad_lanes(v):
    return jnp.pad(v.astype(F32), (0, LANES - v.shape[0])).reshape(1, LANES)


def kernel(x_prompt, x_sample, state_rglru_h, state_rglru_conv, state_ssd_h, state_ssd_conv, state_ffn_conv, g_mix, w_in, rg_conv_w, rg_conv_b, rg_gate_a_w, rg_gate_a_b, rg_gate_i_w, rg_gate_i_b, rg_lambda, g_rg_out, ssd_conv_w, ssd_conv_b, ssd_dt_bias, ssd_A_log, ssd_D, g_ssd_norm, w_out, g_ffn, w_ffn_up, ffn_conv_w, ffn_conv_b, w_ffn_down, g_final):
    depth = g_mix.shape[0]
    pb, seq, d_model = x_prompt.shape
    sb, dec_seq, _ = x_sample.shape
    d_rnn = rg_lambda.shape[1]
    d_ssm = g_ssd_norm.shape[1]
    d_xbc = ssd_conv_w.shape[2]
    d_ff = ffn_conv_w.shape[2]
    heads = ssd_A_log.shape[1]
    headdim = d_ssm // heads
    n_state = state_ssd_h.shape[-1]
    n_main = 2 * d_rnn + d_ssm + d_xbc
    k_rg = rg_conv_w.shape[1]
    k_ssd = ssd_conv_w.shape[1]
    k_ffn = ffn_conv_w.shape[1]
    gn_w = SSD_GROUPS * n_state
    assert heads <= LANES and n_state == LANES and k_ffn == 3 and dec_seq >= max(k_rg, k_ssd) - 1

    hid = lax.broadcasted_iota(jnp.int32, (LANES, d_ssm), 0)
    cid = lax.broadcasted_iota(jnp.int32, (LANES, d_ssm), 1)
    e_sel = (hid == cid // headdim).astype(BF16)
    gid = lax.broadcasted_iota(jnp.int32, (gn_w, d_ssm), 0) // n_state
    cgid = lax.broadcasted_iota(jnp.int32, (gn_w, d_ssm), 1) // (d_ssm // SSD_GROUPS)
    g_sel = (gid == cgid).astype(BF16)

    yp = x_prompt.reshape(pb * seq, d_model)
    ys = jnp.transpose(x_sample, (1, 0, 2)).reshape(dec_seq * sb, d_model)
    p_new = [[], [], [], [], []]
    s_new = [[], [], [], [], []]
    row = lambda v: v.reshape(1, -1)
    swap = lambda a3: jnp.transpose(a3, (1, 0, 2))
    bb = 32

    for l in range(depth):
        w_in_t = _col_tiles(w_in[l][:, :n_main], TN_IN)
        wdt_b = jnp.pad(w_in[l][:, n_main:], ((0, 0), (0, LANES - heads))).astype(BF16)
        w_out_t = _col_tiles(w_out[l], TN_OUT)
        w_up_t = _col_tiles(w_ffn_up[l], TN_UP)
        w_down_t = _col_tiles(w_ffn_down[l], TN_DOWN)
        wa_b = rg_gate_a_w[l].astype(BF16)
        wi_b = rg_gate_i_w[l].astype(BF16)
        rg_args = (rg_conv_w[l], row(rg_conv_b[l]), wa_b, row(rg_gate_a_b[l]), wi_b,
                   row(rg_gate_i_b[l]), row(rg_lambda[l]), row(g_rg_out[l]))
        dtb = _pad_lanes(ssd_dt_bias[l])
        alog = _pad_lanes(ssd_A_log[l])
        dexp = jnp.repeat(ssd_D[l].astype(F32), headdim).reshape(1, d_ssm)
        gn = row(g_ssd_norm[l])

        proj_p, dt_p = _in_proj(yp, row(g_mix[l]), w_in_t, wdt_b, tm=TM)
        rg_p, hl_p = _rglru_prompt(proj_p, pb, seq, d_rnn, *rg_args, tt=256)
        ssd_p, hfin_p = _ssd_prompt(proj_p, dt_p, pb, seq, d_ssm, d_xbc, n_state, headdim,
                                    ssd_conv_w[l], row(ssd_conv_b[l]), dtb, alog, dexp, gn, e_sel)
        x1_p, h2_p = _out_proj(rg_p, ssd_p, w_out_t, yp, row(g_ffn[l]), tm=TM)
        act_p, gst_p = _ffn_up_prompt(h2_p, w_up_t, ffn_conv_w[l], row(ffn_conv_b[l]), seq, tm=TM_UP)
        proj_p3 = proj_p.reshape(pb, seq, n_main)
        p_new[0].append(hl_p[:, SUBLANES - 1, :])
        p_new[1].append(proj_p3[:, seq - (k_rg - 1):, :d_rnn])
        p_new[2].append(hfin_p.reshape(pb, heads, headdim, n_state))
        p_new[3].append(proj_p3[:, seq - (k_ssd - 1):, n_main - d_xbc:])
        tiles = seq // TM_UP
        p_new[4].append(gst_p[tiles - 1::tiles, SUBLANES - (k_ffn - 1):, :])

        ms = ys.shape[0]
        proj_s, dt_s = _in_proj(ys, row(g_mix[l]), w_in_t, wdt_b, tm=ms)
        proj_s3 = proj_s.reshape(dec_seq, sb, n_main)
        rg_s, hl_s = _rglru_sample(proj_s3, swap(state_rglru_conv[l]), state_rglru_h[l], d_rnn,
                                   *rg_args, bb=bb)
        ydp, ecum, xdd, bc, cd = _ssd_sample_pre(
            proj_s3, swap(state_ssd_conv[l]), dt_s.reshape(dec_seq, sb, LANES), d_ssm, d_xbc, n_state,
            ssd_conv_w[l], row(ssd_conv_b[l]), dtb, alog, dexp, e_sel, g_sel, bb=bb)
        bc_b = swap(bc)
        snew, yoff_b = _ssd_sample_state(state_ssd_h[l], bc_b[:, :, gn_w:], bc_b[:, :, :gn_w],
                                         swap(xdd), cd, bt=8)
        ssd_s = _ssd_sample_post(ydp, swap(yoff_b), ecum, proj_s3, gn, d_ssm, d_xbc, bb=bb)
        x1_s, h2_s = _out_proj(rg_s.reshape(ms, d_rnn), ssd_s.reshape(ms, d_ssm), w_out_t, ys,
                               row(g_ffn[l]), tm=ms)
        act_s, gst_s = _ffn_up_sample(h2_s, w_up_t, ffn_conv_w[l], row(ffn_conv_b[l]),
                                      swap(state_ffn_conv[l]).reshape(-1, d_ff), sb)
        s_new[0].append(hl_s)
        s_new[1].append(swap(proj_s3[dec_seq - (k_rg - 1):, :, :d_rnn]))
        s_new[2].append(snew)
        s_new[3].append(swap(proj_s3[dec_seq - (k_ssd - 1):, :, n_main - d_xbc:]))
        s_new[4].append(swap(gst_s.reshape(k_ffn - 1, sb, d_ff)))

        last = l == depth - 1
        yp = _ffn_down(act_p, w_down_t, x1_p, row(g_final), last, tm=TM)
        ys = _ffn_down(act_s, w_down_t, x1_s, row(g_final), last, tm=ms)

    y_prompt = yp.reshape(pb, seq, d_model)
    y_sample = jnp.transpose(ys.reshape(dec_seq, sb, d_model), (1, 0, 2))
    return (y_prompt, y_sample,
            jnp.stack(p_new[0]), jnp.stack(p_new[1]), jnp.stack(p_new[2]), jnp.stack(p_new[3]),
            jnp.stack(p_new[4]),
            jnp.stack(s_new[0]), jnp.stack(s_new[1]), jnp.stack(s_new[2]), jnp.stack(s_new[3]),
            jnp.stack(s_new[4]))
```

```python
import functools

import jax
import jax.numpy as jnp
from jax import lax
from jax.experimental import pallas as pl
from jax.experimental.pallas import tpu as pltpu

F32 = jnp.float32
BF16 = jnp.bfloat16

EPS = 1e-6
RG_C = 8.0
SSD_CHUNK = 128
SSD_GROUPS = 4
LANES = 128
SUBLANES = 8
VMEM_LIMIT = 56 * 2**20

TM = 512
TM_UP = 1024
TT_RG = 256
TN_IN = 1024
TN_OUT = 1024
TN_UP = 512
TN_DOWN = 256


def _params(n_axes):
    return pltpu.CompilerParams(dimension_semantics=("arbitrary",) * n_axes,
                                vmem_limit_bytes=VMEM_LIMIT)


def _dot(a, b):
    return jnp.dot(a, b, preferred_element_type=F32)


def _rms_rows(x, g):
    ms = jnp.mean(x * x, axis=-1, keepdims=True)
    return (x * lax.rsqrt(ms + EPS)) * g


def _split3(x):
    hi = x.astype(BF16)
    r = x - hi.astype(F32)
    mid = r.astype(BF16)
    lo = (r - mid.astype(F32)).astype(BF16)
    return hi, mid, lo


def _dot_sel_right(x, sel):
    hi, mid, lo = _split3(x)
    return (_dot(hi, sel) + _dot(mid, sel)) + _dot(lo, sel)


def _dot_sel_left(sel, x):
    hi, mid, lo = _split3(x)
    return (_dot(sel, hi) + _dot(sel, mid)) + _dot(sel, lo)


def _inproj_kernel(x_ref, g_ref, w_ref, wdt_ref, o_ref, dt_ref, hn_ref):
    @pl.when(pl.program_id(1) == 0)
    def _():
        hn = _rms_rows(x_ref[...], g_ref[...]).astype(BF16)
        hn_ref[...] = hn
        dt_ref[...] = _dot(hn, wdt_ref[...])

    o_ref[...] = _dot(hn_ref[...], w_ref[...])


def _cast_col_tiles(src_ref, dst_ref):
    n_tiles, _, tn = dst_ref.shape
    for t in range(n_tiles):
        dst_ref[t] = src_ref[:, t * tn:(t + 1) * tn].astype(BF16)


def _in_proj(x2d, g, w_b, wdt_b, n_main, *, tm, tn):
    m, d = x2d.shape
    return pl.pallas_call(
        _inproj_kernel,
        grid=(m // tm, n_main // tn),
        in_specs=[pl.BlockSpec((tm, d), lambda i, j: (i, 0)),
                  pl.BlockSpec((1, d), lambda i, j: (0, 0)),
                  pl.BlockSpec((d, tn), lambda i, j: (0, j)),
                  pl.BlockSpec((d, LANES), lambda i, j: (0, 0))],
        out_specs=[pl.BlockSpec((tm, tn), lambda i, j: (i, j)),
                   pl.BlockSpec((tm, LANES), lambda i, j: (i, 0))],
        out_shape=[jax.ShapeDtypeStruct((m, n_main), F32),
                   jax.ShapeDtypeStruct((m, LANES), F32)],
        scratch_shapes=[pltpu.VMEM((tm, d), BF16)],
        compiler_params=_params(2),
        name="in_proj",
    )(x2d, g, w_b, wdt_b)


def _outproj_kernel(rg_ref, ssd_ref, wt_ref, wb_ref, x_ref, g_ref, x1_ref, h2_ref, row_ref, *, nj, tn):
    j = pl.program_id(1)
    x1 = x_ref[...] + (_dot(rg_ref[...], wt_ref[...]) + _dot(ssd_ref[...], wb_ref[...]))
    x1_ref[...] = x1
    row_ref[j] = x1

    @pl.when(j == nj - 1)
    def _():
        ss = jnp.sum(row_ref[0] * row_ref[0], axis=-1, keepdims=True)
        for jj in range(1, nj):
            ss = ss + jnp.sum(row_ref[jj] * row_ref[jj], axis=-1, keepdims=True)
        inv = lax.rsqrt(ss / (nj * tn) + EPS)
        for jj in range(nj):
            h2_ref[:, jj * tn:(jj + 1) * tn] = (
                (row_ref[jj] * inv) * g_ref[:, jj * tn:(jj + 1) * tn]).astype(BF16)


def _out_proj(rg, ssd, w_b, x2d, g, *, tm, tn):
    m, d = x2d.shape
    k_half = rg.shape[1]
    nj = d // tn
    return pl.pallas_call(
        functools.partial(_outproj_kernel, nj=nj, tn=tn),
        grid=(m // tm, nj),
        in_specs=[pl.BlockSpec((tm, k_half), lambda i, j: (i, 0)),
                  pl.BlockSpec((tm, k_half), lambda i, j: (i, 0)),
                  pl.BlockSpec((k_half, tn), lambda i, j: (0, j)),
                  pl.BlockSpec((k_half, tn), lambda i, j: (1, j)),
                  pl.BlockSpec((tm, tn), lambda i, j: (i, j)),
                  pl.BlockSpec((1, d), lambda i, j: (0, 0))],
        out_specs=[pl.BlockSpec((tm, tn), lambda i, j: (i, j)),
                   pl.BlockSpec((tm, d), lambda i, j: (i, 0))],
        out_shape=[jax.ShapeDtypeStruct((m, d), F32),
                   jax.ShapeDtypeStruct((m, d), BF16)],
        scratch_shapes=[pltpu.VMEM((nj, tm, tn), F32)],
        compiler_params=_params(2),
        name="out_proj",
    )(rg, ssd, w_b, w_b, x2d, g)


def _glu(gate_c, val):
    return (jax.nn.gelu(gate_c) * val).astype(BF16)


def _dot_split_k(h_ref, wa_ref, wb_ref):
    ka = wa_ref.shape[0]
    return _dot(h_ref[:, :ka], wa_ref[...]) + _dot(h_ref[:, ka:], wb_ref[...])


def _ffn_up_prompt_kernel(h2_ref, wga_ref, wgb_ref, wva_ref, wvb_ref, cw_ref, cb_ref, wsrc_ref,
                          act_ref, gst_ref, wdst_ref, carry_ref, *, tiles_per_seq):
    i = pl.program_id(0)
    j = pl.program_id(1)
    _cast_col_tiles(wsrc_ref, wdst_ref)
    gate = _dot_split_k(h2_ref, wga_ref, wgb_ref)
    val = _dot_split_k(h2_ref, wva_ref, wvb_ref)
    tm, tn = gate.shape
    w0, w1, w2 = cw_ref[0:1, :], cw_ref[1:2, :], cw_ref[2:3, :]
    cb = cb_ref[...]
    g1 = pltpu.roll(gate, 1, 0)
    g2 = pltpu.roll(gate, 2, 0)
    act_ref[...] = _glu(((cb + g2 * w0) + g1 * w1) + gate * w2, val)
    @pl.when((i % tiles_per_seq) == 0)
    def _():
        carry_ref[j] = jnp.zeros((SUBLANES, tn), F32)

    prev = carry_ref[j]
    rows = lax.broadcasted_iota(jnp.int32, (SUBLANES, tn), 0)
    t1 = jnp.where(rows < 1, pltpu.roll(prev, 1, 0), g1[0:SUBLANES])
    t2 = jnp.where(rows < 2, pltpu.roll(prev, 2, 0), g2[0:SUBLANES])
    act_ref[0:SUBLANES, :] = _glu(((cb + t2 * w0) + t1 * w1) + gate[0:SUBLANES] * w2,
                                  val[0:SUBLANES])
    last = gate[tm - SUBLANES:tm]
    carry_ref[j] = last
    gst_ref[...] = last


def _ffn_up_prompt(h2, w_ta, w_tb, cw, cb, seq, w_next, tn_next, *, tm):
    m, d = h2.shape
    nj, ka, tn = w_ta.shape[0] // 2, w_ta.shape[1], w_ta.shape[2]
    kb = w_tb.shape[1]
    assert ka + kb == d and ka % LANES == 0
    d_ff = nj * tn
    steps = (m // tm) * nj
    k_next, n_next = w_next.shape
    rb = k_next // steps
    assert rb * steps == k_next and rb % 16 == 0 and n_next % tn_next == 0
    return pl.pallas_call(
        functools.partial(_ffn_up_prompt_kernel, tiles_per_seq=seq // tm),
        grid=(m // tm, nj),
        in_specs=[pl.BlockSpec((tm, d), lambda i, j: (i, 0)),
                  pl.BlockSpec((None, ka, tn), lambda i, j: (j, 0, 0)),
                  pl.BlockSpec((None, kb, tn), lambda i, j: (j, 0, 0)),
                  pl.BlockSpec((None, ka, tn), lambda i, j: (j + nj, 0, 0)),
                  pl.BlockSpec((None, kb, tn), lambda i, j: (j + nj, 0, 0)),
                  pl.BlockSpec((3, tn), lambda i, j: (0, j)),
                  pl.BlockSpec((1, tn), lambda i, j: (0, j)),
                  pl.BlockSpec((rb, n_next), lambda i, j: (i * nj + j, 0))],
        out_specs=[pl.BlockSpec((tm, tn), lambda i, j: (i, j)),
                   pl.BlockSpec((None, SUBLANES, tn), lambda i, j: (i, 0, j)),
                   pl.BlockSpec((n_next // tn_next, rb, tn_next), lambda i, j: (0, i * nj + j, 0))],
        out_shape=[jax.ShapeDtypeStruct((m, d_ff), BF16),
                   jax.ShapeDtypeStruct((m // tm, SUBLANES, d_ff), F32),
                   jax.ShapeDtypeStruct((n_next // tn_next, k_next, tn_next), BF16)],
        scratch_shapes=[pltpu.VMEM((nj, SUBLANES, tn), F32)],
        compiler_params=_params(2),
        name="ffn_up_prompt",
    )(h2, w_ta, w_tb, w_ta, w_tb, cw, cb, w_next)


def _ffn_up_sample_kernel(h2_ref, wga_ref, wgb_ref, wva_ref, wvb_ref, cw_ref, cb_ref, st_ref,
                          act_ref, nst_ref, *, nb, nt):
    gate = _dot_split_k(h2_ref, wga_ref, wgb_ref)
    val = _dot_split_k(h2_ref, wva_ref, wvb_ref)
    w0, w1, w2 = cw_ref[0:1, :], cw_ref[1:2, :], cw_ref[2:3, :]
    cb = cb_ref[...]
    ext = [st_ref[0:nb, :], st_ref[nb:2 * nb, :]] + [gate[t * nb:(t + 1) * nb] for t in range(nt)]
    for t in range(nt):
        gc = ((cb + ext[t] * w0) + ext[t + 1] * w1) + ext[t + 2] * w2
        act_ref[t * nb:(t + 1) * nb, :] = _glu(gc, val[t * nb:(t + 1) * nb])
    nst_ref[0:nb, :] = ext[nt]
    nst_ref[nb:2 * nb, :] = ext[nt + 1]


def _ffn_up_sample(h2, w_ta, w_tb, cw, cb, st, nb):
    m, d = h2.shape
    nj, ka, tn = w_ta.shape[0] // 2, w_ta.shape[1], w_ta.shape[2]
    kb = w_tb.shape[1]
    d_ff = nj * tn
    return pl.pallas_call(
        functools.partial(_ffn_up_sample_kernel, nb=nb, nt=m // nb),
        grid=(nj,),
        in_specs=[pl.BlockSpec((m, d), lambda j: (0, 0)),
                  pl.BlockSpec((None, ka, tn), lambda j: (j, 0, 0)),
                  pl.BlockSpec((None, kb, tn), lambda j: (j, 0, 0)),
                  pl.BlockSpec((None, ka, tn), lambda j: (j + nj, 0, 0)),
                  pl.BlockSpec((None, kb, tn), lambda j: (j + nj, 0, 0)),
                  pl.BlockSpec((3, tn), lambda j: (0, j)),
                  pl.BlockSpec((1, tn), lambda j: (0, j)),
                  pl.BlockSpec((2 * nb, tn), lambda j: (0, j))],
        out_specs=[pl.BlockSpec((m, tn), lambda j: (0, j)),
                   pl.BlockSpec((2 * nb, tn), lambda j: (0, j))],
        out_shape=[jax.ShapeDtypeStruct((m, d_ff), BF16),
                   jax.ShapeDtypeStruct((2 * nb, d_ff), F32)],
        compiler_params=_params(1),
        name="ffn_up_sample",
    )(h2, w_ta, w_tb, w_ta, w_tb, cw, cb, st)


def _ffn_down_kernel(act_ref, w_ref, x1_ref, g_ref, y_hbm, rows_ref, sem, *, ni, nj, tm, tn, final_norm):
    i = pl.program_id(0)
    j = pl.program_id(1)

    def rows_out(tile):
        return pltpu.make_async_copy(rows_ref, y_hbm.at[pl.ds(tile * tm, tm)], sem)

    res = x1_ref[...] + _dot(act_ref[...], w_ref[...])

    @pl.when((j == 0) & (i > 0))
    def _():
        rows_out(i - 1).wait()

    rows_ref[:, pl.ds(pl.multiple_of(j * tn, tn), tn)] = res

    @pl.when(j == nj - 1)
    def _():
        if final_norm:
            rows_ref[...] = _rms_rows(rows_ref[...], g_ref[...])
        rows_out(i).start()

        @pl.when(i == ni - 1)
        def _():
            rows_out(i).wait()


def _ffn_down(act, w_t, x1, g, final_norm, *, tm):
    m, d = x1.shape
    nj, d_ff, tn = w_t.shape
    ni = m // tm
    return pl.pallas_call(
        functools.partial(_ffn_down_kernel, ni=ni, nj=nj, tm=tm, tn=tn, final_norm=final_norm),
        grid=(ni, nj),
        in_specs=[pl.BlockSpec((tm, d_ff), lambda i, j: (i, 0)),
                  pl.BlockSpec((None, d_ff, tn), lambda i, j: (j, 0, 0)),
                  pl.BlockSpec((tm, tn), lambda i, j: (i, j)),
                  pl.BlockSpec((1, d), lambda i, j: (0, 0))],
        out_specs=pl.BlockSpec(memory_space=pl.ANY),
        out_shape=jax.ShapeDtypeStruct((m, d), F32),
        scratch_shapes=[pltpu.VMEM((tm, d), F32), pltpu.SemaphoreType.DMA(())],
        compiler_params=_params(2),
        name="ffn_down",
    )(act, w_t, x1, g)


def _softplus(x):
    return jnp.maximum(x, 0.0) + jnp.log1p(jnp.exp(-jnp.abs(x)))


def _rglru_coeffs(xc, wa_ref, ba_ref, wi_ref, bi_ref, lam_ref):
    xb = xc.astype(BF16)
    heads, blk = wa_ref.shape[0], wa_ref.shape[1]
    ga = jnp.concatenate([_dot(xb[:, h * blk:(h + 1) * blk], wa_ref[h]) for h in range(heads)], axis=1)
    gi = jnp.concatenate([_dot(xb[:, h * blk:(h + 1) * blk], wi_ref[h]) for h in range(heads)], axis=1)
    gate_r = jax.nn.sigmoid(ga + ba_ref[...])
    gate_i = jax.nn.sigmoid(gi + bi_ref[...])
    log_a = (-RG_C * gate_r) * _softplus(-lam_ref[...])
    a = jnp.exp(log_a)
    bx = jnp.sqrt(-jnp.tanh(log_a) * (a * a + 1.0)) * (gate_i * xc)
    return a, bx


def _rg_out(h, gate, g_ref):
    return _rms_rows(h * jax.nn.gelu(gate), g_ref[...]).astype(BF16)


def _conv_tile(x, prev, w_ref, b_ref, k_taps):
    rows = lax.broadcasted_iota(jnp.int32, (SUBLANES, x.shape[1]), 0)
    out = b_ref[...]
    top = b_ref[...]
    for k in range(k_taps):
        s = k_taps - 1 - k
        wk = w_ref[k:k + 1, :]
        if s == 0:
            r, t8 = x, x[0:SUBLANES]
        else:
            r = pltpu.roll(x, s, 0)
            t8 = jnp.where(rows < s, pltpu.roll(prev, s, 0), r[0:SUBLANES])
        out = out + r * wk
        top = top + t8 * wk
    return jnp.concatenate([top, out[SUBLANES:]], axis=0)


def _rglru_prompt_kernel(x_ref, gate_ref, cw_ref, cb_ref, wa_ref, ba_ref, wi_ref, bi_ref, lam_ref,
                         g_ref, wsrc_ref, out_ref, hlast_ref, wdst_ref, xprev_ref, hc_ref, a_s, b_s, h_s,
                         *, k_taps):
    c = pl.program_id(1)
    _cast_col_tiles(wsrc_ref, wdst_ref)

    @pl.when(c == 0)
    def _():
        xprev_ref[...] = jnp.zeros_like(xprev_ref)
        hc_ref[...] = jnp.zeros_like(hc_ref)

    x = x_ref[...]
    t_len, width = x.shape
    xc = _conv_tile(x, xprev_ref[...], cw_ref, cb_ref, k_taps)
    xprev_ref[...] = x[t_len - SUBLANES:t_len]
    a, b = _rglru_coeffs(xc, wa_ref, ba_ref, wi_ref, bi_ref, lam_ref)
    pos = lax.broadcasted_iota(jnp.int32, (t_len, 1), 0) % SUBLANES
    for s in (1, 2, 4):
        a_sh = pltpu.roll(a, s, 0)
        b_sh = pltpu.roll(b, s, 0)
        m = pos >= s
        b = jnp.where(m, a * b_sh + b, b)
        a = jnp.where(m, a * a_sh, a)
    a_s[...] = a
    b_s[...] = b
    h = hc_ref[...]
    for grp in range(t_len // SUBLANES):
        lo, hi = grp * SUBLANES, (grp + 1) * SUBLANES
        hl = jnp.broadcast_to(h[SUBLANES - 1:SUBLANES, :], (SUBLANES, width))
        h = a_s[lo:hi, :] * hl + b_s[lo:hi, :]
        h_s[lo:hi, :] = h
    hc_ref[...] = h
    hlast_ref[...] = h
    out_ref[...] = _rg_out(h_s[...], gate_ref[...], g_ref)


def _rglru_prompt(proj, nb, seq, d_rnn, cw, cb, wa, ba, wi, bi, lam, g, w_next, tn_next, rb, *, tt):
    nc = seq // tt
    heads, blk = wa.shape[0], wa.shape[1]
    k_next, n_next = w_next.shape
    vec = pl.BlockSpec((1, d_rnn), lambda b, c: (0, 0))
    gate_w = pl.BlockSpec((heads, blk, blk), lambda b, c: (0, 0, 0))
    return pl.pallas_call(
        functools.partial(_rglru_prompt_kernel, k_taps=cw.shape[0]),
        grid=(nb, nc),
        in_specs=[pl.BlockSpec((tt, d_rnn), lambda b, c: (b * nc + c, 0)),
                  pl.BlockSpec((tt, d_rnn), lambda b, c: (b * nc + c, 1)),
                  pl.BlockSpec((cw.shape[0], d_rnn), lambda b, c: (0, 0)),
                  vec, gate_w, vec, gate_w, vec, vec, vec,
                  pl.BlockSpec((rb, n_next), lambda b, c: (b * nc + c, 0))],
        out_specs=[pl.BlockSpec((tt, d_rnn), lambda b, c: (b * nc + c, 0)),
                   pl.BlockSpec((None, SUBLANES, d_rnn), lambda b, c: (b, 0, 0)),
                   pl.BlockSpec((n_next // tn_next, rb, tn_next), lambda b, c: (0, b * nc + c, 0))],
        out_shape=[jax.ShapeDtypeStruct((nb * seq, d_rnn), BF16),
                   jax.ShapeDtypeStruct((nb, SUBLANES, d_rnn), F32),
                   jax.ShapeDtypeStruct((n_next // tn_next, nb * nc * rb, tn_next), BF16)],
        scratch_shapes=[pltpu.VMEM((SUBLANES, d_rnn), F32), pltpu.VMEM((SUBLANES, d_rnn), F32),
                        pltpu.VMEM((tt, d_rnn), F32), pltpu.VMEM((tt, d_rnn), F32),
                        pltpu.VMEM((tt, d_rnn), F32)],
        compiler_params=_params(2),
        name="rglru_prompt",
    )(proj, proj, cw, cb, wa, ba, wi, bi, lam, g, w_next)


def _conv_slabs(st_ref, x_ref, w_ref, b_ref):
    k_taps = w_ref.shape[0]
    ext = [st_ref[s] for s in range(k_taps - 1)] + [x_ref[t] for t in range(x_ref.shape[0])]
    convs = []
    for t in range(x_ref.shape[0]):
        acc = b_ref[...]
        for k in range(k_taps):
            acc = acc + ext[t + k] * w_ref[k:k + 1, :]
        convs.append(acc)
    return jnp.concatenate(convs, axis=0)


def _rglru_sample_kernel(x_ref, gate_ref, st_ref, h0_ref, cw_ref, cb_ref, wa_ref, ba_ref, wi_ref,
                         bi_ref, lam_ref, g_ref, out_ref, hlast_ref, h_s):
    nt, bb, width = x_ref.shape
    xc = _conv_slabs(st_ref, x_ref, cw_ref, cb_ref)
    a, b = _rglru_coeffs(xc, wa_ref, ba_ref, wi_ref, bi_ref, lam_ref)
    h = h0_ref[...]
    for t in range(nt):
        h = a[t * bb:(t + 1) * bb] * h + b[t * bb:(t + 1) * bb]
        h_s[t * bb:(t + 1) * bb, :] = h
    hlast_ref[...] = h
    gate = gate_ref[...].reshape(nt * bb, width)
    out_ref[...] = _rg_out(h_s[...], gate, g_ref).reshape(nt, bb, width)


def _rglru_sample(proj3, st3, h0, d_rnn, cw, cb, wa, ba, wi, bi, lam, g, *, bb):
    nt, nb, _ = proj3.shape
    heads, blk = wa.shape[0], wa.shape[1]
    k_taps = cw.shape[0]
    vec = pl.BlockSpec((1, d_rnn), lambda i: (0, 0))
    gate_w = pl.BlockSpec((heads, blk, blk), lambda i: (0, 0, 0))
    return pl.pallas_call(
        _rglru_sample_kernel,
        grid=(nb // bb,),
        in_specs=[pl.BlockSpec((nt, bb, d_rnn), lambda i: (0, i, 0)),
                  pl.BlockSpec((nt, bb, d_rnn), lambda i: (0, i, 1)),
                  pl.BlockSpec((k_taps - 1, bb, d_rnn), lambda i: (0, i, 0)),
                  pl.BlockSpec((bb, d_rnn), lambda i: (i, 0)),
                  pl.BlockSpec((k_taps, d_rnn), lambda i: (0, 0)),
                  vec, gate_w, vec, gate_w, vec, vec, vec],
        out_specs=[pl.BlockSpec((nt, bb, d_rnn), lambda i: (0, i, 0)),
                   pl.BlockSpec((bb, d_rnn), lambda i: (i, 0))],
        out_shape=[jax.ShapeDtypeStruct((nt, nb, d_rnn), BF16),
                   jax.ShapeDtypeStruct((nb, d_rnn), F32)],
        scratch_shapes=[pltpu.VMEM((nt * bb, d_rnn), F32)],
        compiler_params=_params(1),
        name="rglru_sample",
    )(proj3, proj3, st3, h0, cw, cb, wa, ba, wi, bi, lam, g)


def _silu(x):
    return x * jax.nn.sigmoid(x)


def _group_norm_out(y, z, gn_ref, groups):
    u = y * _silu(z)
    gw = u.shape[1] // groups
    outs = []
    for g in range(groups):
        ug = u[:, g * gw:(g + 1) * gw]
        ms = jnp.mean(ug * ug, axis=-1, keepdims=True)
        outs.append((ug * lax.rsqrt(ms + EPS)) * gn_ref[:, g * gw:(g + 1) * gw])
    return jnp.concatenate(outs, axis=1).astype(BF16)


def _ssd_prompt_kernel(z_ref, xbc_ref, dt_ref, cw_ref, cb_ref, dtb_ref, alog_ref, dexp_ref, gn_ref,
                       e_ref, wsrc_ref, out_ref, hfin_ref, wdst_ref, carry_ref, s_ref, y_s,
                       *, k_taps, d_ssm, n_state, groups, headdim, nc):
    c = pl.program_id(1)
    _cast_col_tiles(wsrc_ref, wdst_ref)

    @pl.when(c == 0)
    def _():
        carry_ref[...] = jnp.zeros_like(carry_ref)
        s_ref[...] = jnp.zeros_like(s_ref)

    xbc = xbc_ref[...]
    q = xbc.shape[0]
    xc = _silu(_conv_tile(xbc, carry_ref[...], cw_ref, cb_ref, k_taps))
    carry_ref[...] = xbc[q - SUBLANES:q]
    gn_w = groups * n_state
    xs = xc[:, :d_ssm]
    bm = xc[:, d_ssm:d_ssm + gn_w]
    cm = xc[:, d_ssm + gn_w:d_ssm + 2 * gn_w]

    dt = _softplus(dt_ref[...] + dtb_ref[...])
    dta = dt * (-jnp.exp(alog_ref[...]))
    row = lax.broadcasted_iota(jnp.int32, (q, q), 0)
    col = lax.broadcasted_iota(jnp.int32, (q, q), 1)
    causal = row >= col
    tril = jnp.where(causal, 1.0, 0.0).astype(BF16)
    cum = _dot_sel_left(tril, dta)
    cum_t = cum.T
    e = e_ref[...]
    dt_e = _dot_sel_right(dt, e)
    cum_e = _dot_sel_right(cum, e)
    ecum_e = jnp.exp(cum_e)
    xdt = xs * dt_e
    cum_last = cum_e[q - 1:q, :]
    xdd = (xdt * jnp.exp(cum_last - cum_e)).astype(BF16)
    chunk_decay = jnp.exp(cum_last)

    heads_per_group = d_ssm // headdim // groups
    pair_w = 2 * headdim
    gw = heads_per_group * headdim
    lane = lax.broadcasted_iota(jnp.int32, (q, pair_w), 1)
    for g in range(groups):
        cg = cm[:, g * n_state:(g + 1) * n_state].astype(BF16)
        bg32 = bm[:, g * n_state:(g + 1) * n_state]
        bg = bg32.astype(BF16)
        cb_g = lax.dot_general(cg, bg, (((1,), (1,)), ((), ())), preferred_element_type=F32)
        s_old = s_ref[:, g * gw:(g + 1) * gw]
        y_s[:, g * gw:(g + 1) * gw] = _dot(cg, s_old.astype(BF16)) * ecum_e[:, g * gw:(g + 1) * gw]
        st_g = _dot(bg32.T.astype(BF16), xdd[:, g * gw:(g + 1) * gw])
        s_ref[:, g * gw:(g + 1) * gw] = s_old * chunk_decay[:, g * gw:(g + 1) * gw] + st_g
        for pr in range(heads_per_group // 2):
            h0 = g * heads_per_group + 2 * pr
            ms = []
            for h in (h0, h0 + 1):
                seg = (jnp.broadcast_to(cum[:, h:h + 1], (q, q))
                       - jnp.broadcast_to(cum_t[h:h + 1, :], (q, q)))
                ms.append(cb_g * jnp.exp(jnp.where(causal, seg, -jnp.inf)))
            lo_c = h0 * headdim
            xp = xdt[:, lo_c:lo_c + pair_w]
            lhs = jnp.concatenate(ms, axis=1).astype(BF16)
            rhs = jnp.concatenate([jnp.where(lane < headdim, xp, 0.0),
                                   jnp.where(lane >= headdim, xp, 0.0)], axis=0).astype(BF16)
            y_s[:, lo_c:lo_c + pair_w] += _dot(lhs, rhs)

    y = y_s[...] + dexp_ref[...] * xs
    out_ref[...] = _group_norm_out(y, z_ref[...], gn_ref, groups)

    @pl.when(c == nc - 1)
    def _():
        for blk in range(d_ssm // LANES):
            hfin_ref[blk * LANES:(blk + 1) * LANES, :] = s_ref[:, blk * LANES:(blk + 1) * LANES].T


def _ssd_prompt(proj, dt_raw, nb, seq, d_ssm, d_xbc, n_state, headdim, cw, cb, dtb, alog, dexp, gn, e,
                w_next, tn_next, rb, row0):
    q = SSD_CHUNK
    nc = seq // q
    z_blk = (proj.shape[1] - d_xbc - d_ssm) // d_ssm
    x_blk = (proj.shape[1] - d_xbc) // d_xbc
    assert z_blk * d_ssm + d_ssm + d_xbc == proj.shape[1] and x_blk * d_xbc + d_xbc == proj.shape[1]
    k_next, n_next = w_next.shape
    n_tiles = n_next // tn_next
    blk0 = row0 // rb
    assert blk0 * rb == row0 and row0 + nb * nc * rb == k_next
    one = lambda w: pl.BlockSpec((1, w), lambda b, c: (0, 0))
    return pl.pallas_call(
        functools.partial(_ssd_prompt_kernel, k_taps=cw.shape[0], d_ssm=d_ssm, n_state=n_state,
                          groups=SSD_GROUPS, headdim=headdim, nc=nc),
        grid=(nb, nc),
        in_specs=[pl.BlockSpec((q, d_ssm), lambda b, c: (b * nc + c, z_blk)),
                  pl.BlockSpec((q, d_xbc), lambda b, c: (b * nc + c, x_blk)),
                  pl.BlockSpec((q, LANES), lambda b, c: (b * nc + c, 0)),
                  pl.BlockSpec((cw.shape[0], d_xbc), lambda b, c: (0, 0)),
                  one(d_xbc), one(LANES), one(LANES), one(d_ssm), one(d_ssm),
                  pl.BlockSpec((LANES, d_ssm), lambda b, c: (0, 0)),
                  pl.BlockSpec((rb, n_next), lambda b, c: (blk0 + b * nc + c, 0))],
        out_specs=[pl.BlockSpec((q, d_ssm), lambda b, c: (b * nc + c, 0)),
                   pl.BlockSpec((None, d_ssm, n_state), lambda b, c: (b, 0, 0)),
                   pl.BlockSpec((n_tiles, rb, tn_next), lambda b, c: (0, b * nc + c, 0))],
        out_shape=[jax.ShapeDtypeStruct((nb * seq, d_ssm), BF16),
                   jax.ShapeDtypeStruct((nb, d_ssm, n_state), F32),
                   jax.ShapeDtypeStruct((n_tiles, k_next - row0, tn_next), BF16)],
        scratch_shapes=[pltpu.VMEM((SUBLANES, d_xbc), F32), pltpu.VMEM((n_state, d_ssm), F32),
                        pltpu.VMEM((q, d_ssm), F32)],
        compiler_params=_params(2),
        name="ssd_prompt",
    )(proj, proj, dt_raw, cw, cb, dtb, alog, dexp, gn, e, w_next)


def _ssd_sample_pre_kernel(xbc_ref, st_ref, dt_ref, cw_ref, cb_ref, dtb_ref, alog_ref, dexp_ref,
                           e_ref, gsel_ref, ydp_ref, ecum_ref, xdd_ref, bc_ref, cd_ref,
                           *, d_ssm, n_state, groups):
    nt, bb, _ = xbc_ref.shape
    xc = _silu(_conv_slabs(st_ref, xbc_ref, cw_ref, cb_ref))
    gn_w = groups * n_state
    xs = xc[:, :d_ssm]
    bm = xc[:, d_ssm:d_ssm + gn_w]
    cm = xc[:, d_ssm + gn_w:d_ssm + 2 * gn_w]
    bc_ref[...] = xc[:, d_ssm:d_ssm + 2 * gn_w].reshape(nt, bb, 2 * gn_w)

    dt = _softplus(dt_ref[...].reshape(nt * bb, LANES) + dtb_ref[...])
    dta = dt * (-jnp.exp(alog_ref[...]))
    cums = [dta[0:bb]]
    for t in range(1, nt):
        cums.append(cums[-1] + dta[t * bb:(t + 1) * bb])
    cum = jnp.concatenate(cums, axis=0)
    cd_ref[...] = jnp.exp(cums[-1])
    e = e_ref[...]
    dt_e = _dot_sel_right(dt, e)
    cum_e = _dot_sel_right(cum, e)
    ecum_ref[...] = jnp.exp(cum_e).reshape(nt, bb, d_ssm)
    xdt = xs * dt_e
    sl = lambda v, t: v[t * bb:(t + 1) * bb]
    cum_last = sl(cum_e, nt - 1)
    gsel = gsel_ref[...]
    for t in range(nt):
        xdd_ref[t] = sl(xdt, t) * jnp.exp(cum_last - sl(cum_e, t))
        acc = dexp_ref[...] * sl(xs, t)
        for s in range(t + 1):
            cb_e = _dot_sel_right(sl(cm, t) * sl(bm, s), gsel)
            acc = acc + (cb_e * jnp.exp(sl(cum_e, t) - sl(cum_e, s))) * sl(xdt, s)
        ydp_ref[t] = acc


def _ssd_sample_pre(proj3, st3, dt3, d_ssm, d_xbc, n_state, cw, cb, dtb, alog, dexp, e, gsel, *, bb):
    nt, nb, n_main = proj3.shape
    k_taps = cw.shape[0]
    x_blk = (n_main - d_xbc) // d_xbc
    gn_w = SSD_GROUPS * n_state
    full = lambda r, w: pl.BlockSpec((r, w), lambda i: (0, 0))
    slab = lambda n, w: pl.BlockSpec((n, bb, w), lambda i: (0, i, 0))
    f32 = lambda *s: jax.ShapeDtypeStruct(s, F32)
    return pl.pallas_call(
        functools.partial(_ssd_sample_pre_kernel, d_ssm=d_ssm, n_state=n_state, groups=SSD_GROUPS),
        grid=(nb // bb,),
        in_specs=[pl.BlockSpec((nt, bb, d_xbc), lambda i: (0, i, x_blk)),
                  slab(k_taps - 1, d_xbc), slab(nt, LANES), full(k_taps, d_xbc),
                  full(1, d_xbc), full(1, LANES), full(1, LANES), full(1, d_ssm),
                  full(LANES, d_ssm), full(gn_w, d_ssm)],
        out_specs=[slab(nt, d_ssm), slab(nt, d_ssm), slab(nt, d_ssm), slab(nt, 2 * gn_w),
                   pl.BlockSpec((bb, LANES), lambda i: (i, 0))],
        out_shape=[f32(nt, nb, d_ssm), f32(nt, nb, d_ssm), f32(nt, nb, d_ssm),
                   f32(nt, nb, 2 * gn_w), f32(nb, LANES)],
        compiler_params=_params(1),
        name="ssd_sample_pre",
    )(proj3, st3, dt3, cw, cb, dtb, alog, dexp, e, gsel)


def _ssd_sample_state_kernel(s0_ref, c_ref, b_ref, xdd_ref, cd_ref, snew_ref, yoff_ref,
                             *, bt, groups, n_state, headdim):
    heads = s0_ref.shape[1]
    hpg = heads // groups
    gw = hpg * headdim

    def body(b, carry):
        cd_row = cd_ref[pl.ds(b, 1), :]
        cb = c_ref[b].astype(BF16)
        bb = b_ref[b].astype(BF16)
        xb = xdd_ref[b].astype(BF16)
        for g in range(groups):
            s0 = s0_ref[b, g * hpg:(g + 1) * hpg].reshape(gw, n_state)
            yoff = lax.dot_general(cb[:, g * n_state:(g + 1) * n_state], s0.astype(BF16),
                                   (((1,), (1,)), ((), ())), preferred_element_type=F32)
            yoff_ref[b, :, g * gw:(g + 1) * gw] = yoff
            upd = lax.dot_general(xb[:, g * gw:(g + 1) * gw], bb[:, g * n_state:(g + 1) * n_state],
                                  (((0,), (0,)), ((), ())), preferred_element_type=F32)
            dec = jnp.concatenate(
                [jnp.broadcast_to(cd_row[:, h:h + 1], (headdim, n_state))
                 for h in range(g * hpg, (g + 1) * hpg)], axis=0)
            snew_ref[b, g * hpg:(g + 1) * hpg] = (s0 * dec + upd).reshape(hpg, headdim, n_state)
        return carry

    lax.fori_loop(0, bt, body, 0)


def _ssd_sample_state(s0, c_b, b_b, xdd_b, cd, *, bt):
    nb, heads, headdim, n_state = s0.shape
    nt = c_b.shape[1]
    gn_w = c_b.shape[2]
    d_ssm = xdd_b.shape[2]
    return pl.pallas_call(
        functools.partial(_ssd_sample_state_kernel, bt=bt, groups=SSD_GROUPS, n_state=n_state,
                          headdim=headdim),
        grid=(nb // bt,),
        in_specs=[pl.BlockSpec((bt, heads, headdim, n_state), lambda i: (i, 0, 0, 0)),
                  pl.BlockSpec((bt, nt, gn_w), lambda i: (i, 0, 0)),
                  pl.BlockSpec((bt, nt, gn_w), lambda i: (i, 0, 0)),
                  pl.BlockSpec((bt, nt, d_ssm), lambda i: (i, 0, 0)),
                  pl.BlockSpec((bt, LANES), lambda i: (i, 0))],
        out_specs=[pl.BlockSpec((bt, heads, headdim, n_state), lambda i: (i, 0, 0, 0)),
                   pl.BlockSpec((bt, nt, d_ssm), lambda i: (i, 0, 0))],
        out_shape=[jax.ShapeDtypeStruct(s0.shape, F32),
                   jax.ShapeDtypeStruct((nb, nt, d_ssm), F32)],
        compiler_params=_params(1),
        name="ssd_sample_state",
    )(s0, c_b, b_b, xdd_b, cd)


def _ssd_sample_post_kernel(ydp_ref, yoff_ref, ecum_ref, z_ref, gn_ref, out_ref, *, groups):
    nt, bb, d_ssm = ydp_ref.shape
    y = (ydp_ref[...] + yoff_ref[...] * ecum_ref[...]).reshape(nt * bb, d_ssm)
    z = z_ref[...].reshape(nt * bb, d_ssm)
    out_ref[...] = _group_norm_out(y, z, gn_ref, groups).reshape(nt, bb, d_ssm)


def _ssd_sample_post(ydp, yoff, ecum, proj3, gn, d_ssm, d_xbc, *, bb):
    nt, nb, n_main = proj3.shape
    z_blk = (n_main - d_xbc - d_ssm) // d_ssm
    slab = pl.BlockSpec((nt, bb, d_ssm), lambda i: (0, i, 0))
    return pl.pallas_call(
        functools.partial(_ssd_sample_post_kernel, groups=SSD_GROUPS),
        grid=(nb // bb,),
        in_specs=[slab, slab, slab, pl.BlockSpec((nt, bb, d_ssm), lambda i: (0, i, z_blk)),
                  pl.BlockSpec((1, d_ssm), lambda i: (0, 0))],
        out_specs=slab,
        out_shape=jax.ShapeDtypeStruct((nt, nb, d_ssm), BF16),
        compiler_params=_params(1),
        name="ssd_sample_post",
    )(ydp, yoff, ecum, proj3, gn)


def _pad_lanes(v):
    return jnp.pad(v.astype(F32), (0, LANES - v.shape[0])).reshape(1, LANES)


def kernel(x_prompt, x_sample, state_rglru_h, state_rglru_conv, state_ssd_h, state_ssd_conv, state_ffn_conv, g_mix, w_in, rg_conv_w, rg_conv_b, rg_gate_a_w, rg_gate_a_b, rg_gate_i_w, rg_gate_i_b, rg_lambda, g_rg_out, ssd_conv_w, ssd_conv_b, ssd_dt_bias, ssd_A_log, ssd_D, g_ssd_norm, w_out, g_ffn, w_ffn_up, ffn_conv_w, ffn_conv_b, w_ffn_down, g_final):
    depth = g_mix.shape[0]
    pb, seq, d_model = x_prompt.shape
    sb, dec_seq, _ = x_sample.shape
    d_rnn = rg_lambda.shape[1]
    d_ssm = g_ssd_norm.shape[1]
    d_xbc = ssd_conv_w.shape[2]
    d_ff = ffn_conv_w.shape[2]
    heads = ssd_A_log.shape[1]
    headdim = d_ssm // heads
    n_state = state_ssd_h.shape[-1]
    n_main = 2 * d_rnn + d_ssm + d_xbc
    k_rg = rg_conv_w.shape[1]
    k_ssd = ssd_conv_w.shape[1]
    k_ffn = ffn_conv_w.shape[1]
    gn_w = SSD_GROUPS * n_state
    assert heads <= LANES and n_state == LANES and k_ffn == 3 and dec_seq >= max(k_rg, k_ssd) - 1

    hid = lax.broadcasted_iota(jnp.int32, (LANES, d_ssm), 0)
    cid = lax.broadcasted_iota(jnp.int32, (LANES, d_ssm), 1)
    e_sel = (hid == cid // headdim).astype(BF16)
    gid = lax.broadcasted_iota(jnp.int32, (gn_w, d_ssm), 0) // n_state
    cgid = lax.broadcasted_iota(jnp.int32, (gn_w, d_ssm), 1) // (d_ssm // SSD_GROUPS)
    g_sel = (gid == cgid).astype(BF16)

    yp = x_prompt.reshape(pb * seq, d_model)
    ys = jnp.transpose(x_sample, (1, 0, 2)).reshape(dec_seq * sb, d_model)
    p_new = [[], [], [], [], []]
    s_new = [[], [], [], [], []]
    row = lambda v: v.reshape(1, -1)
    swap = lambda a3: jnp.transpose(a3, (1, 0, 2))
    bb = 32

    for l in range(depth):
        w_in_b = w_in[l].astype(BF16)
        wdt_b = jnp.pad(w_in[l][:, n_main:], ((0, 0), (0, LANES - heads))).astype(BF16)
        w_out_b = w_out[l].astype(BF16)
        wa_b = rg_gate_a_w[l].astype(BF16)
        wi_b = rg_gate_i_w[l].astype(BF16)
        rg_args = (rg_conv_w[l], row(rg_conv_b[l]), wa_b, row(rg_gate_a_b[l]), wi_b,
                   row(rg_gate_i_b[l]), row(rg_lambda[l]), row(g_rg_out[l]))
        dtb = _pad_lanes(ssd_dt_bias[l])
        alog = _pad_lanes(ssd_A_log[l])
        dexp = jnp.repeat(ssd_D[l].astype(F32), headdim).reshape(1, d_ssm)
        gn = row(g_ssd_norm[l])

        proj_p, dt_p = _in_proj(yp, row(g_mix[l]), w_in_b, wdt_b, n_main, tm=TM, tn=TN_IN)
        rg_steps = pb * (seq // TT_RG)
        ssd_steps = pb * (seq // SSD_CHUNK)
        rb_ssd = d_model // (2 * rg_steps + ssd_steps)
        assert rb_ssd % 16 == 0 and (2 * rg_steps + ssd_steps) * rb_ssd == d_model
        rg_p, hl_p, w_up_ta = _rglru_prompt(proj_p, pb, seq, d_rnn, *rg_args, w_ffn_up[l], TN_UP,
                                            2 * rb_ssd, tt=TT_RG)
        ssd_p, hfin_p, w_up_tb = _ssd_prompt(proj_p, dt_p, pb, seq, d_ssm, d_xbc, n_state, headdim,
                                             ssd_conv_w[l], row(ssd_conv_b[l]), dtb, alog, dexp, gn, e_sel,
                                             w_ffn_up[l], TN_UP, rb_ssd, 2 * rb_ssd * rg_steps)
        x1_p, h2_p = _out_proj(rg_p, ssd_p, w_out_b, yp, row(g_ffn[l]), tm=TM, tn=TN_OUT)
        act_p, gst_p, w_down_t = _ffn_up_prompt(h2_p, w_up_ta, w_up_tb, ffn_conv_w[l],
                                                row(ffn_conv_b[l]), seq, w_ffn_down[l], TN_DOWN, tm=TM_UP)
        proj_p3 = proj_p.reshape(pb, seq, n_main)
        p_new[0].append(hl_p[:, SUBLANES - 1, :])
        p_new[1].append(proj_p3[:, seq - (k_rg - 1):, :d_rnn])
        p_new[2].append(hfin_p.reshape(pb, heads, headdim, n_state))
        p_new[3].append(proj_p3[:, seq - (k_ssd - 1):, n_main - d_xbc:])
        tiles = seq // TM_UP
        p_new[4].append(gst_p[tiles - 1::tiles, SUBLANES - (k_ffn - 1):, :])

        ms = ys.shape[0]
        proj_s, dt_s = _in_proj(ys, row(g_mix[l]), w_in_b, wdt_b, n_main, tm=ms, tn=TN_IN)
        proj_s3 = proj_s.reshape(dec_seq, sb, n_main)
        rg_s, hl_s = _rglru_sample(proj_s3, swap(state_rglru_conv[l]), state_rglru_h[l], d_rnn,
                                   *rg_args, bb=bb)
        ydp, ecum, xdd, bc, cd = _ssd_sample_pre(
            proj_s3, swap(state_ssd_conv[l]), dt_s.reshape(dec_seq, sb, LANES), d_ssm, d_xbc, n_state,
            ssd_conv_w[l], row(ssd_conv_b[l]), dtb, alog, dexp, e_sel, g_sel, bb=bb)
        bc_b = swap(bc)
        snew, yoff_b = _ssd_sample_state(state_ssd_h[l], bc_b[:, :, gn_w:], bc_b[:, :, :gn_w],
                                         swap(xdd), cd, bt=8)
        ssd_s = _ssd_sample_post(ydp, swap(yoff_b), ecum, proj_s3, gn, d_ssm, d_xbc, bb=bb)
        x1_s, h2_s = _out_proj(rg_s.reshape(ms, d_rnn), ssd_s.reshape(ms, d_ssm), w_out_b, ys,
                               row(g_ffn[l]), tm=ms, tn=TN_OUT)
        act_s, gst_s = _ffn_up_sample(h2_s, w_up_ta, w_up_tb, ffn_conv_w[l], row(ffn_conv_b[l]),
                                      swap(state_ffn_conv[l]).reshape(-1, d_ff), sb)
        s_new[0].append(hl_s)
        s_new[1].append(swap(proj_s3[dec_seq - (k_rg - 1):, :, :d_rnn]))
        s_new[2].append(snew)
        s_new[3].append(swap(proj_s3[dec_seq - (k_ssd - 1):, :, n_main - d_xbc:]))
        s_new[4].append(swap(gst_s.reshape(k_ffn - 1, sb, d_ff)))

        last = l == depth - 1
        yp = _ffn_down(act_p, w_down_t, x1_p, row(g_final), last, tm=TM)
        ys = _ffn_down(act_s, w_down_t, x1_s, row(g_final), last, tm=ms)

    y_prompt = yp.reshape(pb, seq, d_model)
    y_sample = jnp.transpose(ys.reshape(dec_seq, sb, d_model), (1, 0, 2))
    return (y_prompt, y_sample,
            jnp.stack(p_new[0]), jnp.stack(p_new[1]), jnp.stack(p_new[2]), jnp.stack(p_new[3]),
            jnp.stack(p_new[4]),
            jnp.stack(s_new[0]), jnp.stack(s_new[1]), jnp.stack(s_new[2]), jnp.stack(s_new[3]),
            jnp.stack(s_new[4]))
```

```python
import functools

import jax
import jax.numpy as jnp
from jax import lax
from jax.experimental import pallas as pl
from jax.experimental.pallas import tpu as pltpu

F32 = jnp.float32
BF16 = jnp.bfloat16

EPS = 1e-6
RG_C = 8.0
SSD_CHUNK = 128
SSD_GROUPS = 4
LANES = 128
SUBLANES = 8
VMEM_LIMIT = 56 * 2**20

TM = 1024
ROW_CHUNK = 256
TT_RG = 256
TN_IN = 512
TN_OUT = 512
TN_UP = 512
TN_DOWN = 512
TK_DOWN = 4096


def _params(n_axes):
    return pltpu.CompilerParams(dimension_semantics=("arbitrary",) * n_axes,
                                vmem_limit_bytes=VMEM_LIMIT)


def _dot(a, b):
    return jnp.dot(a, b, preferred_element_type=F32)


def _rms_rows(x, g):
    ms = jnp.mean(x * x, axis=-1, keepdims=True)
    return (x * lax.rsqrt(ms + EPS)) * g


def _split3(x):
    hi = x.astype(BF16)
    r = x - hi.astype(F32)
    mid = r.astype(BF16)
    lo = (r - mid.astype(F32)).astype(BF16)
    return hi, mid, lo


def _dot_sel_right(x, sel):
    hi, mid, lo = _split3(x)
    return (_dot(hi, sel) + _dot(mid, sel)) + _dot(lo, sel)


def _dot_sel_left(sel, x):
    hi, mid, lo = _split3(x)
    return (_dot(sel, hi) + _dot(sel, mid)) + _dot(sel, lo)


def _inproj_kernel(x_hbm, g_ref, w_ref, wdt_ref, *refs, ni, tm, cast_w):
    if cast_w:
        o_ref, dt_ref, wout_ref, x_buf, hn_ref, sem = refs
    else:
        o_ref, dt_ref, x_buf, hn_ref, sem = refs
    i = pl.program_id(0)
    j = pl.program_id(1)

    def x_in(tile):
        return pltpu.make_async_copy(x_hbm.at[pl.ds(tile * tm, tm)], x_buf, sem)

    @pl.when(j == 0)
    def _():
        @pl.when(i == 0)
        def _():
            x_in(0).start()

        x_in(i).wait()
        for r in range(0, tm, ROW_CHUNK):
            hn_ref[r:r + ROW_CHUNK, :] = _rms_rows(x_buf[r:r + ROW_CHUNK, :], g_ref[...]).astype(BF16)
        dt_ref[...] = _dot(hn_ref[...], wdt_ref[...])

        @pl.when(i + 1 < ni)
        def _():
            x_in(i + 1).start()

    if cast_w:
        w = w_ref[...].astype(BF16)
        wout_ref[...] = w
    else:
        w = w_ref[...]
    o_ref[...] = _dot(hn_ref[...], w)


def _cast_col_tiles(src_ref, dst_ref):
    n_tiles, _, tn = dst_ref.shape
    for t in range(n_tiles):
        dst_ref[t] = src_ref[:, t * tn:(t + 1) * tn].astype(BF16)


def _in_proj(x2d, g, w, wdt_b, n_main, *, tm, tn):
    m, d = x2d.shape
    cast_w = w.dtype != BF16
    ni = m // tm
    out_specs = [pl.BlockSpec((tm, tn), lambda i, j: (i, j)),
                 pl.BlockSpec((tm, LANES), lambda i, j: (i, 0))]
    out_shape = [jax.ShapeDtypeStruct((m, n_main), F32), jax.ShapeDtypeStruct((m, LANES), F32)]
    if cast_w:
        assert ni == 1, "the bf16 copy of the weight is written once, by a single row tile"
        out_specs.append(pl.BlockSpec((d, tn), lambda i, j: (0, j)))
        out_shape.append(jax.ShapeDtypeStruct((d, n_main), BF16))
    return pl.pallas_call(
        functools.partial(_inproj_kernel, ni=ni, tm=tm, cast_w=cast_w),
        grid=(ni, n_main // tn),
        in_specs=[pl.BlockSpec(memory_space=pl.ANY),
                  pl.BlockSpec((1, d), lambda i, j: (0, 0)),
                  pl.BlockSpec((d, tn), lambda i, j: (0, j)),
                  pl.BlockSpec((d, LANES), lambda i, j: (0, 0))],
        out_specs=out_specs,
        out_shape=out_shape,
        scratch_shapes=[pltpu.VMEM((tm, d), F32), pltpu.VMEM((tm, d), BF16), pltpu.SemaphoreType.DMA(())],
        compiler_params=_params(2),
        name="in_proj",
    )(x2d, g, w, wdt_b)


def _outproj_kernel(rg_ref, ssd_ref, wt_ref, wb_ref, x_ref, g_ref, x1_ref, xg_ref, ss_ref):
    j = pl.program_id(1)
    x1 = x_ref[...] + (_dot(rg_ref[...], wt_ref[...]) + _dot(ssd_ref[...], wb_ref[...]))
    x1_ref[...] = x1
    xg_ref[...] = (x1 * g_ref[...]).astype(BF16)
    part = jnp.broadcast_to(jnp.sum(x1 * x1, axis=-1, keepdims=True), ss_ref.shape)

    @pl.when(j == 0)
    def _():
        ss_ref[...] = part

    @pl.when(j > 0)
    def _():
        ss_ref[...] += part


def _out_proj(rg, ssd, w_b, x2d, g, *, tm, tn):
    m, d = x2d.shape
    k_half = rg.shape[1]
    return pl.pallas_call(
        _outproj_kernel,
        grid=(m // tm, d // tn),
        in_specs=[pl.BlockSpec((tm, k_half), lambda i, j: (i, 0)),
                  pl.BlockSpec((tm, k_half), lambda i, j: (i, 0)),
                  pl.BlockSpec((k_half, tn), lambda i, j: (0, j)),
                  pl.BlockSpec((k_half, tn), lambda i, j: (1, j)),
                  pl.BlockSpec((tm, tn), lambda i, j: (i, j)),
                  pl.BlockSpec((1, tn), lambda i, j: (0, j))],
        out_specs=[pl.BlockSpec((tm, tn), lambda i, j: (i, j)),
                   pl.BlockSpec((tm, tn), lambda i, j: (i, j)),
                   pl.BlockSpec((tm, LANES), lambda i, j: (i, 0))],
        out_shape=[jax.ShapeDtypeStruct((m, d), F32),
                   jax.ShapeDtypeStruct((m, d), BF16),
                   jax.ShapeDtypeStruct((m, LANES), F32)],
        compiler_params=_params(2),
        name="out_proj",
    )(rg, ssd, w_b, w_b, x2d, g)


def _glu(gate_c, val):
    return (jax.nn.gelu(gate_c) * val).astype(BF16)


def _dot_split_k(h_ref, wa_ref, wb_ref):
    ka = wa_ref.shape[0]
    return _dot(h_ref[:, :ka], wa_ref[...]) + _dot(h_ref[:, ka:], wb_ref[...])


def _inv_rms(ss_ref, width):
    return lax.rsqrt(ss_ref[:, 0:1] / width + EPS)


def _ffn_up_prompt_kernel(xg_ref, ss_ref, wga_ref, wgb_ref, wva_ref, wvb_ref, cw_ref, cb_ref, wsrc_ref,
                          act_ref, gst_ref, wdst_ref, carry_ref, *, tiles_per_seq):
    i = pl.program_id(0)
    j = pl.program_id(1)
    _cast_col_tiles(wsrc_ref, wdst_ref)
    inv = _inv_rms(ss_ref, xg_ref.shape[1])
    gate = _dot_split_k(xg_ref, wga_ref, wgb_ref) * inv
    val = _dot_split_k(xg_ref, wva_ref, wvb_ref) * inv
    tm, tn = gate.shape
    w0, w1, w2 = cw_ref[0:1, :], cw_ref[1:2, :], cw_ref[2:3, :]
    cb = cb_ref[...]
    g1 = pltpu.roll(gate, 1, 0)
    g2 = pltpu.roll(gate, 2, 0)
    act_ref[...] = _glu(((cb + g2 * w0) + g1 * w1) + gate * w2, val)
    @pl.when((i % tiles_per_seq) == 0)
    def _():
        carry_ref[j] = jnp.zeros((SUBLANES, tn), F32)

    prev = carry_ref[j]
    rows = lax.broadcasted_iota(jnp.int32, (SUBLANES, tn), 0)
    t1 = jnp.where(rows < 1, pltpu.roll(prev, 1, 0), g1[0:SUBLANES])
    t2 = jnp.where(rows < 2, pltpu.roll(prev, 2, 0), g2[0:SUBLANES])
    act_ref[0:SUBLANES, :] = _glu(((cb + t2 * w0) + t1 * w1) + gate[0:SUBLANES] * w2,
                                  val[0:SUBLANES])
    last = gate[tm - SUBLANES:tm]
    carry_ref[j] = last
    gst_ref[...] = last


def _ffn_up_prompt(h2, ss, w_ta, w_tb, cw, cb, seq, w_next, tn_next, *, tm):
    m, d = h2.shape
    nj, ka, tn = w_ta.shape[0] // 2, w_ta.shape[1], w_ta.shape[2]
    kb = w_tb.shape[1]
    assert ka + kb == d and ka % LANES == 0
    d_ff = nj * tn
    steps = (m // tm) * nj
    k_next, n_next = w_next.shape
    rb = k_next // steps
    assert rb * steps == k_next and rb % 16 == 0 and n_next % tn_next == 0
    return pl.pallas_call(
        functools.partial(_ffn_up_prompt_kernel, tiles_per_seq=seq // tm),
        grid=(m // tm, nj),
        in_specs=[pl.BlockSpec((tm, d), lambda i, j: (i, 0)),
                  pl.BlockSpec((tm, LANES), lambda i, j: (i, 0)),
                  pl.BlockSpec((None, ka, tn), lambda i, j: (j, 0, 0)),
                  pl.BlockSpec((None, kb, tn), lambda i, j: (j, 0, 0)),
                  pl.BlockSpec((None, ka, tn), lambda i, j: (j + nj, 0, 0)),
                  pl.BlockSpec((None, kb, tn), lambda i, j: (j + nj, 0, 0)),
                  pl.BlockSpec((3, tn), lambda i, j: (0, j)),
                  pl.BlockSpec((1, tn), lambda i, j: (0, j)),
                  pl.BlockSpec((rb, n_next), lambda i, j: (i * nj + j, 0))],
        out_specs=[pl.BlockSpec((tm, tn), lambda i, j: (i, j)),
                   pl.BlockSpec((None, SUBLANES, tn), lambda i, j: (i, 0, j)),
                   pl.BlockSpec((n_next // tn_next, rb, tn_next), lambda i, j: (0, i * nj + j, 0))],
        out_shape=[jax.ShapeDtypeStruct((m, d_ff), BF16),
                   jax.ShapeDtypeStruct((m // tm, SUBLANES, d_ff), F32),
                   jax.ShapeDtypeStruct((n_next // tn_next, k_next, tn_next), BF16)],
        scratch_shapes=[pltpu.VMEM((nj, SUBLANES, tn), F32)],
        compiler_params=_params(2),
        name="ffn_up_prompt",
    )(h2, ss, w_ta, w_tb, w_ta, w_tb, cw, cb, w_next)


def _ffn_up_sample_kernel(xg_ref, ss_ref, wga_ref, wgb_ref, wva_ref, wvb_ref, cw_ref, cb_ref, st_ref,
                          act_ref, nst_ref, *, nb, nt):
    inv = _inv_rms(ss_ref, xg_ref.shape[1])
    gate = _dot_split_k(xg_ref, wga_ref, wgb_ref) * inv
    val = _dot_split_k(xg_ref, wva_ref, wvb_ref) * inv
    w0, w1, w2 = cw_ref[0:1, :], cw_ref[1:2, :], cw_ref[2:3, :]
    cb = cb_ref[...]
    ext = [st_ref[0:nb, :], st_ref[nb:2 * nb, :]] + [gate[t * nb:(t + 1) * nb] for t in range(nt)]
    for t in range(nt):
        gc = ((cb + ext[t] * w0) + ext[t + 1] * w1) + ext[t + 2] * w2
        act_ref[t * nb:(t + 1) * nb, :] = _glu(gc, val[t * nb:(t + 1) * nb])
    nst_ref[0:nb, :] = ext[nt]
    nst_ref[nb:2 * nb, :] = ext[nt + 1]


def _ffn_up_sample(h2, ss, w_ta, w_tb, cw, cb, st, nb):
    m, d = h2.shape
    nj, ka, tn = w_ta.shape[0] // 2, w_ta.shape[1], w_ta.shape[2]
    kb = w_tb.shape[1]
    d_ff = nj * tn
    return pl.pallas_call(
        functools.partial(_ffn_up_sample_kernel, nb=nb, nt=m // nb),
        grid=(nj,),
        in_specs=[pl.BlockSpec((m, d), lambda j: (0, 0)),
                  pl.BlockSpec((m, LANES), lambda j: (0, 0)),
                  pl.BlockSpec((None, ka, tn), lambda j: (j, 0, 0)),
                  pl.BlockSpec((None, kb, tn), lambda j: (j, 0, 0)),
                  pl.BlockSpec((None, ka, tn), lambda j: (j + nj, 0, 0)),
                  pl.BlockSpec((None, kb, tn), lambda j: (j + nj, 0, 0)),
                  pl.BlockSpec((3, tn), lambda j: (0, j)),
                  pl.BlockSpec((1, tn), lambda j: (0, j)),
                  pl.BlockSpec((2 * nb, tn), lambda j: (0, j))],
        out_specs=[pl.BlockSpec((m, tn), lambda j: (0, j)),
                   pl.BlockSpec((2 * nb, tn), lambda j: (0, j))],
        out_shape=[jax.ShapeDtypeStruct((m, d_ff), BF16),
                   jax.ShapeDtypeStruct((2 * nb, d_ff), F32)],
        compiler_params=_params(1),
        name="ffn_up_sample",
    )(h2, ss, w_ta, w_tb, w_ta, w_tb, cw, cb, st)


def _ffn_down_kernel(act_ref, w_ref, x1_ref, g_ref, y_hbm, rows_ref, sem,
                     *, ni, nk, nj, tm, tn, final_norm):
    i = pl.program_id(0)
    k = pl.program_id(1)
    j = pl.program_id(2)
    cols = pl.ds(pl.multiple_of(j * tn, tn), tn)

    def rows_out(tile):
        return pltpu.make_async_copy(rows_ref, y_hbm.at[pl.ds(tile * tm, tm)], sem)

    @pl.when(k == 0)
    def _():
        @pl.when((j == 0) & (i > 0))
        def _():
            rows_out(i - 1).wait()

        rows_ref[:, cols] = x1_ref[...]

    rows_ref[:, cols] += _dot(act_ref[...], w_ref[...])

    @pl.when((k == nk - 1) & (j == nj - 1))
    def _():
        if final_norm:
            for r in range(0, tm, ROW_CHUNK):
                rows_ref[r:r + ROW_CHUNK, :] = _rms_rows(rows_ref[r:r + ROW_CHUNK, :], g_ref[...])
        rows_out(i).start()

        @pl.when(i == ni - 1)
        def _():
            rows_out(i).wait()


def _ffn_down(act, w_t, x1, g, final_norm, *, tm, tk):
    m, d = x1.shape
    nj, d_ff, tn = w_t.shape
    ni, nk = m // tm, d_ff // tk
    return pl.pallas_call(
        functools.partial(_ffn_down_kernel, ni=ni, nk=nk, nj=nj, tm=tm, tn=tn, final_norm=final_norm),
        grid=(ni, nk, nj),
        in_specs=[pl.BlockSpec((tm, tk), lambda i, k, j: (i, k)),
                  pl.BlockSpec((None, tk, tn), lambda i, k, j: (j, k, 0)),
                  pl.BlockSpec((tm, tn), lambda i, k, j: (i, jnp.where(k == 0, j, nj - 1))),
                  pl.BlockSpec((1, d), lambda i, k, j: (0, 0))],
        out_specs=pl.BlockSpec(memory_space=pl.ANY),
        out_shape=jax.ShapeDtypeStruct((m, d), F32),
        scratch_shapes=[pltpu.VMEM((tm, d), F32), pltpu.SemaphoreType.DMA(())],
        compiler_params=_params(3),
        name="ffn_down",
    )(act, w_t, x1, g)


def _softplus(x):
    return jnp.maximum(x, 0.0) + jnp.log1p(jnp.exp(-jnp.abs(x)))


def _rglru_coeffs(xc, wa_ref, ba_ref, wi_ref, bi_ref, lam_ref):
    xb = xc.astype(BF16)
    heads, blk = wa_ref.shape[0], wa_ref.shape[1]
    ga = jnp.concatenate([_dot(xb[:, h * blk:(h + 1) * blk], wa_ref[h]) for h in range(heads)], axis=1)
    gi = jnp.concatenate([_dot(xb[:, h * blk:(h + 1) * blk], wi_ref[h]) for h in range(heads)], axis=1)
    gate_r = jax.nn.sigmoid(ga + ba_ref[...])
    gate_i = jax.nn.sigmoid(gi + bi_ref[...])
    log_a = (-RG_C * gate_r) * _softplus(-lam_ref[...])
    a = jnp.exp(log_a)
    bx = jnp.sqrt(-jnp.tanh(log_a) * (a * a + 1.0)) * (gate_i * xc)
    return a, bx


def _rg_out(h, gate, g_ref):
    return _rms_rows(h * jax.nn.gelu(gate), g_ref[...]).astype(BF16)


def _conv_tile(x, prev, w_ref, b_ref, k_taps):
    rows = lax.broadcasted_iota(jnp.int32, (SUBLANES, x.shape[1]), 0)
    out = b_ref[...]
    top = b_ref[...]
    for k in range(k_taps):
        s = k_taps - 1 - k
        wk = w_ref[k:k + 1, :]
        if s == 0:
            r, t8 = x, x[0:SUBLANES]
        else:
            r = pltpu.roll(x, s, 0)
            t8 = jnp.where(rows < s, pltpu.roll(prev, s, 0), r[0:SUBLANES])
        out = out + r * wk
        top = top + t8 * wk
    return jnp.concatenate([top, out[SUBLANES:]], axis=0)


def _rglru_prompt_kernel(x_ref, gate_ref, cw_ref, cb_ref, wa_ref, ba_ref, wi_ref, bi_ref, lam_ref,
                         g_ref, wsrc_ref, out_ref, hlast_ref, wdst_ref, xprev_ref, hc_ref, a_s, b_s, h_s,
                         *, k_taps):
    c = pl.program_id(1)
    _cast_col_tiles(wsrc_ref, wdst_ref)

    @pl.when(c == 0)
    def _():
        xprev_ref[...] = jnp.zeros_like(xprev_ref)
        hc_ref[...] = jnp.zeros_like(hc_ref)

    x = x_ref[...]
    t_len, width = x.shape
    xc = _conv_tile(x, xprev_ref[...], cw_ref, cb_ref, k_taps)
    xprev_ref[...] = x[t_len - SUBLANES:t_len]
    a, b = _rglru_coeffs(xc, wa_ref, ba_ref, wi_ref, bi_ref, lam_ref)
    pos = lax.broadcasted_iota(jnp.int32, (t_len, 1), 0) % SUBLANES
    for s in (1, 2, 4):
        a_sh = pltpu.roll(a, s, 0)
        b_sh = pltpu.roll(b, s, 0)
        m = pos >= s
        b = jnp.where(m, a * b_sh + b, b)
        a = jnp.where(m, a * a_sh, a)
    a_s[...] = a
    b_s[...] = b
    h = hc_ref[...]
    for grp in range(t_len // SUBLANES):
        lo, hi = grp * SUBLANES, (grp + 1) * SUBLANES
        hl = jnp.broadcast_to(h[SUBLANES - 1:SUBLANES, :], (SUBLANES, width))
        h = a_s[lo:hi, :] * hl + b_s[lo:hi, :]
        h_s[lo:hi, :] = h
    hc_ref[...] = h
    hlast_ref[...] = h
    out_ref[...] = _rg_out(h_s[...], gate_ref[...], g_ref)


def _rglru_prompt(proj, nb, seq, d_rnn, cw, cb, wa, ba, wi, bi, lam, g, w_next, tn_next, rb, *, tt):
    nc = seq // tt
    heads, blk = wa.shape[0], wa.shape[1]
    k_next, n_next = w_next.shape
    vec = pl.BlockSpec((1, d_rnn), lambda b, c: (0, 0))
    gate_w = pl.BlockSpec((heads, blk, blk), lambda b, c: (0, 0, 0))
    return pl.pallas_call(
        functools.partial(_rglru_prompt_kernel, k_taps=cw.shape[0]),
        grid=(nb, nc),
        in_specs=[pl.BlockSpec((tt, d_rnn), lambda b, c: (b * nc + c, 0)),
                  pl.BlockSpec((tt, d_rnn), lambda b, c: (b * nc + c, 1)),
                  pl.BlockSpec((cw.shape[0], d_rnn), lambda b, c: (0, 0)),
                  vec, gate_w, vec, gate_w, vec, vec, vec,
                  pl.BlockSpec((rb, n_next), lambda b, c: (b * nc + c, 0))],
        out_specs=[pl.BlockSpec((tt, d_rnn), lambda b, c: (b * nc + c, 0)),
                   pl.BlockSpec((None, SUBLANES, d_rnn), lambda b, c: (b, 0, 0)),
                   pl.BlockSpec((n_next // tn_next, rb, tn_next), lambda b, c: (0, b * nc + c, 0))],
        out_shape=[jax.ShapeDtypeStruct((nb * seq, d_rnn), BF16),
                   jax.ShapeDtypeStruct((nb, SUBLANES, d_rnn), F32),
                   jax.ShapeDtypeStruct((n_next // tn_next, nb * nc * rb, tn_next), BF16)],
        scratch_shapes=[pltpu.VMEM((SUBLANES, d_rnn), F32), pltpu.VMEM((SUBLANES, d_rnn), F32),
                        pltpu.VMEM((tt, d_rnn), F32), pltpu.VMEM((tt, d_rnn), F32),
                        pltpu.VMEM((tt, d_rnn), F32)],
        compiler_params=_params(2),
        name="rglru_prompt",
    )(proj, proj, cw, cb, wa, ba, wi, bi, lam, g, w_next)


def _conv_slabs(st_ref, x_ref, w_ref, b_ref):
    k_taps = w_ref.shape[0]
    ext = [st_ref[s] for s in range(k_taps - 1)] + [x_ref[t] for t in range(x_ref.shape[0])]
    convs = []
    for t in range(x_ref.shape[0]):
        acc = b_ref[...]
        for k in range(k_taps):
            acc = acc + ext[t + k] * w_ref[k:k + 1, :]
        convs.append(acc)
    return jnp.concatenate(convs, axis=0)


def _rglru_sample_kernel(x_ref, gate_ref, st_ref, h0_ref, cw_ref, cb_ref, wa_ref, ba_ref, wi_ref,
                         bi_ref, lam_ref, g_ref, out_ref, hlast_ref, h_s):
    nt, bb, width = x_ref.shape
    xc = _conv_slabs(st_ref, x_ref, cw_ref, cb_ref)
    a, b = _rglru_coeffs(xc, wa_ref, ba_ref, wi_ref, bi_ref, lam_ref)
    h = h0_ref[...]
    for t in range(nt):
        h = a[t * bb:(t + 1) * bb] * h + b[t * bb:(t + 1) * bb]
        h_s[t * bb:(t + 1) * bb, :] = h
    hlast_ref[...] = h
    gate = gate_ref[...].reshape(nt * bb, width)
    out_ref[...] = _rg_out(h_s[...], gate, g_ref).reshape(nt, bb, width)


def _rglru_sample(proj3, st3, h0, d_rnn, cw, cb, wa, ba, wi, bi, lam, g, *, bb):
    nt, nb, _ = proj3.shape
    heads, blk = wa.shape[0], wa.shape[1]
    k_taps = cw.shape[0]
    vec = pl.BlockSpec((1, d_rnn), lambda i: (0, 0))
    gate_w = pl.BlockSpec((heads, blk, blk), lambda i: (0, 0, 0))
    return pl.pallas_call(
        _rglru_sample_kernel,
        grid=(nb // bb,),
        in_specs=[pl.BlockSpec((nt, bb, d_rnn), lambda i: (0, i, 0)),
                  pl.BlockSpec((nt, bb, d_rnn), lambda i: (0, i, 1)),
                  pl.BlockSpec((k_taps - 1, bb, d_rnn), lambda i: (0, i, 0)),
                  pl.BlockSpec((bb, d_rnn), lambda i: (i, 0)),
                  pl.BlockSpec((k_taps, d_rnn), lambda i: (0, 0)),
                  vec, gate_w, vec, gate_w, vec, vec, vec],
        out_specs=[pl.BlockSpec((nt, bb, d_rnn), lambda i: (0, i, 0)),
                   pl.BlockSpec((bb, d_rnn), lambda i: (i, 0))],
        out_shape=[jax.ShapeDtypeStruct((nt, nb, d_rnn), BF16),
                   jax.ShapeDtypeStruct((nb, d_rnn), F32)],
        scratch_shapes=[pltpu.VMEM((nt * bb, d_rnn), F32)],
        compiler_params=_params(1),
        name="rglru_sample",
    )(proj3, proj3, st3, h0, cw, cb, wa, ba, wi, bi, lam, g)


def _silu(x):
    return x * jax.nn.sigmoid(x)


def _group_norm_out(y, z, gn_ref, groups):
    u = y * _silu(z)
    gw = u.shape[1] // groups
    outs = []
    for g in range(groups):
        ug = u[:, g * gw:(g + 1) * gw]
        ms = jnp.mean(ug * ug, axis=-1, keepdims=True)
        outs.append((ug * lax.rsqrt(ms + EPS)) * gn_ref[:, g * gw:(g + 1) * gw])
    return jnp.concatenate(outs, axis=1).astype(BF16)


def _ssd_prompt_kernel(z_ref, xbc_ref, dt_ref, cw_ref, cb_ref, dtb_ref, alog_ref, dexp_ref, gn_ref,
                       e_ref, wsrc_ref, out_ref, hfin_ref, wdst_ref, carry_ref, s_ref, y_s,
                       *, k_taps, d_ssm, n_state, groups, headdim, nc):
    c = pl.program_id(1)
    _cast_col_tiles(wsrc_ref, wdst_ref)

    @pl.when(c == 0)
    def _():
        carry_ref[...] = jnp.zeros_like(carry_ref)
        s_ref[...] = jnp.zeros_like(s_ref)

    xbc = xbc_ref[...]
    q = xbc.shape[0]
    xc = _silu(_conv_tile(xbc, carry_ref[...], cw_ref, cb_ref, k_taps))
    carry_ref[...] = xbc[q - SUBLANES:q]
    gn_w = groups * n_state
    xs = xc[:, :d_ssm]
    bm = xc[:, d_ssm:d_ssm + gn_w]
    cm = xc[:, d_ssm + gn_w:d_ssm + 2 * gn_w]

    dt = _softplus(dt_ref[...] + dtb_ref[...])
    dta = dt * (-jnp.exp(alog_ref[...]))
    row = lax.broadcasted_iota(jnp.int32, (q, q), 0)
    col = lax.broadcasted_iota(jnp.int32, (q, q), 1)
    causal = row >= col
    tril = jnp.where(causal, 1.0, 0.0).astype(BF16)
    cum = _dot_sel_left(tril, dta)
    cum_t = cum.T
    e = e_ref[...]
    dt_e = _dot_sel_right(dt, e)
    cum_e = _dot_sel_right(cum, e)
    ecum_e = jnp.exp(cum_e)
    xdt = xs * dt_e
    cum_last = cum_e[q - 1:q, :]
    xdd = (xdt * jnp.exp(cum_last - cum_e)).astype(BF16)
    chunk_decay = jnp.exp(cum_last)

    heads_per_group = d_ssm // headdim // groups
    pair_w = 2 * headdim
    gw = heads_per_group * headdim
    lane = lax.broadcasted_iota(jnp.int32, (q, pair_w), 1)
    for g in range(groups):
        cg = cm[:, g * n_state:(g + 1) * n_state].astype(BF16)
        bg32 = bm[:, g * n_state:(g + 1) * n_state]
        bg = bg32.astype(BF16)
        cb_g = lax.dot_general(cg, bg, (((1,), (1,)), ((), ())), preferred_element_type=F32)
        s_old = s_ref[:, g * gw:(g + 1) * gw]
        y_s[:, g * gw:(g + 1) * gw] = _dot(cg, s_old.astype(BF16)) * ecum_e[:, g * gw:(g + 1) * gw]
        st_g = _dot(bg32.T.astype(BF16), xdd[:, g * gw:(g + 1) * gw])
        s_ref[:, g * gw:(g + 1) * gw] = s_old * chunk_decay[:, g * gw:(g + 1) * gw] + st_g
        for pr in range(heads_per_group // 2):
            h0 = g * heads_per_group + 2 * pr
            ms = []
            for h in (h0, h0 + 1):
                seg = (jnp.broadcast_to(cum[:, h:h + 1], (q, q))
                       - jnp.broadcast_to(cum_t[h:h + 1, :], (q, q)))
                ms.append(cb_g * jnp.exp(jnp.where(causal, seg, -jnp.inf)))
            lo_c = h0 * headdim
            xp = xdt[:, lo_c:lo_c + pair_w]
            lhs = jnp.concatenate(ms, axis=1).astype(BF16)
            rhs = jnp.concatenate([jnp.where(lane < headdim, xp, 0.0),
                                   jnp.where(lane >= headdim, xp, 0.0)], axis=0).astype(BF16)
            y_s[:, lo_c:lo_c + pair_w] += _dot(lhs, rhs)

    y = y_s[...] + dexp_ref[...] * xs
    out_ref[...] = _group_norm_out(y, z_ref[...], gn_ref, groups)

    @pl.when(c == nc - 1)
    def _():
        for blk in range(d_ssm // LANES):
            hfin_ref[blk * LANES:(blk + 1) * LANES, :] = s_ref[:, blk * LANES:(blk + 1) * LANES].T


def _ssd_prompt(proj, dt_raw, nb, seq, d_ssm, d_xbc, n_state, headdim, cw, cb, dtb, alog, dexp, gn, e,
                w_next, tn_next, rb, row0):
    q = SSD_CHUNK
    nc = seq // q
    z_blk = (proj.shape[1] - d_xbc - d_ssm) // d_ssm
    x_blk = (proj.shape[1] - d_xbc) // d_xbc
    assert z_blk * d_ssm + d_ssm + d_xbc == proj.shape[1] and x_blk * d_xbc + d_xbc == proj.shape[1]
    k_next, n_next = w_next.shape
    n_tiles = n_next // tn_next
    blk0 = row0 // rb
    assert blk0 * rb == row0 and row0 + nb * nc * rb == k_next
    one = lambda w: pl.BlockSpec((1, w), lambda b, c: (0, 0))
    return pl.pallas_call(
        functools.partial(_ssd_prompt_kernel, k_taps=cw.shape[0], d_ssm=d_ssm, n_state=n_state,
                          groups=SSD_GROUPS, headdim=headdim, nc=nc),
        grid=(nb, nc),
        in_specs=[pl.BlockSpec((q, d_ssm), lambda b, c: (b * nc + c, z_blk)),
                  pl.BlockSpec((q, d_xbc), lambda b, c: (b * nc + c, x_blk)),
                  pl.BlockSpec((q, LANES), lambda b, c: (b * nc + c, 0)),
                  pl.BlockSpec((cw.shape[0], d_xbc), lambda b, c: (0, 0)),
                  one(d_xbc), one(LANES), one(LANES), one(d_ssm), one(d_ssm),
                  pl.BlockSpec((LANES, d_ssm), lambda b, c: (0, 0)),
                  pl.BlockSpec((rb, n_next), lambda b, c: (blk0 + b * nc + c, 0))],
        out_specs=[pl.BlockSpec((q, d_ssm), lambda b, c: (b * nc + c, 0)),
                   pl.BlockSpec((None, d_ssm, n_state), lambda b, c: (b, 0, 0)),
                   pl.BlockSpec((n_tiles, rb, tn_next), lambda b, c: (0, b * nc + c, 0))],
        out_shape=[jax.ShapeDtypeStruct((nb * seq, d_ssm), BF16),
                   jax.ShapeDtypeStruct((nb, d_ssm, n_state), F32),
                   jax.ShapeDtypeStruct((n_tiles, k_next - row0, tn_next), BF16)],
        scratch_shapes=[pltpu.VMEM((SUBLANES, d_xbc), F32), pltpu.VMEM((n_state, d_ssm), F32),
                        pltpu.VMEM((q, d_ssm), F32)],
        compiler_params=_params(2),
        name="ssd_prompt",
    )(proj, proj, dt_raw, cw, cb, dtb, alog, dexp, gn, e, w_next)


def _ssd_sample_pre_kernel(xbc_ref, st_ref, dt_ref, cw_ref, cb_ref, dtb_ref, alog_ref, dexp_ref,
                           e_ref, gsel_ref, ydp_ref, ecum_ref, xdd_ref, bc_ref, cd_ref,
                           *, d_ssm, n_state, groups):
    nt, bb, _ = xbc_ref.shape
    xc = _silu(_conv_slabs(st_ref, xbc_ref, cw_ref, cb_ref))
    gn_w = groups * n_state
    xs = xc[:, :d_ssm]
    bm = xc[:, d_ssm:d_ssm + gn_w]
    cm = xc[:, d_ssm + gn_w:d_ssm + 2 * gn_w]
    bc_ref[...] = xc[:, d_ssm:d_ssm + 2 * gn_w].reshape(nt, bb, 2 * gn_w)

    dt = _softplus(dt_ref[...].reshape(nt * bb, LANES) + dtb_ref[...])
    dta = dt * (-jnp.exp(alog_ref[...]))
    cums = [dta[0:bb]]
    for t in range(1, nt):
        cums.append(cums[-1] + dta[t * bb:(t + 1) * bb])
    cum = jnp.concatenate(cums, axis=0)
    cd_ref[...] = jnp.exp(cums[-1])
    e = e_ref[...]
    dt_e = _dot_sel_right(dt, e)
    cum_e = _dot_sel_right(cum, e)
    ecum_ref[...] = jnp.exp(cum_e).reshape(nt, bb, d_ssm)
    xdt = xs * dt_e
    sl = lambda v, t: v[t * bb:(t + 1) * bb]
    cum_last = sl(cum_e, nt - 1)
    gsel = gsel_ref[...]
    for t in range(nt):
        xdd_ref[t] = sl(xdt, t) * jnp.exp(cum_last - sl(cum_e, t))
        acc = dexp_ref[...] * sl(xs, t)
        for s in range(t + 1):
            cb_e = _dot_sel_right(sl(cm, t) * sl(bm, s), gsel)
            acc = acc + (cb_e * jnp.exp(sl(cum_e, t) - sl(cum_e, s))) * sl(xdt, s)
        ydp_ref[t] = acc


def _ssd_sample_pre(proj3, st3, dt3, d_ssm, d_xbc, n_state, cw, cb, dtb, alog, dexp, e, gsel, *, bb):
    nt, nb, n_main = proj3.shape
    k_taps = cw.shape[0]
    x_blk = (n_main - d_xbc) // d_xbc
    gn_w = SSD_GROUPS * n_state
    full = lambda r, w: pl.BlockSpec((r, w), lambda i: (0, 0))
    slab = lambda n, w: pl.BlockSpec((n, bb, w), lambda i: (0, i, 0))
    f32 = lambda *s: jax.ShapeDtypeStruct(s, F32)
    return pl.pallas_call(
        functools.partial(_ssd_sample_pre_kernel, d_ssm=d_ssm, n_state=n_state, groups=SSD_GROUPS),
        grid=(nb // bb,),
        in_specs=[pl.BlockSpec((nt, bb, d_xbc), lambda i: (0, i, x_blk)),
                  slab(k_taps - 1, d_xbc), slab(nt, LANES), full(k_taps, d_xbc),
                  full(1, d_xbc), full(1, LANES), full(1, LANES), full(1, d_ssm),
                  full(LANES, d_ssm), full(gn_w, d_ssm)],
        out_specs=[slab(nt, d_ssm), slab(nt, d_ssm), slab(nt, d_ssm), slab(nt, 2 * gn_w),
                   pl.BlockSpec((bb, LANES), lambda i: (i, 0))],
        out_shape=[f32(nt, nb, d_ssm), f32(nt, nb, d_ssm), f32(nt, nb, d_ssm),
                   f32(nt, nb, 2 * gn_w), f32(nb, LANES)],
        compiler_params=_params(1),
        name="ssd_sample_pre",
    )(proj3, st3, dt3, cw, cb, dtb, alog, dexp, e, gsel)


def _ssd_sample_state_kernel(s0_ref, c_ref, b_ref, xdd_ref, cd_ref, snew_ref, yoff_ref,
                             *, bt, groups, n_state, headdim):
    heads = s0_ref.shape[1]
    hpg = heads // groups
    gw = hpg * headdim

    def body(b, carry):
        cd_row = cd_ref[pl.ds(b, 1), :]
        cb = c_ref[b].astype(BF16)
        bb = b_ref[b].astype(BF16)
        xb = xdd_ref[b].astype(BF16)
        for g in range(groups):
            s0 = s0_ref[b, g * hpg:(g + 1) * hpg].reshape(gw, n_state)
            yoff = lax.dot_general(cb[:, g * n_state:(g + 1) * n_state], s0.astype(BF16),
                                   (((1,), (1,)), ((), ())), preferred_element_type=F32)
            yoff_ref[b, :, g * gw:(g + 1) * gw] = yoff
            upd = lax.dot_general(xb[:, g * gw:(g + 1) * gw], bb[:, g * n_state:(g + 1) * n_state],
                                  (((0,), (0,)), ((), ())), preferred_element_type=F32)
            dec = jnp.concatenate(
                [jnp.broadcast_to(cd_row[:, h:h + 1], (headdim, n_state))
                 for h in range(g * hpg, (g + 1) * hpg)], axis=0)
            snew_ref[b, g * hpg:(g + 1) * hpg] = (s0 * dec + upd).reshape(hpg, headdim, n_state)
        return carry

    lax.fori_loop(0, bt, body, 0)


def _ssd_sample_state(s0, c_b, b_b, xdd_b, cd, *, bt):
    nb, heads, headdim, n_state = s0.shape
    nt = c_b.shape[1]
    gn_w = c_b.shape[2]
    d_ssm = xdd_b.shape[2]
    return pl.pallas_call(
        functools.partial(_ssd_sample_state_kernel, bt=bt, groups=SSD_GROUPS, n_state=n_state,
                          headdim=headdim),
        grid=(nb // bt,),
        in_specs=[pl.BlockSpec((bt, heads, headdim, n_state), lambda i: (i, 0, 0, 0)),
                  pl.BlockSpec((bt, nt, gn_w), lambda i: (i, 0, 0)),
                  pl.BlockSpec((bt, nt, gn_w), lambda i: (i, 0, 0)),
                  pl.BlockSpec((bt, nt, d_ssm), lambda i: (i, 0, 0)),
                  pl.BlockSpec((bt, LANES), lambda i: (i, 0))],
        out_specs=[pl.BlockSpec((bt, heads, headdim, n_state), lambda i: (i, 0, 0, 0)),
                   pl.BlockSpec((bt, nt, d_ssm), lambda i: (i, 0, 0))],
        out_shape=[jax.ShapeDtypeStruct(s0.shape, F32),
                   jax.ShapeDtypeStruct((nb, nt, d_ssm), F32)],
        compiler_params=_params(1),
        name="ssd_sample_state",
    )(s0, c_b, b_b, xdd_b, cd)


def _ssd_sample_post_kernel(ydp_ref, yoff_ref, ecum_ref, z_ref, gn_ref, out_ref, *, groups):
    nt, bb, d_ssm = ydp_ref.shape
    y = (ydp_ref[...] + yoff_ref[...] * ecum_ref[...]).reshape(nt * bb, d_ssm)
    z = z_ref[...].reshape(nt * bb, d_ssm)
    out_ref[...] = _group_norm_out(y, z, gn_ref, groups).reshape(nt, bb, d_ssm)


def _ssd_sample_post(ydp, yoff, ecum, proj3, gn, d_ssm, d_xbc, *, bb):
    nt, nb, n_main = proj3.shape
    z_blk = (n_main - d_xbc - d_ssm) // d_ssm
    slab = pl.BlockSpec((nt, bb, d_ssm), lambda i: (0, i, 0))
    return pl.pallas_call(
        functools.partial(_ssd_sample_post_kernel, groups=SSD_GROUPS),
        grid=(nb // bb,),
        in_specs=[slab, slab, slab, pl.BlockSpec((nt, bb, d_ssm), lambda i: (0, i, z_blk)),
                  pl.BlockSpec((1, d_ssm), lambda i: (0, 0))],
        out_specs=slab,
        out_shape=jax.ShapeDtypeStruct((nt, nb, d_ssm), BF16),
        compiler_params=_params(1),
        name="ssd_sample_post",
    )(ydp, yoff, ecum, proj3, gn)


def _pad_lanes(v):
    return jnp.pad(v.astype(F32), (0, LANES - v.shape[0])).reshape(1, LANES)


def kernel(x_prompt, x_sample, state_rglru_h, state_rglru_conv, state_ssd_h, state_ssd_conv, state_ffn_conv, g_mix, w_in, rg_conv_w, rg_conv_b, rg_gate_a_w, rg_gate_a_b, rg_gate_i_w, rg_gate_i_b, rg_lambda, g_rg_out, ssd_conv_w, ssd_conv_b, ssd_dt_bias, ssd_A_log, ssd_D, g_ssd_norm, w_out, g_ffn, w_ffn_up, ffn_conv_w, ffn_conv_b, w_ffn_down, g_final):
    depth = g_mix.shape[0]
    pb, seq, d_model = x_prompt.shape
    sb, dec_seq, _ = x_sample.shape
    d_rnn = rg_lambda.shape[1]
    d_ssm = g_ssd_norm.shape[1]
    d_xbc = ssd_conv_w.shape[2]
    d_ff = ffn_conv_w.shape[2]
    heads = ssd_A_log.shape[1]
    headdim = d_ssm // heads
    n_state = state_ssd_h.shape[-1]
    n_main = 2 * d_rnn + d_ssm + d_xbc
    k_rg = rg_conv_w.shape[1]
    k_ssd = ssd_conv_w.shape[1]
    k_ffn = ffn_conv_w.shape[1]
    gn_w = SSD_GROUPS * n_state
    assert heads <= LANES and n_state == LANES and k_ffn == 3 and dec_seq >= max(k_rg, k_ssd) - 1

    hid = lax.broadcasted_iota(jnp.int32, (LANES, d_ssm), 0)
    cid = lax.broadcasted_iota(jnp.int32, (LANES, d_ssm), 1)
    e_sel = (hid == cid // headdim).astype(BF16)
    gid = lax.broadcasted_iota(jnp.int32, (gn_w, d_ssm), 0) // n_state
    cgid = lax.broadcasted_iota(jnp.int32, (gn_w, d_ssm), 1) // (d_ssm // SSD_GROUPS)
    g_sel = (gid == cgid).astype(BF16)

    yp = x_prompt.reshape(pb * seq, d_model)
    ys = jnp.transpose(x_sample, (1, 0, 2)).reshape(dec_seq * sb, d_model)
    p_new = [[], [], [], [], []]
    s_new = [[], [], [], [], []]
    row = lambda v: v.reshape(1, -1)
    swap = lambda a3: jnp.transpose(a3, (1, 0, 2))
    bb = 32

    for l in range(depth):
        wdt_b = jnp.pad(w_in[l][:, n_main:], ((0, 0), (0, LANES - heads))).astype(BF16)
        w_out_b = w_out[l].astype(BF16)
        wa_b = rg_gate_a_w[l].astype(BF16)
        wi_b = rg_gate_i_w[l].astype(BF16)
        rg_args = (rg_conv_w[l], row(rg_conv_b[l]), wa_b, row(rg_gate_a_b[l]), wi_b,
                   row(rg_gate_i_b[l]), row(rg_lambda[l]), row(g_rg_out[l]))
        dtb = _pad_lanes(ssd_dt_bias[l])
        alog = _pad_lanes(ssd_A_log[l])
        dexp = jnp.repeat(ssd_D[l].astype(F32), headdim).reshape(1, d_ssm)
        gn = row(g_ssd_norm[l])

        ms = ys.shape[0]
        proj_s, dt_s, w_in_b = _in_proj(ys, row(g_mix[l]), w_in[l], wdt_b, n_main, tm=ms, tn=TN_IN)

        proj_p, dt_p = _in_proj(yp, row(g_mix[l]), w_in_b, wdt_b, n_main, tm=TM, tn=TN_IN)
        rg_steps = pb * (seq // TT_RG)
        ssd_steps = pb * (seq // SSD_CHUNK)
        rb_ssd = d_model // (2 * rg_steps + ssd_steps)
        assert rb_ssd % 16 == 0 and (2 * rg_steps + ssd_steps) * rb_ssd == d_model
        rg_p, hl_p, w_up_ta = _rglru_prompt(proj_p, pb, seq, d_rnn, *rg_args, w_ffn_up[l], TN_UP,
                                            2 * rb_ssd, tt=TT_RG)
        ssd_p, hfin_p, w_up_tb = _ssd_prompt(proj_p, dt_p, pb, seq, d_ssm, d_xbc, n_state, headdim,
                                             ssd_conv_w[l], row(ssd_conv_b[l]), dtb, alog, dexp, gn, e_sel,
                                             w_ffn_up[l], TN_UP, rb_ssd, 2 * rb_ssd * rg_steps)
        x1_p, h2_p, ss_p = _out_proj(rg_p, ssd_p, w_out_b, yp, row(g_ffn[l]), tm=TM, tn=TN_OUT)
        act_p, gst_p, w_down_t = _ffn_up_prompt(h2_p, ss_p, w_up_ta, w_up_tb, ffn_conv_w[l],
                                                row(ffn_conv_b[l]), seq, w_ffn_down[l], TN_DOWN, tm=TM)
        proj_p3 = proj_p.reshape(pb, seq, n_main)
        p_new[0].append(hl_p[:, SUBLANES - 1, :])
        p_new[1].append(proj_p3[:, seq - (k_rg - 1):, :d_rnn])
        p_new[2].append(hfin_p.reshape(pb, heads, headdim, n_state))
        p_new[3].append(proj_p3[:, seq - (k_ssd - 1):, n_main - d_xbc:])
        tiles = seq // TM
        p_new[4].append(gst_p[tiles - 1::tiles, SUBLANES - (k_ffn - 1):, :])

        proj_s3 = proj_s.reshape(dec_seq, sb, n_main)
        rg_s, hl_s = _rglru_sample(proj_s3, swap(state_rglru_conv[l]), state_rglru_h[l], d_rnn,
                                   *rg_args, bb=bb)
        ydp, ecum, xdd, bc, cd = _ssd_sample_pre(
            proj_s3, swap(state_ssd_conv[l]), dt_s.reshape(dec_seq, sb, LANES), d_ssm, d_xbc, n_state,
            ssd_conv_w[l], row(ssd_conv_b[l]), dtb, alog, dexp, e_sel, g_sel, bb=bb)
        bc_b = swap(bc)
        snew, yoff_b = _ssd_sample_state(state_ssd_h[l], bc_b[:, :, gn_w:], bc_b[:, :, :gn_w],
                                         swap(xdd), cd, bt=8)
        ssd_s = _ssd_sample_post(ydp, swap(yoff_b), ecum, proj_s3, gn, d_ssm, d_xbc, bb=bb)
        x1_s, h2_s, ss_s = _out_proj(rg_s.reshape(ms, d_rnn), ssd_s.reshape(ms, d_ssm), w_out_b, ys,
                                     row(g_ffn[l]), tm=ms, tn=TN_OUT)
        act_s, gst_s = _ffn_up_sample(h2_s, ss_s, w_up_ta, w_up_tb, ffn_conv_w[l], row(ffn_conv_b[l]),
                                      swap(state_ffn_conv[l]).reshape(-1, d_ff), sb)
        s_new[0].append(hl_s)
        s_new[1].append(swap(proj_s3[dec_seq - (k_rg - 1):, :, :d_rnn]))
        s_new[2].append(snew)
        s_new[3].append(swap(proj_s3[dec_seq - (k_ssd - 1):, :, n_main - d_xbc:]))
        s_new[4].append(swap(gst_s.reshape(k_ffn - 1, sb, d_ff)))

        last = l == depth - 1
        yp = _ffn_down(act_p, w_down_t, x1_p, row(g_final), last, tm=TM, tk=TK_DOWN)
        ys = _ffn_down(act_s, w_down_t, x1_s, row(g_final), last, tm=ms, tk=TK_DOWN)

    y_prompt = yp.reshape(pb, seq, d_model)
    y_sample = jnp.transpose(ys.reshape(dec_seq, sb, d_model), (1, 0, 2))
    return (y_prompt, y_sample,
            jnp.stack(p_new[0]), jnp.stack(p_new[1]), jnp.stack(p_new[2]), jnp.stack(p_new[3]),
            jnp.stack(p_new[4]),
            jnp.stack(s_new[0]), jnp.stack(s_new[1]), jnp.stack(s_new[2]), jnp.stack(s_new[3]),
            jnp.stack(s_new[4]))
```

```python
import functools

import jax
import jax.numpy as jnp
from jax import lax
from jax.experimental import pallas as pl
from jax.experimental.pallas import tpu as pltpu

F32 = jnp.float32
BF16 = jnp.bfloat16

EPS = 1e-6
RG_C = 8.0
SSD_CHUNK = 128
SSD_GROUPS = 4
LANES = 128
SUBLANES = 8
VMEM_LIMIT = 56 * 2**20

TM = 1024
ROW_CHUNK = 256
TT_RG = 256
TN_IN = 512
TN_OUT = 512
TN_UP = 512
TN_DOWN = 512
TK_DOWN = 4096


def _params(n_axes):
    return pltpu.CompilerParams(dimension_semantics=("arbitrary",) * n_axes,
                                vmem_limit_bytes=VMEM_LIMIT)


def _dot(a, b):
    return jnp.dot(a, b, preferred_element_type=F32)


def _rms_rows(x, g):
    ms = jnp.mean(x * x, axis=-1, keepdims=True)
    return (x * lax.rsqrt(ms + EPS)) * g


def _split3(x):
    hi = x.astype(BF16)
    r = x - hi.astype(F32)
    mid = r.astype(BF16)
    lo = (r - mid.astype(F32)).astype(BF16)
    return hi, mid, lo


def _dot_sel_right(x, sel):
    hi, mid, lo = _split3(x)
    return (_dot(hi, sel) + _dot(mid, sel)) + _dot(lo, sel)


def _dot_sel_left(sel, x):
    hi, mid, lo = _split3(x)
    return (_dot(sel, hi) + _dot(sel, mid)) + _dot(sel, lo)


def _inproj_kernel(x_hbm, g_ref, w_ref, wdt_ref, o_ref, dt_ref, x_buf, hn_ref, sem, *, ni, tm):
    i = pl.program_id(0)
    j = pl.program_id(1)

    def x_in(tile):
        return pltpu.make_async_copy(x_hbm.at[pl.ds(tile * tm, tm)], x_buf, sem)

    @pl.when(j == 0)
    def _():
        @pl.when(i == 0)
        def _():
            x_in(0).start()

        x_in(i).wait()
        for r in range(0, tm, ROW_CHUNK):
            hn_ref[r:r + ROW_CHUNK, :] = _rms_rows(x_buf[r:r + ROW_CHUNK, :], g_ref[...]).astype(BF16)
        dt_ref[...] = _dot(hn_ref[...], wdt_ref[...])

        @pl.when(i + 1 < ni)
        def _():
            x_in(i + 1).start()

    o_ref[...] = _dot(hn_ref[...], w_ref[...])


def _cast_col_tiles(src_ref, dst_ref):
    n_tiles, _, tn = dst_ref.shape
    for t in range(n_tiles):
        dst_ref[t] = src_ref[:, t * tn:(t + 1) * tn].astype(BF16)


def _in_proj(x2d, g, w_b, wdt_b, n_main, *, tm, tn):
    m, d = x2d.shape
    ni = m // tm
    return pl.pallas_call(
        functools.partial(_inproj_kernel, ni=ni, tm=tm),
        grid=(ni, n_main // tn),
        in_specs=[pl.BlockSpec(memory_space=pl.ANY),
                  pl.BlockSpec((1, d), lambda i, j: (0, 0)),
                  pl.BlockSpec((d, tn), lambda i, j: (0, j)),
                  pl.BlockSpec((d, LANES), lambda i, j: (0, 0))],
        out_specs=[pl.BlockSpec((tm, tn), lambda i, j: (i, j)),
                   pl.BlockSpec((tm, LANES), lambda i, j: (i, 0))],
        out_shape=[jax.ShapeDtypeStruct((m, n_main), F32), jax.ShapeDtypeStruct((m, LANES), F32)],
        scratch_shapes=[pltpu.VMEM((tm, d), F32), pltpu.VMEM((tm, d), BF16), pltpu.SemaphoreType.DMA(())],
        compiler_params=_params(2),
        name="in_proj",
    )(x2d, g, w_b, wdt_b)


def _outproj_kernel(rg_ref, ssd_ref, wt_ref, wb_ref, x_ref, g_ref, x1_ref, xg_ref, ss_ref):
    j = pl.program_id(1)
    x1 = x_ref[...] + (_dot(rg_ref[...], wt_ref[...]) + _dot(ssd_ref[...], wb_ref[...]))
    x1_ref[...] = x1
    xg_ref[...] = (x1 * g_ref[...]).astype(BF16)
    part = jnp.broadcast_to(jnp.sum(x1 * x1, axis=-1, keepdims=True), ss_ref.shape)

    @pl.when(j == 0)
    def _():
        ss_ref[...] = part

    @pl.when(j > 0)
    def _():
        ss_ref[...] += part


def _out_proj(rg, ssd, w_b, x2d, g, *, tm, tn):
    m, d = x2d.shape
    k_half = rg.shape[1]
    return pl.pallas_call(
        _outproj_kernel,
        grid=(m // tm, d // tn),
        in_specs=[pl.BlockSpec((tm, k_half), lambda i, j: (i, 0)),
                  pl.BlockSpec((tm, k_half), lambda i, j: (i, 0)),
                  pl.BlockSpec((k_half, tn), lambda i, j: (0, j)),
                  pl.BlockSpec((k_half, tn), lambda i, j: (1, j)),
                  pl.BlockSpec((tm, tn), lambda i, j: (i, j)),
                  pl.BlockSpec((1, tn), lambda i, j: (0, j))],
        out_specs=[pl.BlockSpec((tm, tn), lambda i, j: (i, j)),
                   pl.BlockSpec((tm, tn), lambda i, j: (i, j)),
                   pl.BlockSpec((tm, LANES), lambda i, j: (i, 0))],
        out_shape=[jax.ShapeDtypeStruct((m, d), F32),
                   jax.ShapeDtypeStruct((m, d), BF16),
                   jax.ShapeDtypeStruct((m, LANES), F32)],
        compiler_params=_params(2),
        name="out_proj",
    )(rg, ssd, w_b, w_b, x2d, g)


def _glu(gate_c, val):
    return (jax.nn.gelu(gate_c) * val).astype(BF16)


def _dot_split_k(h_ref, wa_ref, wb_ref):
    ka = wa_ref.shape[0]
    return _dot(h_ref[:, :ka], wa_ref[...]) + _dot(h_ref[:, ka:], wb_ref[...])


def _inv_rms(ss_ref, width):
    return lax.rsqrt(ss_ref[:, 0:1] / width + EPS)


def _ffn_up_prompt_kernel(xg_ref, ss_ref, wga_ref, wgb_ref, wva_ref, wvb_ref, cw_ref, cb_ref, wsrc_ref,
                          act_ref, gst_ref, wdst_ref, carry_ref, *, tiles_per_seq):
    i = pl.program_id(0)
    j = pl.program_id(1)
    _cast_col_tiles(wsrc_ref, wdst_ref)
    inv = _inv_rms(ss_ref, xg_ref.shape[1])
    gate = _dot_split_k(xg_ref, wga_ref, wgb_ref) * inv
    val = _dot_split_k(xg_ref, wva_ref, wvb_ref) * inv
    tm, tn = gate.shape
    w0, w1, w2 = cw_ref[0:1, :], cw_ref[1:2, :], cw_ref[2:3, :]
    cb = cb_ref[...]
    g1 = pltpu.roll(gate, 1, 0)
    g2 = pltpu.roll(gate, 2, 0)
    act_ref[...] = _glu(((cb + g2 * w0) + g1 * w1) + gate * w2, val)
    @pl.when((i % tiles_per_seq) == 0)
    def _():
        carry_ref[j] = jnp.zeros((SUBLANES, tn), F32)

    prev = carry_ref[j]
    rows = lax.broadcasted_iota(jnp.int32, (SUBLANES, tn), 0)
    t1 = jnp.where(rows < 1, pltpu.roll(prev, 1, 0), g1[0:SUBLANES])
    t2 = jnp.where(rows < 2, pltpu.roll(prev, 2, 0), g2[0:SUBLANES])
    act_ref[0:SUBLANES, :] = _glu(((cb + t2 * w0) + t1 * w1) + gate[0:SUBLANES] * w2,
                                  val[0:SUBLANES])
    last = gate[tm - SUBLANES:tm]
    carry_ref[j] = last
    gst_ref[...] = last


def _ffn_up_prompt(h2, ss, w_ta, w_tb, cw, cb, seq, w_next, tn_next, *, tm):
    m, d = h2.shape
    nj, ka, tn = w_ta.shape[0] // 2, w_ta.shape[1], w_ta.shape[2]
    kb = w_tb.shape[1]
    assert ka + kb == d and ka % LANES == 0
    d_ff = nj * tn
    steps = (m // tm) * nj
    k_next, n_next = w_next.shape
    rb = k_next // steps
    assert rb * steps == k_next and rb % 16 == 0 and n_next % tn_next == 0
    return pl.pallas_call(
        functools.partial(_ffn_up_prompt_kernel, tiles_per_seq=seq // tm),
        grid=(m // tm, nj),
        in_specs=[pl.BlockSpec((tm, d), lambda i, j: (i, 0)),
                  pl.BlockSpec((tm, LANES), lambda i, j: (i, 0)),
                  pl.BlockSpec((None, ka, tn), lambda i, j: (j, 0, 0)),
                  pl.BlockSpec((None, kb, tn), lambda i, j: (j, 0, 0)),
                  pl.BlockSpec((None, ka, tn), lambda i, j: (j + nj, 0, 0)),
                  pl.BlockSpec((None, kb, tn), lambda i, j: (j + nj, 0, 0)),
                  pl.BlockSpec((3, tn), lambda i, j: (0, j)),
                  pl.BlockSpec((1, tn), lambda i, j: (0, j)),
                  pl.BlockSpec((rb, n_next), lambda i, j: (i * nj + j, 0))],
        out_specs=[pl.BlockSpec((tm, tn), lambda i, j: (i, j)),
                   pl.BlockSpec((None, SUBLANES, tn), lambda i, j: (i, 0, j)),
                   pl.BlockSpec((n_next // tn_next, rb, tn_next), lambda i, j: (0, i * nj + j, 0))],
        out_shape=[jax.ShapeDtypeStruct((m, d_ff), BF16),
                   jax.ShapeDtypeStruct((m // tm, SUBLANES, d_ff), F32),
                   jax.ShapeDtypeStruct((n_next // tn_next, k_next, tn_next), BF16)],
        scratch_shapes=[pltpu.VMEM((nj, SUBLANES, tn), F32)],
        compiler_params=_params(2),
        name="ffn_up_prompt",
    )(h2, ss, w_ta, w_tb, w_ta, w_tb, cw, cb, w_next)


def _ffn_up_sample_kernel(xg_ref, ss_ref, wga_ref, wgb_ref, wva_ref, wvb_ref, cw_ref, cb_ref, st_ref,
                          act_ref, nst_ref, *, nb, nt):
    inv = _inv_rms(ss_ref, xg_ref.shape[1])
    gate = _dot_split_k(xg_ref, wga_ref, wgb_ref) * inv
    val = _dot_split_k(xg_ref, wva_ref, wvb_ref) * inv
    w0, w1, w2 = cw_ref[0:1, :], cw_ref[1:2, :], cw_ref[2:3, :]
    cb = cb_ref[...]
    ext = [st_ref[0:nb, :], st_ref[nb:2 * nb, :]] + [gate[t * nb:(t + 1) * nb] for t in range(nt)]
    for t in range(nt):
        gc = ((cb + ext[t] * w0) + ext[t + 1] * w1) + ext[t + 2] * w2
        act_ref[t * nb:(t + 1) * nb, :] = _glu(gc, val[t * nb:(t + 1) * nb])
    nst_ref[0:nb, :] = ext[nt]
    nst_ref[nb:2 * nb, :] = ext[nt + 1]


def _ffn_up_sample(h2, ss, w_ta, w_tb, cw, cb, st, nb):
    m, d = h2.shape
    nj, ka, tn = w_ta.shape[0] // 2, w_ta.shape[1], w_ta.shape[2]
    kb = w_tb.shape[1]
    d_ff = nj * tn
    return pl.pallas_call(
        functools.partial(_ffn_up_sample_kernel, nb=nb, nt=m // nb),
        grid=(nj,),
        in_specs=[pl.BlockSpec((m, d), lambda j: (0, 0)),
                  pl.BlockSpec((m, LANES), lambda j: (0, 0)),
                  pl.BlockSpec((None, ka, tn), lambda j: (j, 0, 0)),
                  pl.BlockSpec((None, kb, tn), lambda j: (j, 0, 0)),
                  pl.BlockSpec((None, ka, tn), lambda j: (j + nj, 0, 0)),
                  pl.BlockSpec((None, kb, tn), lambda j: (j + nj, 0, 0)),
                  pl.BlockSpec((3, tn), lambda j: (0, j)),
                  pl.BlockSpec((1, tn), lambda j: (0, j)),
                  pl.BlockSpec((2 * nb, tn), lambda j: (0, j))],
        out_specs=[pl.BlockSpec((m, tn), lambda j: (0, j)),
                   pl.BlockSpec((2 * nb, tn), lambda j: (0, j))],
        out_shape=[jax.ShapeDtypeStruct((m, d_ff), BF16),
                   jax.ShapeDtypeStruct((2 * nb, d_ff), F32)],
        compiler_params=_params(1),
        name="ffn_up_sample",
    )(h2, ss, w_ta, w_tb, w_ta, w_tb, cw, cb, st)


def _ffn_down_kernel(act_ref, w_ref, x1_ref, g_ref, y_hbm, rows_ref, sem,
                     *, ni, nk, nj, tm, tn, final_norm):
    i = pl.program_id(0)
    k = pl.program_id(1)
    j = pl.program_id(2)
    cols = pl.ds(pl.multiple_of(j * tn, tn), tn)

    def rows_out(tile):
        return pltpu.make_async_copy(rows_ref, y_hbm.at[pl.ds(tile * tm, tm)], sem)

    @pl.when(k == 0)
    def _():
        @pl.when((j == 0) & (i > 0))
        def _():
            rows_out(i - 1).wait()

        rows_ref[:, cols] = x1_ref[...]

    rows_ref[:, cols] += _dot(act_ref[...], w_ref[...])

    @pl.when((k == nk - 1) & (j == nj - 1))
    def _():
        if final_norm:
            for r in range(0, tm, ROW_CHUNK):
                rows_ref[r:r + ROW_CHUNK, :] = _rms_rows(rows_ref[r:r + ROW_CHUNK, :], g_ref[...])
        rows_out(i).start()

        @pl.when(i == ni - 1)
        def _():
            rows_out(i).wait()


def _ffn_down(act, w_t, x1, g, final_norm, *, tm, tk):
    m, d = x1.shape
    nj, d_ff, tn = w_t.shape
    ni, nk = m // tm, d_ff // tk
    return pl.pallas_call(
        functools.partial(_ffn_down_kernel, ni=ni, nk=nk, nj=nj, tm=tm, tn=tn, final_norm=final_norm),
        grid=(ni, nk, nj),
        in_specs=[pl.BlockSpec((tm, tk), lambda i, k, j: (i, k)),
                  pl.BlockSpec((None, tk, tn), lambda i, k, j: (j, k, 0)),
                  pl.BlockSpec((tm, tn), lambda i, k, j: (i, jnp.where(k == 0, j, nj - 1))),
                  pl.BlockSpec((1, d), lambda i, k, j: (0, 0))],
        out_specs=pl.BlockSpec(memory_space=pl.ANY),
        out_shape=jax.ShapeDtypeStruct((m, d), F32),
        scratch_shapes=[pltpu.VMEM((tm, d), F32), pltpu.SemaphoreType.DMA(())],
        compiler_params=_params(3),
        name="ffn_down",
    )(act, w_t, x1, g)


def _softplus(x):
    return jnp.maximum(x, 0.0) + jnp.log1p(jnp.exp(-jnp.abs(x)))


def _rglru_coeffs(xc, wa_ref, ba_ref, wi_ref, bi_ref, lam_ref):
    xb = xc.astype(BF16)
    heads, blk = wa_ref.shape[0], wa_ref.shape[1]
    ga = jnp.concatenate([_dot(xb[:, h * blk:(h + 1) * blk], wa_ref[h]) for h in range(heads)], axis=1)
    gi = jnp.concatenate([_dot(xb[:, h * blk:(h + 1) * blk], wi_ref[h]) for h in range(heads)], axis=1)
    gate_r = jax.nn.sigmoid(ga + ba_ref[...])
    gate_i = jax.nn.sigmoid(gi + bi_ref[...])
    log_a = (-RG_C * gate_r) * _softplus(-lam_ref[...])
    a = jnp.exp(log_a)
    bx = jnp.sqrt(-jnp.tanh(log_a) * (a * a + 1.0)) * (gate_i * xc)
    return a, bx


def _rg_out(h, gate, g_ref):
    return _rms_rows(h * jax.nn.gelu(gate), g_ref[...]).astype(BF16)


def _conv_tile(x, prev, w_ref, b_ref, k_taps):
    t_len, width = x.shape
    groups = t_len // SUBLANES
    x3 = x.reshape(groups, SUBLANES, width)
    pos = lax.broadcasted_iota(jnp.int32, x3.shape, 1)
    out = b_ref[...].reshape(1, 1, width)
    for k in range(k_taps):
        s = k_taps - 1 - k
        wk = w_ref[k:k + 1, :].reshape(1, 1, width)
        if s == 0:
            shifted = x3
        else:
            rot = pltpu.roll(x3, s, 1)
            rot_prev = jnp.concatenate([pltpu.roll(prev, s, 0)[None], rot[:groups - 1]], axis=0)
            shifted = jnp.where(pos >= s, rot, rot_prev)
        out = out + shifted * wk
    return out.reshape(t_len, width)


def _rglru_prompt_kernel(x_ref, gate_ref, cw_ref, cb_ref, wa_ref, ba_ref, wi_ref, bi_ref, lam_ref,
                         g_ref, wsrc_ref, out_ref, hlast_ref, wdst_ref, xprev_ref, hc_ref, a_s, b_s, h_s,
                         *, k_taps):
    c = pl.program_id(1)
    _cast_col_tiles(wsrc_ref, wdst_ref)

    @pl.when(c == 0)
    def _():
        xprev_ref[...] = jnp.zeros_like(xprev_ref)
        hc_ref[...] = jnp.zeros_like(hc_ref)

    x = x_ref[...]
    t_len, width = x.shape
    xc = _conv_tile(x, xprev_ref[...], cw_ref, cb_ref, k_taps)
    xprev_ref[...] = x[t_len - SUBLANES:t_len]
    a, b = _rglru_coeffs(xc, wa_ref, ba_ref, wi_ref, bi_ref, lam_ref)
    groups = t_len // SUBLANES
    a = a.reshape(groups, SUBLANES, width)
    b = b.reshape(groups, SUBLANES, width)
    pos = lax.broadcasted_iota(jnp.int32, a.shape, 1)
    for s in (1, 2, 4):
        a_sh = pltpu.roll(a, s, 1)
        b_sh = pltpu.roll(b, s, 1)
        m = pos >= s
        b = jnp.where(m, a * b_sh + b, b)
        a = jnp.where(m, a * a_sh, a)
    a_s[...] = a.reshape(t_len, width)
    b_s[...] = b.reshape(t_len, width)
    h = hc_ref[...]
    for grp in range(t_len // SUBLANES):
        lo, hi = grp * SUBLANES, (grp + 1) * SUBLANES
        hl = jnp.broadcast_to(h[SUBLANES - 1:SUBLANES, :], (SUBLANES, width))
        h = a_s[lo:hi, :] * hl + b_s[lo:hi, :]
        h_s[lo:hi, :] = h
    hc_ref[...] = h
    hlast_ref[...] = h
    out_ref[...] = _rg_out(h_s[...], gate_ref[...], g_ref)


def _rglru_prompt(proj, nb, seq, d_rnn, cw, cb, wa, ba, wi, bi, lam, g, w_next, tn_next, rb, *, tt):
    nc = seq // tt
    heads, blk = wa.shape[0], wa.shape[1]
    k_next, n_next = w_next.shape
    vec = pl.BlockSpec((1, d_rnn), lambda b, c: (0, 0))
    gate_w = pl.BlockSpec((heads, blk, blk), lambda b, c: (0, 0, 0))
    return pl.pallas_call(
        functools.partial(_rglru_prompt_kernel, k_taps=cw.shape[0]),
        grid=(nb, nc),
        in_specs=[pl.BlockSpec((tt, d_rnn), lambda b, c: (b * nc + c, 0)),
                  pl.BlockSpec((tt, d_rnn), lambda b, c: (b * nc + c, 1)),
                  pl.BlockSpec((cw.shape[0], d_rnn), lambda b, c: (0, 0)),
                  vec, gate_w, vec, gate_w, vec, vec, vec,
                  pl.BlockSpec((rb, n_next), lambda b, c: (b * nc + c, 0))],
        out_specs=[pl.BlockSpec((tt, d_rnn), lambda b, c: (b * nc + c, 0)),
                   pl.BlockSpec((None, SUBLANES, d_rnn), lambda b, c: (b, 0, 0)),
                   pl.BlockSpec((n_next // tn_next, rb, tn_next), lambda b, c: (0, b * nc + c, 0))],
        out_shape=[jax.ShapeDtypeStruct((nb * seq, d_rnn), BF16),
                   jax.ShapeDtypeStruct((nb, SUBLANES, d_rnn), F32),
                   jax.ShapeDtypeStruct((n_next // tn_next, nb * nc * rb, tn_next), BF16)],
        scratch_shapes=[pltpu.VMEM((SUBLANES, d_rnn), F32), pltpu.VMEM((SUBLANES, d_rnn), F32),
                        pltpu.VMEM((tt, d_rnn), F32), pltpu.VMEM((tt, d_rnn), F32),
                        pltpu.VMEM((tt, d_rnn), F32)],
        compiler_params=_params(2),
        name="rglru_prompt",
    )(proj, proj, cw, cb, wa, ba, wi, bi, lam, g, w_next)


def _conv_slabs(st_ref, x_ref, w_ref, b_ref):
    k_taps = w_ref.shape[0]
    ext = [st_ref[s] for s in range(k_taps - 1)] + [x_ref[t] for t in range(x_ref.shape[0])]
    convs = []
    for t in range(x_ref.shape[0]):
        acc = b_ref[...]
        for k in range(k_taps):
            acc = acc + ext[t + k] * w_ref[k:k + 1, :]
        convs.append(acc)
    return jnp.concatenate(convs, axis=0)


def _rglru_sample_kernel(x_ref, gate_ref, st_ref, h0_ref, cw_ref, cb_ref, wa_ref, ba_ref, wi_ref,
                         bi_ref, lam_ref, g_ref, out_ref, hlast_ref, h_s):
    nt, bb, width = x_ref.shape
    xc = _conv_slabs(st_ref, x_ref, cw_ref, cb_ref)
    a, b = _rglru_coeffs(xc, wa_ref, ba_ref, wi_ref, bi_ref, lam_ref)
    h = h0_ref[...]
    for t in range(nt):
        h = a[t * bb:(t + 1) * bb] * h + b[t * bb:(t + 1) * bb]
        h_s[t * bb:(t + 1) * bb, :] = h
    hlast_ref[...] = h
    gate = gate_ref[...].reshape(nt * bb, width)
    out_ref[...] = _rg_out(h_s[...], gate, g_ref).reshape(nt, bb, width)


def _rglru_sample(proj3, st3, h0, d_rnn, cw, cb, wa, ba, wi, bi, lam, g, *, bb):
    nt, nb, _ = proj3.shape
    heads, blk = wa.shape[0], wa.shape[1]
    k_taps = cw.shape[0]
    vec = pl.BlockSpec((1, d_rnn), lambda i: (0, 0))
    gate_w = pl.BlockSpec((heads, blk, blk), lambda i: (0, 0, 0))
    return pl.pallas_call(
        _rglru_sample_kernel,
        grid=(nb // bb,),
        in_specs=[pl.BlockSpec((nt, bb, d_rnn), lambda i: (0, i, 0)),
                  pl.BlockSpec((nt, bb, d_rnn), lambda i: (0, i, 1)),
                  pl.BlockSpec((k_taps - 1, bb, d_rnn), lambda i: (0, i, 0)),
                  pl.BlockSpec((bb, d_rnn), lambda i: (i, 0)),
                  pl.BlockSpec((k_taps, d_rnn), lambda i: (0, 0)),
                  vec, gate_w, vec, gate_w, vec, vec, vec],
        out_specs=[pl.BlockSpec((nt, bb, d_rnn), lambda i: (0, i, 0)),
                   pl.BlockSpec((bb, d_rnn), lambda i: (i, 0))],
        out_shape=[jax.ShapeDtypeStruct((nt, nb, d_rnn), BF16),
                   jax.ShapeDtypeStruct((nb, d_rnn), F32)],
        scratch_shapes=[pltpu.VMEM((nt * bb, d_rnn), F32)],
        compiler_params=_params(1),
        name="rglru_sample",
    )(proj3, proj3, st3, h0, cw, cb, wa, ba, wi, bi, lam, g)


def _silu(x):
    return x * jax.nn.sigmoid(x)


def _group_norm_out(y, z, gn_ref, groups):
    u = y * _silu(z)
    gw = u.shape[1] // groups
    outs = []
    for g in range(groups):
        ug = u[:, g * gw:(g + 1) * gw]
        ms = jnp.mean(ug * ug, axis=-1, keepdims=True)
        outs.append((ug * lax.rsqrt(ms + EPS)) * gn_ref[:, g * gw:(g + 1) * gw])
    return jnp.concatenate(outs, axis=1).astype(BF16)


def _ssd_prompt_kernel(z_ref, xbc_ref, dt_ref, cw_ref, cb_ref, dtb_ref, alog_ref, dexp_ref, gn_ref,
                       e_ref, wsrc_ref, wsrc2_ref, out_ref, hfin_ref, wdst_ref, wdst2_ref,
                       carry_ref, s_ref, y_s, *, k_taps, d_ssm, n_state, groups, headdim, nc):
    c = pl.program_id(1)
    _cast_col_tiles(wsrc_ref, wdst_ref)
    wdst2_ref[...] = wsrc2_ref[...].astype(BF16)

    @pl.when(c == 0)
    def _():
        carry_ref[...] = jnp.zeros_like(carry_ref)
        s_ref[...] = jnp.zeros_like(s_ref)

    xbc = xbc_ref[...]
    q = xbc.shape[0]
    xc = _silu(_conv_tile(xbc, carry_ref[...], cw_ref, cb_ref, k_taps))
    carry_ref[...] = xbc[q - SUBLANES:q]
    gn_w = groups * n_state
    xs = xc[:, :d_ssm]
    bm = xc[:, d_ssm:d_ssm + gn_w]
    cm = xc[:, d_ssm + gn_w:d_ssm + 2 * gn_w]

    dt = _softplus(dt_ref[...] + dtb_ref[...])
    dta = dt * (-jnp.exp(alog_ref[...]))
    row = lax.broadcasted_iota(jnp.int32, (q, q), 0)
    col = lax.broadcasted_iota(jnp.int32, (q, q), 1)
    causal = row >= col
    tril = jnp.where(causal, 1.0, 0.0).astype(BF16)
    cum = _dot_sel_left(tril, dta)
    cum_t = cum.T
    e = e_ref[...]
    dt_e = _dot_sel_right(dt, e)
    cum_e = _dot_sel_right(cum, e)
    ecum_e = jnp.exp(cum_e)
    xdt = xs * dt_e
    cum_last = cum_e[q - 1:q, :]
    xdd = (xdt * jnp.exp(cum_last - cum_e)).astype(BF16)
    chunk_decay = jnp.exp(cum_last)

    heads_per_group = d_ssm // headdim // groups
    pair_w = 2 * headdim
    gw = heads_per_group * headdim
    lane = lax.broadcasted_iota(jnp.int32, (q, pair_w), 1)
    for g in range(groups):
        cg = cm[:, g * n_state:(g + 1) * n_state].astype(BF16)
        bg32 = bm[:, g * n_state:(g + 1) * n_state]
        bg = bg32.astype(BF16)
        cb_g = lax.dot_general(cg, bg, (((1,), (1,)), ((), ())), preferred_element_type=F32)
        s_old = s_ref[:, g * gw:(g + 1) * gw]
        y_s[:, g * gw:(g + 1) * gw] = _dot(cg, s_old.astype(BF16)) * ecum_e[:, g * gw:(g + 1) * gw]
        st_g = _dot(bg32.T.astype(BF16), xdd[:, g * gw:(g + 1) * gw])
        s_ref[:, g * gw:(g + 1) * gw] = s_old * chunk_decay[:, g * gw:(g + 1) * gw] + st_g
        for pr in range(heads_per_group // 2):
            h0 = g * heads_per_group + 2 * pr
            ms = []
            for h in (h0, h0 + 1):
                seg = (jnp.broadcast_to(cum[:, h:h + 1], (q, q))
                       - jnp.broadcast_to(cum_t[h:h + 1, :], (q, q)))
                ms.append(cb_g * jnp.exp(jnp.where(causal, seg, -jnp.inf)))
            lo_c = h0 * headdim
            xp = xdt[:, lo_c:lo_c + pair_w]
            lhs = jnp.concatenate(ms, axis=1).astype(BF16)
            rhs = jnp.concatenate([jnp.where(lane < headdim, xp, 0.0),
                                   jnp.where(lane >= headdim, xp, 0.0)], axis=0).astype(BF16)
            y_s[:, lo_c:lo_c + pair_w] += _dot(lhs, rhs)

    y = y_s[...] + dexp_ref[...] * xs
    out_ref[...] = _group_norm_out(y, z_ref[...], gn_ref, groups)

    @pl.when(c == nc - 1)
    def _():
        for blk in range(d_ssm // LANES):
            hfin_ref[blk * LANES:(blk + 1) * LANES, :] = s_ref[:, blk * LANES:(blk + 1) * LANES].T


def _ssd_prompt(proj, dt_raw, nb, seq, d_ssm, d_xbc, n_state, headdim, cw, cb, dtb, alog, dexp, gn, e,
                w_next, tn_next, rb, row0, w_plain):
    q = SSD_CHUNK
    nc = seq // q
    k_plain, n_plain = w_plain.shape
    rb2 = k_plain // (nb * nc)
    assert rb2 * nb * nc == k_plain and rb2 % 16 == 0
    z_blk = (proj.shape[1] - d_xbc - d_ssm) // d_ssm
    x_blk = (proj.shape[1] - d_xbc) // d_xbc
    assert z_blk * d_ssm + d_ssm + d_xbc == proj.shape[1] and x_blk * d_xbc + d_xbc == proj.shape[1]
    k_next, n_next = w_next.shape
    n_tiles = n_next // tn_next
    blk0 = row0 // rb
    assert blk0 * rb == row0 and row0 + nb * nc * rb == k_next
    one = lambda w: pl.BlockSpec((1, w), lambda b, c: (0, 0))
    return pl.pallas_call(
        functools.partial(_ssd_prompt_kernel, k_taps=cw.shape[0], d_ssm=d_ssm, n_state=n_state,
                          groups=SSD_GROUPS, headdim=headdim, nc=nc),
        grid=(nb, nc),
        in_specs=[pl.BlockSpec((q, d_ssm), lambda b, c: (b * nc + c, z_blk)),
                  pl.BlockSpec((q, d_xbc), lambda b, c: (b * nc + c, x_blk)),
                  pl.BlockSpec((q, LANES), lambda b, c: (b * nc + c, 0)),
                  pl.BlockSpec((cw.shape[0], d_xbc), lambda b, c: (0, 0)),
                  one(d_xbc), one(LANES), one(LANES), one(d_ssm), one(d_ssm),
                  pl.BlockSpec((LANES, d_ssm), lambda b, c: (0, 0)),
                  pl.BlockSpec((rb, n_next), lambda b, c: (blk0 + b * nc + c, 0)),
                  pl.BlockSpec((rb2, n_plain), lambda b, c: (b * nc + c, 0))],
        out_specs=[pl.BlockSpec((q, d_ssm), lambda b, c: (b * nc + c, 0)),
                   pl.BlockSpec((None, d_ssm, n_state), lambda b, c: (b, 0, 0)),
                   pl.BlockSpec((n_tiles, rb, tn_next), lambda b, c: (0, b * nc + c, 0)),
                   pl.BlockSpec((rb2, n_plain), lambda b, c: (b * nc + c, 0))],
        out_shape=[jax.ShapeDtypeStruct((nb * seq, d_ssm), BF16),
                   jax.ShapeDtypeStruct((nb, d_ssm, n_state), F32),
                   jax.ShapeDtypeStruct((n_tiles, k_next - row0, tn_next), BF16),
                   jax.ShapeDtypeStruct((k_plain, n_plain), BF16)],
        scratch_shapes=[pltpu.VMEM((SUBLANES, d_xbc), F32), pltpu.VMEM((n_state, d_ssm), F32),
                        pltpu.VMEM((q, d_ssm), F32)],
        compiler_params=_params(2),
        name="ssd_prompt",
    )(proj, proj, dt_raw, cw, cb, dtb, alog, dexp, gn, e, w_next, w_plain)


def _ssd_sample_pre_kernel(xbc_ref, st_ref, dt_ref, cw_ref, cb_ref, dtb_ref, alog_ref, dexp_ref,
                           e_ref, gsel_ref, ydp_ref, ecum_ref, xdd_ref, bc_ref, cd_ref,
                           *, d_ssm, n_state, groups):
    nt, bb, _ = xbc_ref.shape
    xc = _silu(_conv_slabs(st_ref, xbc_ref, cw_ref, cb_ref))
    gn_w = groups * n_state
    xs = xc[:, :d_ssm]
    bm = xc[:, d_ssm:d_ssm + gn_w]
    cm = xc[:, d_ssm + gn_w:d_ssm + 2 * gn_w]
    bc_ref[...] = xc[:, d_ssm:d_ssm + 2 * gn_w].reshape(nt, bb, 2 * gn_w)

    dt = _softplus(dt_ref[...].reshape(nt * bb, LANES) + dtb_ref[...])
    dta = dt * (-jnp.exp(alog_ref[...]))
    cums = [dta[0:bb]]
    for t in range(1, nt):
        cums.append(cums[-1] + dta[t * bb:(t + 1) * bb])
    cum = jnp.concatenate(cums, axis=0)
    cd_ref[...] = jnp.exp(cums[-1])
    e = e_ref[...]
    dt_e = _dot_sel_right(dt, e)
    cum_e = _dot_sel_right(cum, e)
    ecum_ref[...] = jnp.exp(cum_e).reshape(nt, bb, d_ssm)
    xdt = xs * dt_e
    sl = lambda v, t: v[t * bb:(t + 1) * bb]
    cum_last = sl(cum_e, nt - 1)
    gsel = gsel_ref[...]
    for t in range(nt):
        xdd_ref[t] = sl(xdt, t) * jnp.exp(cum_last - sl(cum_e, t))
        acc = dexp_ref[...] * sl(xs, t)
        for s in range(t + 1):
            cb_e = _dot_sel_right(sl(cm, t) * sl(bm, s), gsel)
            acc = acc + (cb_e * jnp.exp(sl(cum_e, t) - sl(cum_e, s))) * sl(xdt, s)
        ydp_ref[t] = acc


def _ssd_sample_pre(proj3, st3, dt3, d_ssm, d_xbc, n_state, cw, cb, dtb, alog, dexp, e, gsel, *, bb):
    nt, nb, n_main = proj3.shape
    k_taps = cw.shape[0]
    x_blk = (n_main - d_xbc) // d_xbc
    gn_w = SSD_GROUPS * n_state
    full = lambda r, w: pl.BlockSpec((r, w), lambda i: (0, 0))
    slab = lambda n, w: pl.BlockSpec((n, bb, w), lambda i: (0, i, 0))
    f32 = lambda *s: jax.ShapeDtypeStruct(s, F32)
    return pl.pallas_call(
        functools.partial(_ssd_sample_pre_kernel, d_ssm=d_ssm, n_state=n_state, groups=SSD_GROUPS),
        grid=(nb // bb,),
        in_specs=[pl.BlockSpec((nt, bb, d_xbc), lambda i: (0, i, x_blk)),
                  slab(k_taps - 1, d_xbc), slab(nt, LANES), full(k_taps, d_xbc),
                  full(1, d_xbc), full(1, LANES), full(1, LANES), full(1, d_ssm),
                  full(LANES, d_ssm), full(gn_w, d_ssm)],
        out_specs=[slab(nt, d_ssm), slab(nt, d_ssm), slab(nt, d_ssm), slab(nt, 2 * gn_w),
                   pl.BlockSpec((bb, LANES), lambda i: (i, 0))],
        out_shape=[f32(nt, nb, d_ssm), f32(nt, nb, d_ssm), f32(nt, nb, d_ssm),
                   f32(nt, nb, 2 * gn_w), f32(nb, LANES)],
        compiler_params=_params(1),
        name="ssd_sample_pre",
    )(proj3, st3, dt3, cw, cb, dtb, alog, dexp, e, gsel)


def _ssd_sample_state_kernel(s0_ref, c_ref, b_ref, xdd_ref, cd_ref, snew_ref, yoff_ref,
                             *, bt, groups, n_state, headdim):
    heads = s0_ref.shape[1]
    hpg = heads // groups
    gw = hpg * headdim

    def body(b, carry):
        cd_row = cd_ref[pl.ds(b, 1), :]
        cb = c_ref[b].astype(BF16)
        bb = b_ref[b].astype(BF16)
        xb = xdd_ref[b].astype(BF16)
        for g in range(groups):
            s0 = s0_ref[b, g * hpg:(g + 1) * hpg].reshape(gw, n_state)
            yoff = lax.dot_general(cb[:, g * n_state:(g + 1) * n_state], s0.astype(BF16),
                                   (((1,), (1,)), ((), ())), preferred_element_type=F32)
            yoff_ref[b, :, g * gw:(g + 1) * gw] = yoff
            upd = lax.dot_general(xb[:, g * gw:(g + 1) * gw], bb[:, g * n_state:(g + 1) * n_state],
                                  (((0,), (0,)), ((), ())), preferred_element_type=F32)
            dec = jnp.concatenate(
                [jnp.broadcast_to(cd_row[:, h:h + 1], (headdim, n_state))
                 for h in range(g * hpg, (g + 1) * hpg)], axis=0)
            snew_ref[b, g * hpg:(g + 1) * hpg] = (s0 * dec + upd).reshape(hpg, headdim, n_state)
        return carry

    lax.fori_loop(0, bt, body, 0)


def _ssd_sample_state(s0, c_b, b_b, xdd_b, cd, *, bt):
    nb, heads, headdim, n_state = s0.shape
    nt = c_b.shape[1]
    gn_w = c_b.shape[2]
    d_ssm = xdd_b.shape[2]
    return pl.pallas_call(
        functools.partial(_ssd_sample_state_kernel, bt=bt, groups=SSD_GROUPS, n_state=n_state,
                          headdim=headdim),
        grid=(nb // bt,),
        in_specs=[pl.BlockSpec((bt, heads, headdim, n_state), lambda i: (i, 0, 0, 0)),
                  pl.BlockSpec((bt, nt, gn_w), lambda i: (i, 0, 0)),
                  pl.BlockSpec((bt, nt, gn_w), lambda i: (i, 0, 0)),
                  pl.BlockSpec((bt, nt, d_ssm), lambda i: (i, 0, 0)),
                  pl.BlockSpec((bt, LANES), lambda i: (i, 0))],
        out_specs=[pl.BlockSpec((bt, heads, headdim, n_state), lambda i: (i, 0, 0, 0)),
                   pl.BlockSpec((bt, nt, d_ssm), lambda i: (i, 0, 0))],
        out_shape=[jax.ShapeDtypeStruct(s0.shape, F32),
                   jax.ShapeDtypeStruct((nb, nt, d_ssm), F32)],
        compiler_params=_params(1),
        name="ssd_sample_state",
    )(s0, c_b, b_b, xdd_b, cd)


def _ssd_sample_post_kernel(ydp_ref, yoff_ref, ecum_ref, z_ref, gn_ref, out_ref, *, groups):
    nt, bb, d_ssm = ydp_ref.shape
    y = (ydp_ref[...] + yoff_ref[...] * ecum_ref[...]).reshape(nt * bb, d_ssm)
    z = z_ref[...].reshape(nt * bb, d_ssm)
    out_ref[...] = _group_norm_out(y, z, gn_ref, groups).reshape(nt, bb, d_ssm)


def _ssd_sample_post(ydp, yoff, ecum, proj3, gn, d_ssm, d_xbc, *, bb):
    nt, nb, n_main = proj3.shape
    z_blk = (n_main - d_xbc - d_ssm) // d_ssm
    slab = pl.BlockSpec((nt, bb, d_ssm), lambda i: (0, i, 0))
    return pl.pallas_call(
        functools.partial(_ssd_sample_post_kernel, groups=SSD_GROUPS),
        grid=(nb // bb,),
        in_specs=[slab, slab, slab, pl.BlockSpec((nt, bb, d_ssm), lambda i: (0, i, z_blk)),
                  pl.BlockSpec((1, d_ssm), lambda i: (0, 0))],
        out_specs=slab,
        out_shape=jax.ShapeDtypeStruct((nt, nb, d_ssm), BF16),
        compiler_params=_params(1),
        name="ssd_sample_post",
    )(ydp, yoff, ecum, proj3, gn)


def _pad_lanes(v):
    return jnp.pad(v.astype(F32), (0, LANES - v.shape[0])).reshape(1, LANES)


def kernel(x_prompt, x_sample, state_rglru_h, state_rglru_conv, state_ssd_h, state_ssd_conv, state_ffn_conv, g_mix, w_in, rg_conv_w, rg_conv_b, rg_gate_a_w, rg_gate_a_b, rg_gate_i_w, rg_gate_i_b, rg_lambda, g_rg_out, ssd_conv_w, ssd_conv_b, ssd_dt_bias, ssd_A_log, ssd_D, g_ssd_norm, w_out, g_ffn, w_ffn_up, ffn_conv_w, ffn_conv_b, w_ffn_down, g_final):
    depth = g_mix.shape[0]
    pb, seq, d_model = x_prompt.shape
    sb, dec_seq, _ = x_sample.shape
    d_rnn = rg_lambda.shape[1]
    d_ssm = g_ssd_norm.shape[1]
    d_xbc = ssd_conv_w.shape[2]
    d_ff = ffn_conv_w.shape[2]
    heads = ssd_A_log.shape[1]
    headdim = d_ssm // heads
    n_state = state_ssd_h.shape[-1]
    n_main = 2 * d_rnn + d_ssm + d_xbc
    k_rg = rg_conv_w.shape[1]
    k_ssd = ssd_conv_w.shape[1]
    k_ffn = ffn_conv_w.shape[1]
    gn_w = SSD_GROUPS * n_state
    assert heads <= LANES and n_state == LANES and k_ffn == 3 and dec_seq >= max(k_rg, k_ssd) - 1

    hid = lax.broadcasted_iota(jnp.int32, (LANES, d_ssm), 0)
    cid = lax.broadcasted_iota(jnp.int32, (LANES, d_ssm), 1)
    e_sel = (hid == cid // headdim).astype(BF16)
    gid = lax.broadcasted_iota(jnp.int32, (gn_w, d_ssm), 0) // n_state
    cgid = lax.broadcasted_iota(jnp.int32, (gn_w, d_ssm), 1) // (d_ssm // SSD_GROUPS)
    g_sel = (gid == cgid).astype(BF16)

    yp = x_prompt.reshape(pb * seq, d_model)
    ys = jnp.transpose(x_sample, (1, 0, 2)).reshape(dec_seq * sb, d_model)
    p_new = [[], [], [], [], []]
    s_new = [[], [], [], [], []]
    row = lambda v: v.reshape(1, -1)
    swap = lambda a3: jnp.transpose(a3, (1, 0, 2))
    bb = 32

    for l in range(depth):
        wdt_b = jnp.pad(w_in[l][:, n_main:], ((0, 0), (0, LANES - heads))).astype(BF16)
        wa_b = rg_gate_a_w[l].astype(BF16)
        wi_b = rg_gate_i_w[l].astype(BF16)
        rg_args = (rg_conv_w[l], row(rg_conv_b[l]), wa_b, row(rg_gate_a_b[l]), wi_b,
                   row(rg_gate_i_b[l]), row(rg_lambda[l]), row(g_rg_out[l]))
        dtb = _pad_lanes(ssd_dt_bias[l])
        alog = _pad_lanes(ssd_A_log[l])
        dexp = jnp.repeat(ssd_D[l].astype(F32), headdim).reshape(1, d_ssm)
        gn = row(g_ssd_norm[l])

        w_in_b = w_in[l][:, :n_main].astype(BF16)
        ms = ys.shape[0]
        proj_s, dt_s = _in_proj(ys, row(g_mix[l]), w_in_b, wdt_b, n_main, tm=ms, tn=TN_IN)

        proj_p, dt_p = _in_proj(yp, row(g_mix[l]), w_in_b, wdt_b, n_main, tm=TM, tn=TN_IN)
        rg_steps = pb * (seq // TT_RG)
        ssd_steps = pb * (seq // SSD_CHUNK)
        rb_ssd = d_model // (2 * rg_steps + ssd_steps)
        assert rb_ssd % 16 == 0 and (2 * rg_steps + ssd_steps) * rb_ssd == d_model
        rg_p, hl_p, w_up_ta = _rglru_prompt(proj_p, pb, seq, d_rnn, *rg_args, w_ffn_up[l], TN_UP,
                                            2 * rb_ssd, tt=TT_RG)
        ssd_p, hfin_p, w_up_tb, w_out_b = _ssd_prompt(
            proj_p, dt_p, pb, seq, d_ssm, d_xbc, n_state, headdim, ssd_conv_w[l], row(ssd_conv_b[l]),
            dtb, alog, dexp, gn, e_sel, w_ffn_up[l], TN_UP, rb_ssd, 2 * rb_ssd * rg_steps, w_out[l])
        x1_p, h2_p, ss_p = _out_proj(rg_p, ssd_p, w_out_b, yp, row(g_ffn[l]), tm=TM, tn=TN_OUT)
        act_p, gst_p, w_down_t = _ffn_up_prompt(h2_p, ss_p, w_up_ta, w_up_tb, ffn_conv_w[l],
                                                row(ffn_conv_b[l]), seq, w_ffn_down[l], TN_DOWN, tm=TM)
        proj_p3 = proj_p.reshape(pb, seq, n_main)
        p_new[0].append(hl_p[:, SUBLANES - 1, :])
        p_new[1].append(proj_p3[:, seq - (k_rg - 1):, :d_rnn])
        p_new[2].append(hfin_p.reshape(pb, heads, headdim, n_state))
        p_new[3].append(proj_p3[:, seq - (k_ssd - 1):, n_main - d_xbc:])
        tiles = seq // TM
        p_new[4].append(gst_p[tiles - 1::tiles, SUBLANES - (k_ffn - 1):, :])

        proj_s3 = proj_s.reshape(dec_seq, sb, n_main)
        rg_s, hl_s = _rglru_sample(proj_s3, swap(state_rglru_conv[l]), state_rglru_h[l], d_rnn,
                                   *rg_args, bb=bb)
        ydp, ecum, xdd, bc, cd = _ssd_sample_pre(
            proj_s3, swap(state_ssd_conv[l]), dt_s.reshape(dec_seq, sb, LANES), d_ssm, d_xbc, n_state,
            ssd_conv_w[l], row(ssd_conv_b[l]), dtb, alog, dexp, e_sel, g_sel, bb=bb)
        bc_b = swap(bc)
        snew, yoff_b = _ssd_sample_state(state_ssd_h[l], bc_b[:, :, gn_w:], bc_b[:, :, :gn_w],
                                         swap(xdd), cd, bt=8)
        ssd_s = _ssd_sample_post(ydp, swap(yoff_b), ecum, proj_s3, gn, d_ssm, d_xbc, bb=bb)
        x1_s, h2_s, ss_s = _out_proj(rg_s.reshape(ms, d_rnn), ssd_s.reshape(ms, d_ssm), w_out_b, ys,
                                     row(g_ffn[l]), tm=ms, tn=TN_OUT)
        act_s, gst_s = _ffn_up_sample(h2_s, ss_s, w_up_ta, w_up_tb, ffn_conv_w[l], row(ffn_conv_b[l]),
                                      swap(state_ffn_conv[l]).reshape(-1, d_ff), sb)
        s_new[0].append(hl_s)
        s_new[1].append(swap(proj_s3[dec_seq - (k_rg - 1):, :, :d_rnn]))
        s_new[2].append(snew)
        s_new[3].append(swap(proj_s3[dec_seq - (k_ssd - 1):, :, n_main - d_xbc:]))
        s_new[4].append(swap(gst_s.reshape(k_ffn - 1, sb, d_ff)))

        last = l == depth - 1
        yp = _ffn_down(act_p, w_down_t, x1_p, row(g_final), last, tm=TM, tk=TK_DOWN)
        ys = _ffn_down(act_s, w_down_t, x1_s, row(g_final), last, tm=ms, tk=TK_DOWN)

    y_prompt = yp.reshape(pb, seq, d_model)
    y_sample = jnp.transpose(ys.reshape(dec_seq, sb, d_model), (1, 0, 2))
    return (y_prompt, y_sample,
            jnp.stack(p_new[0]), jnp.stack(p_new[1]), jnp.stack(p_new[2]), jnp.stack(p_new[3]),
            jnp.stack(p_new[4]),
            jnp.stack(s_new[0]), jnp.stack(s_new[1]), jnp.stack(s_new[2]), jnp.stack(s_new[3]),
            jnp.stack(s_new[4]))
```

```python
import functools

import jax
import jax.numpy as jnp
from jax import lax
from jax.experimental import pallas as pl
from jax.experimental.pallas import tpu as pltpu

F32 = jnp.float32
BF16 = jnp.bfloat16

EPS = 1e-6
RG_C = 8.0
SSD_CHUNK = 128
SSD_GROUPS = 4
LANES = 128
SUBLANES = 8
VMEM_LIMIT = 56 * 2**20

TM = 1024
ROW_CHUNK = 256
TT_RG = 256
TN_IN = 512
TN_OUT = 512
TN_UP = 512
TN_DOWN = 512
TK_DOWN = 4096


def _params(n_axes):
    return pltpu.CompilerParams(dimension_semantics=("arbitrary",) * n_axes,
                                vmem_limit_bytes=VMEM_LIMIT)


def _dot(a, b):
    return jnp.dot(a, b, preferred_element_type=F32)


def _rms_rows(x, g):
    ms = jnp.mean(x * x, axis=-1, keepdims=True)
    return (x * lax.rsqrt(ms + EPS)) * g


def _split3(x):
    hi = x.astype(BF16)
    r = x - hi.astype(F32)
    mid = r.astype(BF16)
    lo = (r - mid.astype(F32)).astype(BF16)
    return hi, mid, lo


def _dot_sel_right(x, sel):
    hi, mid, lo = _split3(x)
    return (_dot(hi, sel) + _dot(mid, sel)) + _dot(lo, sel)


def _dot_sel_left(sel, x):
    hi, mid, lo = _split3(x)
    return (_dot(sel, hi) + _dot(sel, mid)) + _dot(sel, lo)


def _inproj_kernel(x_hbm, g_ref, w_ref, wdt_ref, o_ref, dt_ref, x_buf, hn_ref, sem, *, ni, tm):
    i = pl.program_id(0)
    j = pl.program_id(1)

    def x_in(tile):
        return pltpu.make_async_copy(x_hbm.at[pl.ds(tile * tm, tm)], x_buf, sem)

    @pl.when(j == 0)
    def _():
        @pl.when(i == 0)
        def _():
            x_in(0).start()

        x_in(i).wait()
        for r in range(0, tm, ROW_CHUNK):
            hn_ref[r:r + ROW_CHUNK, :] = _rms_rows(x_buf[r:r + ROW_CHUNK, :], g_ref[...]).astype(BF16)
        dt_ref[...] = _dot(hn_ref[...], wdt_ref[...])

        @pl.when(i + 1 < ni)
        def _():
            x_in(i + 1).start()

    o_ref[...] = _dot(hn_ref[...], w_ref[...])


def _lookahead_block(src_at, buf_ref, sem_ref, blk, n_blocks, first_step):
    slot = blk % 2

    def fetch(b, s):
        return pltpu.make_async_copy(src_at(b), buf_ref.at[s], sem_ref.at[s])

    @pl.when(first_step)
    def _():
        @pl.when(blk == 0)
        def _():
            fetch(0, 0).start()

        fetch(blk, slot).wait()

        @pl.when(blk + 1 < n_blocks)
        def _():
            fetch(blk + 1, 1 - slot).start()

    return slot


def _cast_col_tiles(src_ref, dst_ref):
    n_tiles, _, tn = dst_ref.shape
    for t in range(n_tiles):
        dst_ref[t] = src_ref[:, t * tn:(t + 1) * tn].astype(BF16)


def _in_proj(x2d, g, w_b, wdt_b, n_main, *, tm, tn):
    m, d = x2d.shape
    ni = m // tm
    return pl.pallas_call(
        functools.partial(_inproj_kernel, ni=ni, tm=tm),
        grid=(ni, n_main // tn),
        in_specs=[pl.BlockSpec(memory_space=pl.ANY),
                  pl.BlockSpec((1, d), lambda i, j: (0, 0)),
                  pl.BlockSpec((d, tn), lambda i, j: (0, j)),
                  pl.BlockSpec((d, LANES), lambda i, j: (0, 0))],
        out_specs=[pl.BlockSpec((tm, tn), lambda i, j: (i, j)),
                   pl.BlockSpec((tm, LANES), lambda i, j: (i, 0))],
        out_shape=[jax.ShapeDtypeStruct((m, n_main), F32), jax.ShapeDtypeStruct((m, LANES), F32)],
        scratch_shapes=[pltpu.VMEM((tm, d), F32), pltpu.VMEM((tm, d), BF16), pltpu.SemaphoreType.DMA(())],
        compiler_params=_params(2),
        name="in_proj",
    )(x2d, g, w_b, wdt_b)


def _outproj_kernel(rg_hbm, ssd_hbm, wt_ref, wb_ref, x_ref, g_ref, x1_ref, xg_ref, ss_ref,
                    rg_buf, ssd_buf, rg_sem, ssd_sem, *, ni, tm):
    i = pl.program_id(0)
    j = pl.program_id(1)
    slot = _lookahead_block(lambda b: rg_hbm.at[pl.ds(b * tm, tm)], rg_buf, rg_sem, i, ni, j == 0)
    _lookahead_block(lambda b: ssd_hbm.at[pl.ds(b * tm, tm)], ssd_buf, ssd_sem, i, ni, j == 0)
    x1 = x_ref[...] + (_dot(rg_buf[slot], wt_ref[...]) + _dot(ssd_buf[slot], wb_ref[...]))
    x1_ref[...] = x1
    xg_ref[...] = (x1 * g_ref[...]).astype(BF16)
    part = jnp.broadcast_to(jnp.sum(x1 * x1, axis=-1, keepdims=True), ss_ref.shape)

    @pl.when(j == 0)
    def _():
        ss_ref[...] = part

    @pl.when(j > 0)
    def _():
        ss_ref[...] += part


def _out_proj(rg, ssd, w_b, x2d, g, *, tm, tn):
    m, d = x2d.shape
    k_half = rg.shape[1]
    ni = m // tm
    slots = pltpu.VMEM((2, tm, k_half), BF16)
    return pl.pallas_call(
        functools.partial(_outproj_kernel, ni=ni, tm=tm),
        grid=(ni, d // tn),
        in_specs=[pl.BlockSpec(memory_space=pl.ANY),
                  pl.BlockSpec(memory_space=pl.ANY),
                  pl.BlockSpec((k_half, tn), lambda i, j: (0, j)),
                  pl.BlockSpec((k_half, tn), lambda i, j: (1, j)),
                  pl.BlockSpec((tm, tn), lambda i, j: (i, j)),
                  pl.BlockSpec((1, tn), lambda i, j: (0, j))],
        out_specs=[pl.BlockSpec((tm, tn), lambda i, j: (i, j)),
                   pl.BlockSpec((tm, tn), lambda i, j: (i, j)),
                   pl.BlockSpec((tm, LANES), lambda i, j: (i, 0))],
        out_shape=[jax.ShapeDtypeStruct((m, d), F32),
                   jax.ShapeDtypeStruct((m, d), BF16),
                   jax.ShapeDtypeStruct((m, LANES), F32)],
        scratch_shapes=[slots, slots, pltpu.SemaphoreType.DMA((2,)), pltpu.SemaphoreType.DMA((2,))],
        compiler_params=_params(2),
        name="out_proj",
    )(rg, ssd, w_b, w_b, x2d, g)


def _glu(gate_c, val):
    return (jax.nn.gelu(gate_c) * val).astype(BF16)


def _dot_split_k(h_ref, wa_ref, wb_ref):
    ka = wa_ref.shape[0]
    return _dot(h_ref[:, :ka], wa_ref[...]) + _dot(h_ref[:, ka:], wb_ref[...])


def _inv_rms(ss_ref, width):
    return lax.rsqrt(ss_ref[:, 0:1] / width + EPS)


def _ffn_up_prompt_kernel(xg_ref, ss_ref, wga_ref, wgb_ref, wva_ref, wvb_ref, cw_ref, cb_ref, wsrc_ref,
                          act_ref, gst_ref, wdst_ref, carry_ref, *, tiles_per_seq):
    i = pl.program_id(0)
    j = pl.program_id(1)
    _cast_col_tiles(wsrc_ref, wdst_ref)
    inv = _inv_rms(ss_ref, xg_ref.shape[1])
    gate = _dot_split_k(xg_ref, wga_ref, wgb_ref) * inv
    val = _dot_split_k(xg_ref, wva_ref, wvb_ref) * inv
    tm, tn = gate.shape
    w0, w1, w2 = cw_ref[0:1, :], cw_ref[1:2, :], cw_ref[2:3, :]
    cb = cb_ref[...]
    g1 = pltpu.roll(gate, 1, 0)
    g2 = pltpu.roll(gate, 2, 0)
    act_ref[...] = _glu(((cb + g2 * w0) + g1 * w1) + gate * w2, val)
    @pl.when((i % tiles_per_seq) == 0)
    def _():
        carry_ref[j] = jnp.zeros((SUBLANES, tn), F32)

    prev = carry_ref[j]
    rows = lax.broadcasted_iota(jnp.int32, (SUBLANES, tn), 0)
    t1 = jnp.where(rows < 1, pltpu.roll(prev, 1, 0), g1[0:SUBLANES])
    t2 = jnp.where(rows < 2, pltpu.roll(prev, 2, 0), g2[0:SUBLANES])
    act_ref[0:SUBLANES, :] = _glu(((cb + t2 * w0) + t1 * w1) + gate[0:SUBLANES] * w2,
                                  val[0:SUBLANES])
    last = gate[tm - SUBLANES:tm]
    carry_ref[j] = last
    gst_ref[...] = last


def _ffn_up_prompt(h2, ss, w_ta, w_tb, cw, cb, seq, w_next, tn_next, *, tm):
    m, d = h2.shape
    nj, ka, tn = w_ta.shape[0] // 2, w_ta.shape[1], w_ta.shape[2]
    kb = w_tb.shape[1]
    assert ka + kb == d and ka % LANES == 0
    d_ff = nj * tn
    steps = (m // tm) * nj
    k_next, n_next = w_next.shape
    rb = k_next // steps
    assert rb * steps == k_next and rb % 16 == 0 and n_next % tn_next == 0
    return pl.pallas_call(
        functools.partial(_ffn_up_prompt_kernel, tiles_per_seq=seq // tm),
        grid=(m // tm, nj),
        in_specs=[pl.BlockSpec((tm, d), lambda i, j: (i, 0)),
                  pl.BlockSpec((tm, LANES), lambda i, j: (i, 0)),
                  pl.BlockSpec((None, ka, tn), lambda i, j: (j, 0, 0)),
                  pl.BlockSpec((None, kb, tn), lambda i, j: (j, 0, 0)),
                  pl.BlockSpec((None, ka, tn), lambda i, j: (j + nj, 0, 0)),
                  pl.BlockSpec((None, kb, tn), lambda i, j: (j + nj, 0, 0)),
                  pl.BlockSpec((3, tn), lambda i, j: (0, j)),
                  pl.BlockSpec((1, tn), lambda i, j: (0, j)),
                  pl.BlockSpec((rb, n_next), lambda i, j: (i * nj + j, 0))],
        out_specs=[pl.BlockSpec((tm, tn), lambda i, j: (i, j)),
                   pl.BlockSpec((None, SUBLANES, tn), lambda i, j: (i, 0, j)),
                   pl.BlockSpec((n_next // tn_next, rb, tn_next), lambda i, j: (0, i * nj + j, 0))],
        out_shape=[jax.ShapeDtypeStruct((m, d_ff), BF16),
                   jax.ShapeDtypeStruct((m // tm, SUBLANES, d_ff), F32),
                   jax.ShapeDtypeStruct((n_next // tn_next, k_next, tn_next), BF16)],
        scratch_shapes=[pltpu.VMEM((nj, SUBLANES, tn), F32)],
        compiler_params=_params(2),
        name="ffn_up_prompt",
    )(h2, ss, w_ta, w_tb, w_ta, w_tb, cw, cb, w_next)


def _ffn_up_sample_kernel(xg_ref, ss_ref, wga_ref, wgb_ref, wva_ref, wvb_ref, cw_ref, cb_ref, st_ref,
                          act_ref, nst_ref, *, nb, nt):
    inv = _inv_rms(ss_ref, xg_ref.shape[1])
    gate = _dot_split_k(xg_ref, wga_ref, wgb_ref) * inv
    val = _dot_split_k(xg_ref, wva_ref, wvb_ref) * inv
    w0, w1, w2 = cw_ref[0:1, :], cw_ref[1:2, :], cw_ref[2:3, :]
    cb = cb_ref[...]
    ext = [st_ref[0:nb, :], st_ref[nb:2 * nb, :]] + [gate[t * nb:(t + 1) * nb] for t in range(nt)]
    for t in range(nt):
        gc = ((cb + ext[t] * w0) + ext[t + 1] * w1) + ext[t + 2] * w2
        act_ref[t * nb:(t + 1) * nb, :] = _glu(gc, val[t * nb:(t + 1) * nb])
    nst_ref[0:nb, :] = ext[nt]
    nst_ref[nb:2 * nb, :] = ext[nt + 1]


def _ffn_up_sample(h2, ss, w_ta, w_tb, cw, cb, st, nb):
    m, d = h2.shape
    nj, ka, tn = w_ta.shape[0] // 2, w_ta.shape[1], w_ta.shape[2]
    kb = w_tb.shape[1]
    d_ff = nj * tn
    return pl.pallas_call(
        functools.partial(_ffn_up_sample_kernel, nb=nb, nt=m // nb),
        grid=(nj,),
        in_specs=[pl.BlockSpec((m, d), lambda j: (0, 0)),
                  pl.BlockSpec((m, LANES), lambda j: (0, 0)),
                  pl.BlockSpec((None, ka, tn), lambda j: (j, 0, 0)),
                  pl.BlockSpec((None, kb, tn), lambda j: (j, 0, 0)),
                  pl.BlockSpec((None, ka, tn), lambda j: (j + nj, 0, 0)),
                  pl.BlockSpec((None, kb, tn), lambda j: (j + nj, 0, 0)),
                  pl.BlockSpec((3, tn), lambda j: (0, j)),
                  pl.BlockSpec((1, tn), lambda j: (0, j)),
                  pl.BlockSpec((2 * nb, tn), lambda j: (0, j))],
        out_specs=[pl.BlockSpec((m, tn), lambda j: (0, j)),
                   pl.BlockSpec((2 * nb, tn), lambda j: (0, j))],
        out_shape=[jax.ShapeDtypeStruct((m, d_ff), BF16),
                   jax.ShapeDtypeStruct((2 * nb, d_ff), F32)],
        compiler_params=_params(1),
        name="ffn_up_sample",
    )(h2, ss, w_ta, w_tb, w_ta, w_tb, cw, cb, st)


def _ffn_down_kernel(act_hbm, w_ref, x1_ref, g_ref, y_hbm, rows_ref, sem, act_buf, act_sem,
                     *, ni, nk, nj, tm, tk, tn, final_norm):
    i = pl.program_id(0)
    k = pl.program_id(1)
    j = pl.program_id(2)
    cols = pl.ds(pl.multiple_of(j * tn, tn), tn)

    def rows_out(tile):
        return pltpu.make_async_copy(rows_ref, y_hbm.at[pl.ds(tile * tm, tm)], sem)

    def act_at(b):
        return act_hbm.at[pl.ds((b // nk) * tm, tm), pl.ds(pl.multiple_of((b % nk) * tk, tk), tk)]

    slot = _lookahead_block(act_at, act_buf, act_sem, i * nk + k, ni * nk, j == 0)

    @pl.when(k == 0)
    def _():
        @pl.when((j == 0) & (i > 0))
        def _():
            rows_out(i - 1).wait()

        rows_ref[:, cols] = x1_ref[...]

    rows_ref[:, cols] += _dot(act_buf[slot], w_ref[...])

    @pl.when((k == nk - 1) & (j == nj - 1))
    def _():
        if final_norm:
            for r in range(0, tm, ROW_CHUNK):
                rows_ref[r:r + ROW_CHUNK, :] = _rms_rows(rows_ref[r:r + ROW_CHUNK, :], g_ref[...])
        rows_out(i).start()

        @pl.when(i == ni - 1)
        def _():
            rows_out(i).wait()


def _ffn_down(act, w_t, x1, g, final_norm, *, tm, tk):
    m, d = x1.shape
    nj, d_ff, tn = w_t.shape
    ni, nk = m // tm, d_ff // tk
    return pl.pallas_call(
        functools.partial(_ffn_down_kernel, ni=ni, nk=nk, nj=nj, tm=tm, tk=tk, tn=tn,
                          final_norm=final_norm),
        grid=(ni, nk, nj),
        in_specs=[pl.BlockSpec(memory_space=pl.ANY),
                  pl.BlockSpec((None, tk, tn), lambda i, k, j: (j, k, 0)),
                  pl.BlockSpec((tm, tn), lambda i, k, j: (i, jnp.where(k == 0, j, nj - 1))),
                  pl.BlockSpec((1, d), lambda i, k, j: (0, 0))],
        out_specs=pl.BlockSpec(memory_space=pl.ANY),
        out_shape=jax.ShapeDtypeStruct((m, d), F32),
        scratch_shapes=[pltpu.VMEM((tm, d), F32), pltpu.SemaphoreType.DMA(()),
                        pltpu.VMEM((2, tm, tk), BF16), pltpu.SemaphoreType.DMA((2,))],
        compiler_params=_params(3),
        name="ffn_down",
    )(act, w_t, x1, g)


def _softplus(x):
    return jnp.maximum(x, 0.0) + jnp.log1p(jnp.exp(-jnp.abs(x)))


def _rglru_coeffs(xc, wa_ref, ba_ref, wi_ref, bi_ref, lam_ref):
    xb = xc.astype(BF16)
    heads, blk = wa_ref.shape[0], wa_ref.shape[1]
    ga = jnp.concatenate([_dot(xb[:, h * blk:(h + 1) * blk], wa_ref[h]) for h in range(heads)], axis=1)
    gi = jnp.concatenate([_dot(xb[:, h * blk:(h + 1) * blk], wi_ref[h]) for h in range(heads)], axis=1)
    gate_r = jax.nn.sigmoid(ga + ba_ref[...])
    gate_i = jax.nn.sigmoid(gi + bi_ref[...])
    log_a = (-RG_C * gate_r) * _softplus(-lam_ref[...])
    a = jnp.exp(log_a)
    bx = jnp.sqrt(-jnp.tanh(log_a) * (a * a + 1.0)) * (gate_i * xc)
    return a, bx


def _rg_out(h, gate, g_ref):
    return _rms_rows(h * jax.nn.gelu(gate), g_ref[...]).astype(BF16)


def _conv_tile(x, prev, w_ref, b_ref, k_taps):
    t_len, width = x.shape
    groups = t_len // SUBLANES
    x3 = x.reshape(groups, SUBLANES, width)
    pos = lax.broadcasted_iota(jnp.int32, x3.shape, 1)
    out = b_ref[...].reshape(1, 1, width)
    for k in range(k_taps):
        s = k_taps - 1 - k
        wk = w_ref[k:k + 1, :].reshape(1, 1, width)
        if s == 0:
            shifted = x3
        else:
            rot = pltpu.roll(x3, s, 1)
            rot_prev = jnp.concatenate([pltpu.roll(prev, s, 0)[None], rot[:groups - 1]], axis=0)
            shifted = jnp.where(pos >= s, rot, rot_prev)
        out = out + shifted * wk
    return out.reshape(t_len, width)


def _rglru_prompt_kernel(x_ref, gate_ref, cw_ref, cb_ref, wa_ref, ba_ref, wi_ref, bi_ref, lam_ref,
                         g_ref, wsrc_ref, out_ref, hlast_ref, wdst_ref, xprev_ref, hc_ref, a_s, b_s, h_s,
                         *, k_taps):
    c = pl.program_id(1)
    _cast_col_tiles(wsrc_ref, wdst_ref)

    @pl.when(c == 0)
    def _():
        xprev_ref[...] = jnp.zeros_like(xprev_ref)
        hc_ref[...] = jnp.zeros_like(hc_ref)

    x = x_ref[...]
    t_len, width = x.shape
    xc = _conv_tile(x, xprev_ref[...], cw_ref, cb_ref, k_taps)
    xprev_ref[...] = x[t_len - SUBLANES:t_len]
    a, b = _rglru_coeffs(xc, wa_ref, ba_ref, wi_ref, bi_ref, lam_ref)
    groups = t_len // SUBLANES
    a = a.reshape(groups, SUBLANES, width)
    b = b.reshape(groups, SUBLANES, width)
    pos = lax.broadcasted_iota(jnp.int32, a.shape, 1)
    for s in (1, 2, 4):
        a_sh = pltpu.roll(a, s, 1)
        b_sh = pltpu.roll(b, s, 1)
        m = pos >= s
        b = jnp.where(m, a * b_sh + b, b)
        a = jnp.where(m, a * a_sh, a)
    a_s[...] = a.reshape(t_len, width)
    b_s[...] = b.reshape(t_len, width)
    h = hc_ref[...]
    for grp in range(t_len // SUBLANES):
        lo, hi = grp * SUBLANES, (grp + 1) * SUBLANES
        hl = jnp.broadcast_to(h[SUBLANES - 1:SUBLANES, :], (SUBLANES, width))
        h = a_s[lo:hi, :] * hl + b_s[lo:hi, :]
        h_s[lo:hi, :] = h
    hc_ref[...] = h
    hlast_ref[...] = h
    out_ref[...] = _rg_out(h_s[...], gate_ref[...], g_ref)


def _rglru_prompt(proj, nb, seq, d_rnn, cw, cb, wa, ba, wi, bi, lam, g, w_next, tn_next, rb, *, tt):
    nc = seq // tt
    heads, blk = wa.shape[0], wa.shape[1]
    k_next, n_next = w_next.shape
    vec = pl.BlockSpec((1, d_rnn), lambda b, c: (0, 0))
    gate_w = pl.BlockSpec((heads, blk, blk), lambda b, c: (0, 0, 0))
    return pl.pallas_call(
        functools.partial(_rglru_prompt_kernel, k_taps=cw.shape[0]),
        grid=(nb, nc),
        in_specs=[pl.BlockSpec((tt, d_rnn), lambda b, c: (b * nc + c, 0)),
                  pl.BlockSpec((tt, d_rnn), lambda b, c: (b * nc + c, 1)),
                  pl.BlockSpec((cw.shape[0], d_rnn), lambda b, c: (0, 0)),
                  vec, gate_w, vec, gate_w, vec, vec, vec,
                  pl.BlockSpec((rb, n_next), lambda b, c: (b * nc + c, 0))],
        out_specs=[pl.BlockSpec((tt, d_rnn), lambda b, c: (b * nc + c, 0)),
                   pl.BlockSpec((None, SUBLANES, d_rnn), lambda b, c: (b, 0, 0)),
                   pl.BlockSpec((n_next // tn_next, rb, tn_next), lambda b, c: (0, b * nc + c, 0))],
        out_shape=[jax.ShapeDtypeStruct((nb * seq, d_rnn), BF16),
                   jax.ShapeDtypeStruct((nb, SUBLANES, d_rnn), F32),
                   jax.ShapeDtypeStruct((n_next // tn_next, nb * nc * rb, tn_next), BF16)],
        scratch_shapes=[pltpu.VMEM((SUBLANES, d_rnn), F32), pltpu.VMEM((SUBLANES, d_rnn), F32),
                        pltpu.VMEM((tt, d_rnn), F32), pltpu.VMEM((tt, d_rnn), F32),
                        pltpu.VMEM((tt, d_rnn), F32)],
        compiler_params=_params(2),
        name="rglru_prompt",
    )(proj, proj, cw, cb, wa, ba, wi, bi, lam, g, w_next)


def _conv_slabs(st_ref, x_ref, w_ref, b_ref):
    k_taps = w_ref.shape[0]
    ext = [st_ref[s] for s in range(k_taps - 1)] + [x_ref[t] for t in range(x_ref.shape[0])]
    convs = []
    for t in range(x_ref.shape[0]):
        acc = b_ref[...]
        for k in range(k_taps):
            acc = acc + ext[t + k] * w_ref[k:k + 1, :]
        convs.append(acc)
    return jnp.concatenate(convs, axis=0)


def _rglru_sample_kernel(x_ref, gate_ref, st_ref, h0_ref, cw_ref, cb_ref, wa_ref, ba_ref, wi_ref,
                         bi_ref, lam_ref, g_ref, out_ref, hlast_ref, h_s):
    nt, bb, width = x_ref.shape
    xc = _conv_slabs(st_ref, x_ref, cw_ref, cb_ref)
    a, b = _rglru_coeffs(xc, wa_ref, ba_ref, wi_ref, bi_ref, lam_ref)
    h = h0_ref[...]
    for t in range(nt):
        h = a[t * bb:(t + 1) * bb] * h + b[t * bb:(t + 1) * bb]
        h_s[t * bb:(t + 1) * bb, :] = h
    hlast_ref[...] = h
    gate = gate_ref[...].reshape(nt * bb, width)
    out_ref[...] = _rg_out(h_s[...], gate, g_ref).reshape(nt, bb, width)


def _rglru_sample(proj3, st3, h0, d_rnn, cw, cb, wa, ba, wi, bi, lam, g, *, bb):
    nt, nb, _ = proj3.shape
    heads, blk = wa.shape[0], wa.shape[1]
    k_taps = cw.shape[0]
    vec = pl.BlockSpec((1, d_rnn), lambda i: (0, 0))
    gate_w = pl.BlockSpec((heads, blk, blk), lambda i: (0, 0, 0))
    return pl.pallas_call(
        _rglru_sample_kernel,
        grid=(nb // bb,),
        in_specs=[pl.BlockSpec((nt, bb, d_rnn), lambda i: (0, i, 0)),
                  pl.BlockSpec((nt, bb, d_rnn), lambda i: (0, i, 1)),
                  pl.BlockSpec((k_taps - 1, bb, d_rnn), lambda i: (0, i, 0)),
                  pl.BlockSpec((bb, d_rnn), lambda i: (i, 0)),
                  pl.BlockSpec((k_taps, d_rnn), lambda i: (0, 0)),
                  vec, gate_w, vec, gate_w, vec, vec, vec],
        out_specs=[pl.BlockSpec((nt, bb, d_rnn), lambda i: (0, i, 0)),
                   pl.BlockSpec((bb, d_rnn), lambda i: (i, 0))],
        out_shape=[jax.ShapeDtypeStruct((nt, nb, d_rnn), BF16),
                   jax.ShapeDtypeStruct((nb, d_rnn), F32)],
        scratch_shapes=[pltpu.VMEM((nt * bb, d_rnn), F32)],
        compiler_params=_params(1),
        name="rglru_sample",
    )(proj3, proj3, st3, h0, cw, cb, wa, ba, wi, bi, lam, g)


def _silu(x):
    return x * jax.nn.sigmoid(x)


def _group_norm_out(y, z, gn_ref, groups):
    u = y * _silu(z)
    gw = u.shape[1] // groups
    outs = []
    for g in range(groups):
        ug = u[:, g * gw:(g + 1) * gw]
        ms = jnp.mean(ug * ug, axis=-1, keepdims=True)
        outs.append((ug * lax.rsqrt(ms + EPS)) * gn_ref[:, g * gw:(g + 1) * gw])
    return jnp.concatenate(outs, axis=1).astype(BF16)


def _ssd_prompt_kernel(z_ref, xbc_ref, dt_ref, cw_ref, cb_ref, dtb_ref, alog_ref, dexp_ref, gn_ref,
                       e_ref, wsrc_ref, wsrc2_ref, out_ref, hfin_ref, wdst_ref, wdst2_ref,
                       carry_ref, s_ref, y_s, *, k_taps, d_ssm, n_state, groups, headdim, nc):
    c = pl.program_id(1)
    _cast_col_tiles(wsrc_ref, wdst_ref)
    wdst2_ref[...] = wsrc2_ref[...].astype(BF16)

    @pl.when(c == 0)
    def _():
        carry_ref[...] = jnp.zeros_like(carry_ref)
        s_ref[...] = jnp.zeros_like(s_ref)

    xbc = xbc_ref[...]
    q = xbc.shape[0]
    xc = _silu(_conv_tile(xbc, carry_ref[...], cw_ref, cb_ref, k_taps))
    carry_ref[...] = xbc[q - SUBLANES:q]
    gn_w = groups * n_state
    xs = xc[:, :d_ssm]
    bm = xc[:, d_ssm:d_ssm + gn_w]
    cm = xc[:, d_ssm + gn_w:d_ssm + 2 * gn_w]

    dt = _softplus(dt_ref[...] + dtb_ref[...])
    dta = dt * (-jnp.exp(alog_ref[...]))
    row = lax.broadcasted_iota(jnp.int32, (q, q), 0)
    col = lax.broadcasted_iota(jnp.int32, (q, q), 1)
    causal = row >= col
    tril = jnp.where(causal, 1.0, 0.0).astype(BF16)
    cum = _dot_sel_left(tril, dta)
    cum_t = cum.T
    e = e_ref[...]
    dt_e = _dot_sel_right(dt, e)
    cum_e = _dot_sel_right(cum, e)
    ecum_e = jnp.exp(cum_e)
    xdt = xs * dt_e
    cum_last = cum_e[q - 1:q, :]
    xdd = (xdt * jnp.exp(cum_last - cum_e)).astype(BF16)
    chunk_decay = jnp.exp(cum_last)

    heads_per_group = d_ssm // headdim // groups
    pair_w = 2 * headdim
    gw = heads_per_group * headdim
    lane = lax.broadcasted_iota(jnp.int32, (q, pair_w), 1)
    for g in range(groups):
        cg = cm[:, g * n_state:(g + 1) * n_state].astype(BF16)
        bg32 = bm[:, g * n_state:(g + 1) * n_state]
        bg = bg32.astype(BF16)
        cb_g = lax.dot_general(cg, bg, (((1,), (1,)), ((), ())), preferred_element_type=F32)
        s_old = s_ref[:, g * gw:(g + 1) * gw]
        y_s[:, g * gw:(g + 1) * gw] = _dot(cg, s_old.astype(BF16)) * ecum_e[:, g * gw:(g + 1) * gw]
        st_g = _dot(bg32.T.astype(BF16), xdd[:, g * gw:(g + 1) * gw])
        s_ref[:, g * gw:(g + 1) * gw] = s_old * chunk_decay[:, g * gw:(g + 1) * gw] + st_g
        for pr in range(heads_per_group // 2):
            h0 = g * heads_per_group + 2 * pr
            ms = []
            for h in (h0, h0 + 1):
                seg = (jnp.broadcast_to(cum[:, h:h + 1], (q, q))
                       - jnp.broadcast_to(cum_t[h:h + 1, :], (q, q)))
                ms.append(cb_g * jnp.exp(jnp.where(causal, seg, -jnp.inf)))
            lo_c = h0 * headdim
            xp = xdt[:, lo_c:lo_c + pair_w]
            lhs = jnp.concatenate(ms, axis=1).astype(BF16)
            rhs = jnp.concatenate([jnp.where(lane < headdim, xp, 0.0),
                                   jnp.where(lane >= headdim, xp, 0.0)], axis=0).astype(BF16)
            y_s[:, lo_c:lo_c + pair_w] += _dot(lhs, rhs)

    y = y_s[...] + dexp_ref[...] * xs
    out_ref[...] = _group_norm_out(y, z_ref[...], gn_ref, groups)

    @pl.when(c == nc - 1)
    def _():
        for blk in range(d_ssm // LANES):
            hfin_ref[blk * LANES:(blk + 1) * LANES, :] = s_ref[:, blk * LANES:(blk + 1) * LANES].T


def _ssd_prompt(proj, dt_raw, nb, seq, d_ssm, d_xbc, n_state, headdim, cw, cb, dtb, alog, dexp, gn, e,
                w_next, tn_next, rb, row0, w_plain):
    q = SSD_CHUNK
    nc = seq // q
    k_plain, n_plain = w_plain.shape
    rb2 = k_plain // (nb * nc)
    assert rb2 * nb * nc == k_plain and rb2 % 16 == 0
    z_blk = (proj.shape[1] - d_xbc - d_ssm) // d_ssm
    x_blk = (proj.shape[1] - d_xbc) // d_xbc
    assert z_blk * d_ssm + d_ssm + d_xbc == proj.shape[1] and x_blk * d_xbc + d_xbc == proj.shape[1]
    k_next, n_next = w_next.shape
    n_tiles = n_next // tn_next
    blk0 = row0 // rb
    assert blk0 * rb == row0 and row0 + nb * nc * rb == k_next
    one = lambda w: pl.BlockSpec((1, w), lambda b, c: (0, 0))
    return pl.pallas_call(
        functools.partial(_ssd_prompt_kernel, k_taps=cw.shape[0], d_ssm=d_ssm, n_state=n_state,
                          groups=SSD_GROUPS, headdim=headdim, nc=nc),
        grid=(nb, nc),
        in_specs=[pl.BlockSpec((q, d_ssm), lambda b, c: (b * nc + c, z_blk)),
                  pl.BlockSpec((q, d_xbc), lambda b, c: (b * nc + c, x_blk)),
                  pl.BlockSpec((q, LANES), lambda b, c: (b * nc + c, 0)),
                  pl.BlockSpec((cw.shape[0], d_xbc), lambda b, c: (0, 0)),
                  one(d_xbc), one(LANES), one(LANES), one(d_ssm), one(d_ssm),
                  pl.BlockSpec((LANES, d_ssm), lambda b, c: (0, 0)),
                  pl.BlockSpec((rb, n_next), lambda b, c: (blk0 + b * nc + c, 0)),
                  pl.BlockSpec((rb2, n_plain), lambda b, c: (b * nc + c, 0))],
        out_specs=[pl.BlockSpec((q, d_ssm), lambda b, c: (b * nc + c, 0)),
                   pl.BlockSpec((None, d_ssm, n_state), lambda b, c: (b, 0, 0)),
                   pl.BlockSpec((n_tiles, rb, tn_next), lambda b, c: (0, b * nc + c, 0)),
                   pl.BlockSpec((rb2, n_plain), lambda b, c: (b * nc + c, 0))],
        out_shape=[jax.ShapeDtypeStruct((nb * seq, d_ssm), BF16),
                   jax.ShapeDtypeStruct((nb, d_ssm, n_state), F32),
                   jax.ShapeDtypeStruct((n_tiles, k_next - row0, tn_next), BF16),
                   jax.ShapeDtypeStruct((k_plain, n_plain), BF16)],
        scratch_shapes=[pltpu.VMEM((SUBLANES, d_xbc), F32), pltpu.VMEM((n_state, d_ssm), F32),
                        pltpu.VMEM((q, d_ssm), F32)],
        compiler_params=_params(2),
        name="ssd_prompt",
    )(proj, proj, dt_raw, cw, cb, dtb, alog, dexp, gn, e, w_next, w_plain)


def _ssd_sample_pre_kernel(xbc_ref, st_ref, dt_ref, cw_ref, cb_ref, dtb_ref, alog_ref, dexp_ref,
                           e_ref, gsel_ref, ydp_ref, ecum_ref, xdd_ref, bc_ref, cd_ref,
                           *, d_ssm, n_state, groups):
    nt, bb, _ = xbc_ref.shape
    xc = _silu(_conv_slabs(st_ref, xbc_ref, cw_ref, cb_ref))
    gn_w = groups * n_state
    xs = xc[:, :d_ssm]
    bm = xc[:, d_ssm:d_ssm + gn_w]
    cm = xc[:, d_ssm + gn_w:d_ssm + 2 * gn_w]
    bc_ref[...] = xc[:, d_ssm:d_ssm + 2 * gn_w].reshape(nt, bb, 2 * gn_w)

    dt = _softplus(dt_ref[...].reshape(nt * bb, LANES) + dtb_ref[...])
    dta = dt * (-jnp.exp(alog_ref[...]))
    cums = [dta[0:bb]]
    for t in range(1, nt):
        cums.append(cums[-1] + dta[t * bb:(t + 1) * bb])
    cum = jnp.concatenate(cums, axis=0)
    cd_ref[...] = jnp.exp(cums[-1])
    e = e_ref[...]
    dt_e = _dot_sel_right(dt, e)
    cum_e = _dot_sel_right(cum, e)
    ecum_ref[...] = jnp.exp(cum_e).reshape(nt, bb, d_ssm)
    xdt = xs * dt_e
    sl = lambda v, t: v[t * bb:(t + 1) * bb]
    cum_last = sl(cum_e, nt - 1)
    gsel = gsel_ref[...]
    for t in range(nt):
        xdd_ref[t] = sl(xdt, t) * jnp.exp(cum_last - sl(cum_e, t))
        acc = dexp_ref[...] * sl(xs, t)
        for s in range(t + 1):
            cb_e = _dot_sel_right(sl(cm, t) * sl(bm, s), gsel)
            acc = acc + (cb_e * jnp.exp(sl(cum_e, t) - sl(cum_e, s))) * sl(xdt, s)
        ydp_ref[t] = acc


def _ssd_sample_pre(proj3, st3, dt3, d_ssm, d_xbc, n_state, cw, cb, dtb, alog, dexp, e, gsel, *, bb):
    nt, nb, n_main = proj3.shape
    k_taps = cw.shape[0]
    x_blk = (n_main - d_xbc) // d_xbc
    gn_w = SSD_GROUPS * n_state
    full = lambda r, w: pl.BlockSpec((r, w), lambda i: (0, 0))
    slab = lambda n, w: pl.BlockSpec((n, bb, w), lambda i: (0, i, 0))
    f32 = lambda *s: jax.ShapeDtypeStruct(s, F32)
    return pl.pallas_call(
        functools.partial(_ssd_sample_pre_kernel, d_ssm=d_ssm, n_state=n_state, groups=SSD_GROUPS),
        grid=(nb // bb,),
        in_specs=[pl.BlockSpec((nt, bb, d_xbc), lambda i: (0, i, x_blk)),
                  slab(k_taps - 1, d_xbc), slab(nt, LANES), full(k_taps, d_xbc),
                  full(1, d_xbc), full(1, LANES), full(1, LANES), full(1, d_ssm),
                  full(LANES, d_ssm), full(gn_w, d_ssm)],
        out_specs=[slab(nt, d_ssm), slab(nt, d_ssm), slab(nt, d_ssm), slab(nt, 2 * gn_w),
                   pl.BlockSpec((bb, LANES), lambda i: (i, 0))],
        out_shape=[f32(nt, nb, d_ssm), f32(nt, nb, d_ssm), f32(nt, nb, d_ssm),
                   f32(nt, nb, 2 * gn_w), f32(nb, LANES)],
        compiler_params=_params(1),
        name="ssd_sample_pre",
    )(proj3, st3, dt3, cw, cb, dtb, alog, dexp, e, gsel)


def _ssd_sample_state_kernel(s0_ref, c_ref, b_ref, xdd_ref, cd_ref, snew_ref, yoff_ref,
                             *, bt, groups, n_state, headdim):
    heads = s0_ref.shape[1]
    hpg = heads // groups
    gw = hpg * headdim

    def body(b, carry):
        cd_row = cd_ref[pl.ds(b, 1), :]
        cb = c_ref[b].astype(BF16)
        bb = b_ref[b].astype(BF16)
        xb = xdd_ref[b].astype(BF16)
        for g in range(groups):
            s0 = s0_ref[b, g * hpg:(g + 1) * hpg].reshape(gw, n_state)
            yoff = lax.dot_general(cb[:, g * n_state:(g + 1) * n_state], s0.astype(BF16),
                                   (((1,), (1,)), ((), ())), preferred_element_type=F32)
            yoff_ref[b, :, g * gw:(g + 1) * gw] = yoff
            upd = lax.dot_general(xb[:, g * gw:(g + 1) * gw], bb[:, g * n_state:(g + 1) * n_state],
                                  (((0,), (0,)), ((), ())), preferred_element_type=F32)
            dec = jnp.concatenate(
                [jnp.broadcast_to(cd_row[:, h:h + 1], (headdim, n_state))
                 for h in range(g * hpg, (g + 1) * hpg)], axis=0)
            snew_ref[b, g * hpg:(g + 1) * hpg] = (s0 * dec + upd).reshape(hpg, headdim, n_state)
        return carry

    lax.fori_loop(0, bt, body, 0)


def _ssd_sample_state(s0, c_b, b_b, xdd_b, cd, *, bt):
    nb, heads, headdim, n_state = s0.shape
    nt = c_b.shape[1]
    gn_w = c_b.shape[2]
    d_ssm = xdd_b.shape[2]
    return pl.pallas_call(
        functools.partial(_ssd_sample_state_kernel, bt=bt, groups=SSD_GROUPS, n_state=n_state,
                          headdim=headdim),
        grid=(nb // bt,),
        in_specs=[pl.BlockSpec((bt, heads, headdim, n_state), lambda i: (i, 0, 0, 0)),
                  pl.BlockSpec((bt, nt, gn_w), lambda i: (i, 0, 0)),
                  pl.BlockSpec((bt, nt, gn_w), lambda i: (i, 0, 0)),
                  pl.BlockSpec((bt, nt, d_ssm), lambda i: (i, 0, 0)),
                  pl.BlockSpec((bt, LANES), lambda i: (i, 0))],
        out_specs=[pl.BlockSpec((bt, heads, headdim, n_state), lambda i: (i, 0, 0, 0)),
                   pl.BlockSpec((bt, nt, d_ssm), lambda i: (i, 0, 0))],
        out_shape=[jax.ShapeDtypeStruct(s0.shape, F32),
                   jax.ShapeDtypeStruct((nb, nt, d_ssm), F32)],
        compiler_params=_params(1),
        name="ssd_sample_state",
    )(s0, c_b, b_b, xdd_b, cd)


def _ssd_sample_post_kernel(ydp_ref, yoff_ref, ecum_ref, z_ref, gn_ref, out_ref, *, groups):
    nt, bb, d_ssm = ydp_ref.shape
    y = (ydp_ref[...] + yoff_ref[...] * ecum_ref[...]).reshape(nt * bb, d_ssm)
    z = z_ref[...].reshape(nt * bb, d_ssm)
    out_ref[...] = _group_norm_out(y, z, gn_ref, groups).reshape(nt, bb, d_ssm)


def _ssd_sample_post(ydp, yoff, ecum, proj3, gn, d_ssm, d_xbc, *, bb):
    nt, nb, n_main = proj3.shape
    z_blk = (n_main - d_xbc - d_ssm) // d_ssm
    slab = pl.BlockSpec((nt, bb, d_ssm), lambda i: (0, i, 0))
    return pl.pallas_call(
        functools.partial(_ssd_sample_post_kernel, groups=SSD_GROUPS),
        grid=(nb // bb,),
        in_specs=[slab, slab, slab, pl.BlockSpec((nt, bb, d_ssm), lambda i: (0, i, z_blk)),
                  pl.BlockSpec((1, d_ssm), lambda i: (0, 0))],
        out_specs=slab,
        out_shape=jax.ShapeDtypeStruct((nt, nb, d_ssm), BF16),
        compiler_params=_params(1),
        name="ssd_sample_post",
    )(ydp, yoff, ecum, proj3, gn)


def _pad_lanes(v):
    return jnp.pad(v.astype(F32), (0, LANES - v.shape[0])).reshape(1, LANES)


def kernel(x_prompt, x_sample, state_rglru_h, state_rglru_conv, state_ssd_h, state_ssd_conv, state_ffn_conv, g_mix, w_in, rg_conv_w, rg_conv_b, rg_gate_a_w, rg_gate_a_b, rg_gate_i_w, rg_gate_i_b, rg_lambda, g_rg_out, ssd_conv_w, ssd_conv_b, ssd_dt_bias, ssd_A_log, ssd_D, g_ssd_norm, w_out, g_ffn, w_ffn_up, ffn_conv_w, ffn_conv_b, w_ffn_down, g_final):
    depth = g_mix.shape[0]
    pb, seq, d_model = x_prompt.shape
    sb, dec_seq, _ = x_sample.shape
    d_rnn = rg_lambda.shape[1]
    d_ssm = g_ssd_norm.shape[1]
    d_xbc = ssd_conv_w.shape[2]
    d_ff = ffn_conv_w.shape[2]
    heads = ssd_A_log.shape[1]
    headdim = d_ssm // heads
    n_state = state_ssd_h.shape[-1]
    n_main = 2 * d_rnn + d_ssm + d_xbc
    k_rg = rg_conv_w.shape[1]
    k_ssd = ssd_conv_w.shape[1]
    k_ffn = ffn_conv_w.shape[1]
    gn_w = SSD_GROUPS * n_state
    assert heads <= LANES and n_state == LANES and k_ffn == 3 and dec_seq >= max(k_rg, k_ssd) - 1

    hid = lax.broadcasted_iota(jnp.int32, (LANES, d_ssm), 0)
    cid = lax.broadcasted_iota(jnp.int32, (LANES, d_ssm), 1)
    e_sel = (hid == cid // headdim).astype(BF16)
    gid = lax.broadcasted_iota(jnp.int32, (gn_w, d_ssm), 0) // n_state
    cgid = lax.broadcasted_iota(jnp.int32, (gn_w, d_ssm), 1) // (d_ssm // SSD_GROUPS)
    g_sel = (gid == cgid).astype(BF16)

    yp = x_prompt.reshape(pb * seq, d_model)
    ys = jnp.transpose(x_sample, (1, 0, 2)).reshape(dec_seq * sb, d_model)
    p_new = [[], [], [], [], []]
    s_new = [[], [], [], [], []]
    row = lambda v: v.reshape(1, -1)
    swap = lambda a3: jnp.transpose(a3, (1, 0, 2))
    bb = 32

    for l in range(depth):
        wdt_b = jnp.pad(w_in[l][:, n_main:], ((0, 0), (0, LANES - heads))).astype(BF16)
        wa_b = rg_gate_a_w[l].astype(BF16)
        wi_b = rg_gate_i_w[l].astype(BF16)
        rg_args = (rg_conv_w[l], row(rg_conv_b[l]), wa_b, row(rg_gate_a_b[l]), wi_b,
                   row(rg_gate_i_b[l]), row(rg_lambda[l]), row(g_rg_out[l]))
        dtb = _pad_lanes(ssd_dt_bias[l])
        alog = _pad_lanes(ssd_A_log[l])
        dexp = jnp.repeat(ssd_D[l].astype(F32), headdim).reshape(1, d_ssm)
        gn = row(g_ssd_norm[l])

        w_in_b = w_in[l].astype(BF16)
        ms = ys.shape[0]
        proj_s, dt_s = _in_proj(ys, row(g_mix[l]), w_in_b, wdt_b, n_main, tm=ms, tn=TN_IN)

        proj_p, dt_p = _in_proj(yp, row(g_mix[l]), w_in_b, wdt_b, n_main, tm=TM, tn=TN_IN)
        rg_steps = pb * (seq // TT_RG)
        ssd_steps = pb * (seq // SSD_CHUNK)
        rb_ssd = d_model // (2 * rg_steps + ssd_steps)
        assert rb_ssd % 16 == 0 and (2 * rg_steps + ssd_steps) * rb_ssd == d_model
        rg_p, hl_p, w_up_ta = _rglru_prompt(proj_p, pb, seq, d_rnn, *rg_args, w_ffn_up[l], TN_UP,
                                            2 * rb_ssd, tt=TT_RG)
        ssd_p, hfin_p, w_up_tb, w_out_b = _ssd_prompt(
            proj_p, dt_p, pb, seq, d_ssm, d_xbc, n_state, headdim, ssd_conv_w[l], row(ssd_conv_b[l]),
            dtb, alog, dexp, gn, e_sel, w_ffn_up[l], TN_UP, rb_ssd, 2 * rb_ssd * rg_steps, w_out[l])
        x1_p, h2_p, ss_p = _out_proj(rg_p, ssd_p, w_out_b, yp, row(g_ffn[l]), tm=TM, tn=TN_OUT)
        act_p, gst_p, w_down_t = _ffn_up_prompt(h2_p, ss_p, w_up_ta, w_up_tb, ffn_conv_w[l],
                                                row(ffn_conv_b[l]), seq, w_ffn_down[l], TN_DOWN, tm=TM)
        proj_p3 = proj_p.reshape(pb, seq, n_main)
        p_new[0].append(hl_p[:, SUBLANES - 1, :])
        p_new[1].append(proj_p3[:, seq - (k_rg - 1):, :d_rnn])
        p_new[2].append(hfin_p.reshape(pb, heads, headdim, n_state))
        p_new[3].append(proj_p3[:, seq - (k_ssd - 1):, n_main - d_xbc:])
        tiles = seq // TM
        p_new[4].append(gst_p[tiles - 1::tiles, SUBLANES - (k_ffn - 1):, :])

        proj_s3 = proj_s.reshape(dec_seq, sb, n_main)
        rg_s, hl_s = _rglru_sample(proj_s3, swap(state_rglru_conv[l]), state_rglru_h[l], d_rnn,
                                   *rg_args, bb=bb)
        ydp, ecum, xdd, bc, cd = _ssd_sample_pre(
            proj_s3, swap(state_ssd_conv[l]), dt_s.reshape(dec_seq, sb, LANES), d_ssm, d_xbc, n_state,
            ssd_conv_w[l], row(ssd_conv_b[l]), dtb, alog, dexp, e_sel, g_sel, bb=bb)
        bc_b = swap(bc)
        snew, yoff_b = _ssd_sample_state(state_ssd_h[l], bc_b[:, :, gn_w:], bc_b[:, :, :gn_w],
                                         swap(xdd), cd, bt=8)
        ssd_s = _ssd_sample_post(ydp, swap(yoff_b), ecum, proj_s3, gn, d_ssm, d_xbc, bb=bb)
        x1_s, h2_s, ss_s = _out_proj(rg_s.reshape(ms, d_rnn), ssd_s.reshape(ms, d_ssm), w_out_b, ys,
                                     row(g_ffn[l]), tm=ms, tn=TN_OUT)
        act_s, gst_s = _ffn_up_sample(h2_s, ss_s, w_up_ta, w_up_tb, ffn_conv_w[l], row(ffn_conv_b[l]),
                                      swap(state_ffn_conv[l]).reshape(-1, d_ff), sb)
        s_new[0].append(hl_s)
        s_new[1].append(swap(proj_s3[dec_seq - (k_rg - 1):, :, :d_rnn]))
        s_new[2].append(snew)
        s_new[3].append(swap(proj_s3[dec_seq - (k_ssd - 1):, :, n_main - d_xbc:]))
        s_new[4].append(swap(gst_s.reshape(k_ffn - 1, sb, d_ff)))

        last = l == depth - 1
        yp = _ffn_down(act_p, w_down_t, x1_p, row(g_final), last, tm=TM, tk=TK_DOWN)
        ys = _ffn_down(act_s, w_down_t, x1_s, row(g_final), last, tm=ms, tk=TK_DOWN)

    y_prompt = yp.reshape(pb, seq, d_model)
    y_sample = jnp.transpose(ys.reshape(dec_seq, sb, d_model), (1, 0, 2))
    return (y_prompt, y_sample,
            jnp.stack(p_new[0]), jnp.stack(p_new[1]), jnp.stack(p_new[2]), jnp.stack(p_new[3]),
            jnp.stack(p_new[4]),
            jnp.stack(s_new[0]), jnp.stack(s_new[1]), jnp.stack(s_new[2]), jnp.stack(s_new[3]),
            jnp.stack(s_new[4]))
```

```python
import functools

import jax
import jax.numpy as jnp
from jax import lax
from jax.experimental import pallas as pl
from jax.experimental.pallas import tpu as pltpu

F32 = jnp.float32
BF16 = jnp.bfloat16

EPS = 1e-6
RG_C = 8.0
SSD_CHUNK = 128
SSD_GROUPS = 4
LANES = 128
SUBLANES = 8
VMEM_LIMIT = 56 * 2**20

TM = 1024
ROW_CHUNK = 256
TT_RG = 256
TN_IN = 512
TN_OUT = 512
TN_UP = 512
TN_DOWN = 512
TK_DOWN = 4096


def _params(n_axes):
    return pltpu.CompilerParams(dimension_semantics=("arbitrary",) * n_axes,
                                vmem_limit_bytes=VMEM_LIMIT)


def _dot(a, b):
    return jnp.dot(a, b, preferred_element_type=F32)


def _rms_rows(x, g):
    ms = jnp.mean(x * x, axis=-1, keepdims=True)
    return (x * lax.rsqrt(ms + EPS)) * g


def _split3(x):
    hi = x.astype(BF16)
    r = x - hi.astype(F32)
    mid = r.astype(BF16)
    lo = (r - mid.astype(F32)).astype(BF16)
    return hi, mid, lo


def _dot_sel_right(x, sel):
    hi, mid, lo = _split3(x)
    return (_dot(hi, sel) + _dot(mid, sel)) + _dot(lo, sel)


def _dot_sel_left(sel, x):
    hi, mid, lo = _split3(x)
    return (_dot(sel, hi) + _dot(sel, mid)) + _dot(sel, lo)


def _inproj_kernel(x_hbm, g_ref, w_ref, wdt_ref, o_ref, dt_ref, x_buf, hn_ref, sem, *, ni, tm):
    i = pl.program_id(0)
    j = pl.program_id(1)

    def x_in(tile):
        return pltpu.make_async_copy(x_hbm.at[pl.ds(tile * tm, tm)], x_buf, sem)

    @pl.when(j == 0)
    def _():
        @pl.when(i == 0)
        def _():
            x_in(0).start()

        x_in(i).wait()
        for r in range(0, tm, ROW_CHUNK):
            hn_ref[r:r + ROW_CHUNK, :] = _rms_rows(x_buf[r:r + ROW_CHUNK, :], g_ref[...]).astype(BF16)
        dt_ref[...] = _dot(hn_ref[...], wdt_ref[...])

        @pl.when(i + 1 < ni)
        def _():
            x_in(i + 1).start()

    o_ref[...] = _dot(hn_ref[...], w_ref[...])


def _cast_col_tiles(src_ref, dst_ref):
    n_tiles, _, tn = dst_ref.shape
    for t in range(n_tiles):
        dst_ref[t] = src_ref[:, t * tn:(t + 1) * tn].astype(BF16)


def _in_proj(x2d, g, w_b, wdt_b, n_main, *, tm, tn):
    m, d = x2d.shape
    ni = m // tm
    return pl.pallas_call(
        functools.partial(_inproj_kernel, ni=ni, tm=tm),
        grid=(ni, n_main // tn),
        in_specs=[pl.BlockSpec(memory_space=pl.ANY),
                  pl.BlockSpec((1, d), lambda i, j: (0, 0)),
                  pl.BlockSpec((d, tn), lambda i, j: (0, j)),
                  pl.BlockSpec((d, LANES), lambda i, j: (0, 0))],
        out_specs=[pl.BlockSpec((tm, tn), lambda i, j: (i, j)),
                   pl.BlockSpec((tm, LANES), lambda i, j: (i, 0))],
        out_shape=[jax.ShapeDtypeStruct((m, n_main), F32), jax.ShapeDtypeStruct((m, LANES), F32)],
        scratch_shapes=[pltpu.VMEM((tm, d), F32), pltpu.VMEM((tm, d), BF16), pltpu.SemaphoreType.DMA(())],
        compiler_params=_params(2),
        name="in_proj",
    )(x2d, g, w_b, wdt_b)


def _outproj_kernel(rg_ref, ssd_ref, wt_ref, wb_ref, x_ref, g_ref, x1_ref, xg_ref, ss_ref):
    j = pl.program_id(1)
    x1 = x_ref[...] + (_dot(rg_ref[...], wt_ref[...]) + _dot(ssd_ref[...], wb_ref[...]))
    x1_ref[...] = x1
    xg_ref[...] = (x1 * g_ref[...]).astype(BF16)
    part = jnp.broadcast_to(jnp.sum(x1 * x1, axis=-1, keepdims=True), ss_ref.shape)

    @pl.when(j == 0)
    def _():
        ss_ref[...] = part

    @pl.when(j > 0)
    def _():
        ss_ref[...] += part


def _out_proj(rg, ssd, w_b, x2d, g, *, tm, tn):
    m, d = x2d.shape
    k_half = rg.shape[1]
    return pl.pallas_call(
        _outproj_kernel,
        grid=(m // tm, d // tn),
        in_specs=[pl.BlockSpec((tm, k_half), lambda i, j: (i, 0)),
                  pl.BlockSpec((tm, k_half), lambda i, j: (i, 0)),
                  pl.BlockSpec((k_half, tn), lambda i, j: (0, j)),
                  pl.BlockSpec((k_half, tn), lambda i, j: (1, j)),
                  pl.BlockSpec((tm, tn), lambda i, j: (i, j)),
                  pl.BlockSpec((1, tn), lambda i, j: (0, j))],
        out_specs=[pl.BlockSpec((tm, tn), lambda i, j: (i, j)),
                   pl.BlockSpec((tm, tn), lambda i, j: (i, j)),
                   pl.BlockSpec((tm, LANES), lambda i, j: (i, 0))],
        out_shape=[jax.ShapeDtypeStruct((m, d), F32),
                   jax.ShapeDtypeStruct((m, d), BF16),
                   jax.ShapeDtypeStruct((m, LANES), F32)],
        compiler_params=_params(2),
        name="out_proj",
    )(rg, ssd, w_b, w_b, x2d, g)


def _glu(gate_c, val):
    return (jax.nn.gelu(gate_c) * val).astype(BF16)


def _dot_split_k(h_ref, wa_ref, wb_ref):
    ka = wa_ref.shape[0]
    return _dot(h_ref[:, :ka], wa_ref[...]) + _dot(h_ref[:, ka:], wb_ref[...])


def _inv_rms(ss_ref, width):
    return lax.rsqrt(ss_ref[:, 0:1] / width + EPS)


def _ffn_up_prompt_kernel(xg_ref, ss_ref, wga_ref, wgb_ref, wva_ref, wvb_ref, cw_ref, cb_ref, wsrc_ref,
                          act_ref, gst_ref, wdst_ref, carry_ref, *, tiles_per_seq):
    i = pl.program_id(0)
    j = pl.program_id(1)
    _cast_col_tiles(wsrc_ref, wdst_ref)
    inv = _inv_rms(ss_ref, xg_ref.shape[1])
    gate = _dot_split_k(xg_ref, wga_ref, wgb_ref) * inv
    val = _dot_split_k(xg_ref, wva_ref, wvb_ref) * inv
    tm, tn = gate.shape
    w0, w1, w2 = cw_ref[0:1, :], cw_ref[1:2, :], cw_ref[2:3, :]
    cb = cb_ref[...]
    g1 = pltpu.roll(gate, 1, 0)
    g2 = pltpu.roll(gate, 2, 0)
    act_ref[...] = _glu(((cb + g2 * w0) + g1 * w1) + gate * w2, val)
    @pl.when((i % tiles_per_seq) == 0)
    def _():
        carry_ref[j] = jnp.zeros((SUBLANES, tn), F32)

    prev = carry_ref[j]
    rows = lax.broadcasted_iota(jnp.int32, (SUBLANES, tn), 0)
    t1 = jnp.where(rows < 1, pltpu.roll(prev, 1, 0), g1[0:SUBLANES])
    t2 = jnp.where(rows < 2, pltpu.roll(prev, 2, 0), g2[0:SUBLANES])
    act_ref[0:SUBLANES, :] = _glu(((cb + t2 * w0) + t1 * w1) + gate[0:SUBLANES] * w2,
                                  val[0:SUBLANES])
    last = gate[tm - SUBLANES:tm]
    carry_ref[j] = last
    gst_ref[...] = last


def _ffn_up_prompt(h2, ss, w_ta, w_tb, cw, cb, seq, w_next, tn_next, *, tm):
    m, d = h2.shape
    nj, ka, tn = w_ta.shape[0] // 2, w_ta.shape[1], w_ta.shape[2]
    kb = w_tb.shape[1]
    assert ka + kb == d and ka % LANES == 0
    d_ff = nj * tn
    steps = (m // tm) * nj
    k_next, n_next = w_next.shape
    rb = k_next // steps
    assert rb * steps == k_next and rb % 16 == 0 and n_next % tn_next == 0
    return pl.pallas_call(
        functools.partial(_ffn_up_prompt_kernel, tiles_per_seq=seq // tm),
        grid=(m // tm, nj),
        in_specs=[pl.BlockSpec((tm, d), lambda i, j: (i, 0)),
                  pl.BlockSpec((tm, LANES), lambda i, j: (i, 0)),
                  pl.BlockSpec((None, ka, tn), lambda i, j: (j, 0, 0)),
                  pl.BlockSpec((None, kb, tn), lambda i, j: (j, 0, 0)),
                  pl.BlockSpec((None, ka, tn), lambda i, j: (j + nj, 0, 0)),
                  pl.BlockSpec((None, kb, tn), lambda i, j: (j + nj, 0, 0)),
                  pl.BlockSpec((3, tn), lambda i, j: (0, j)),
                  pl.BlockSpec((1, tn), lambda i, j: (0, j)),
                  pl.BlockSpec((rb, n_next), lambda i, j: (i * nj + j, 0))],
        out_specs=[pl.BlockSpec((tm, tn), lambda i, j: (i, j)),
                   pl.BlockSpec((None, SUBLANES, tn), lambda i, j: (i, 0, j)),
                   pl.BlockSpec((n_next // tn_next, rb, tn_next), lambda i, j: (0, i * nj + j, 0))],
        out_shape=[jax.ShapeDtypeStruct((m, d_ff), BF16),
                   jax.ShapeDtypeStruct((m // tm, SUBLANES, d_ff), F32),
                   jax.ShapeDtypeStruct((n_next // tn_next, k_next, tn_next), BF16)],
        scratch_shapes=[pltpu.VMEM((nj, SUBLANES, tn), F32)],
        compiler_params=_params(2),
        name="ffn_up_prompt",
    )(h2, ss, w_ta, w_tb, w_ta, w_tb, cw, cb, w_next)


def _ffn_up_sample_kernel(xg_ref, ss_ref, wga_ref, wgb_ref, wva_ref, wvb_ref, cw_ref, cb_ref, st_ref,
                          act_ref, nst_ref, *, nb, nt):
    inv = _inv_rms(ss_ref, xg_ref.shape[1])
    gate = _dot_split_k(xg_ref, wga_ref, wgb_ref) * inv
    val = _dot_split_k(xg_ref, wva_ref, wvb_ref) * inv
    w0, w1, w2 = cw_ref[0:1, :], cw_ref[1:2, :], cw_ref[2:3, :]
    cb = cb_ref[...]
    ext = [st_ref[0:nb, :], st_ref[nb:2 * nb, :]] + [gate[t * nb:(t + 1) * nb] for t in range(nt)]
    for t in range(nt):
        gc = ((cb + ext[t] * w0) + ext[t + 1] * w1) + ext[t + 2] * w2
        act_ref[t * nb:(t + 1) * nb, :] = _glu(gc, val[t * nb:(t + 1) * nb])
    nst_ref[0:nb, :] = ext[nt]
    nst_ref[nb:2 * nb, :] = ext[nt + 1]


def _ffn_up_sample(h2, ss, w_ta, w_tb, cw, cb, st, nb):
    m, d = h2.shape
    nj, ka, tn = w_ta.shape[0] // 2, w_ta.shape[1], w_ta.shape[2]
    kb = w_tb.shape[1]
    d_ff = nj * tn
    return pl.pallas_call(
        functools.partial(_ffn_up_sample_kernel, nb=nb, nt=m // nb),
        grid=(nj,),
        in_specs=[pl.BlockSpec((m, d), lambda j: (0, 0)),
                  pl.BlockSpec((m, LANES), lambda j: (0, 0)),
                  pl.BlockSpec((None, ka, tn), lambda j: (j, 0, 0)),
                  pl.BlockSpec((None, kb, tn), lambda j: (j, 0, 0)),
                  pl.BlockSpec((None, ka, tn), lambda j: (j + nj, 0, 0)),
                  pl.BlockSpec((None, kb, tn), lambda j: (j + nj, 0, 0)),
                  pl.BlockSpec((3, tn), lambda j: (0, j)),
                  pl.BlockSpec((1, tn), lambda j: (0, j)),
                  pl.BlockSpec((2 * nb, tn), lambda j: (0, j))],
        out_specs=[pl.BlockSpec((m, tn), lambda j: (0, j)),
                   pl.BlockSpec((2 * nb, tn), lambda j: (0, j))],
        out_shape=[jax.ShapeDtypeStruct((m, d_ff), BF16),
                   jax.ShapeDtypeStruct((2 * nb, d_ff), F32)],
        compiler_params=_params(1),
        name="ffn_up_sample",
    )(h2, ss, w_ta, w_tb, w_ta, w_tb, cw, cb, st)


def _ffn_down_kernel(act_ref, w_ref, x1_ref, g_ref, y_hbm, rows_ref, sem,
                     *, ni, nk, nj, tm, tn, final_norm):
    i = pl.program_id(0)
    k = pl.program_id(1)
    j = pl.program_id(2)
    cols = pl.ds(pl.multiple_of(j * tn, tn), tn)

    def rows_out(tile):
        return pltpu.make_async_copy(rows_ref, y_hbm.at[pl.ds(tile * tm, tm)], sem)

    @pl.when(k == 0)
    def _():
        @pl.when((j == 0) & (i > 0))
        def _():
            rows_out(i - 1).wait()

        rows_ref[:, cols] = x1_ref[...]

    rows_ref[:, cols] += _dot(act_ref[...], w_ref[...])

    @pl.when((k == nk - 1) & (j == nj - 1))
    def _():
        if final_norm:
            for r in range(0, tm, ROW_CHUNK):
                rows_ref[r:r + ROW_CHUNK, :] = _rms_rows(rows_ref[r:r + ROW_CHUNK, :], g_ref[...])
        rows_out(i).start()

        @pl.when(i == ni - 1)
        def _():
            rows_out(i).wait()


def _ffn_down(act, w_t, x1, g, final_norm, *, tm, tk):
    m, d = x1.shape
    nj, d_ff, tn = w_t.shape
    ni, nk = m // tm, d_ff // tk
    return pl.pallas_call(
        functools.partial(_ffn_down_kernel, ni=ni, nk=nk, nj=nj, tm=tm, tn=tn, final_norm=final_norm),
        grid=(ni, nk, nj),
        in_specs=[pl.BlockSpec((tm, tk), lambda i, k, j: (i, k)),
                  pl.BlockSpec((None, tk, tn), lambda i, k, j: (j, k, 0)),
                  pl.BlockSpec((tm, tn), lambda i, k, j: (i, jnp.where(k == 0, j, nj - 1))),
                  pl.BlockSpec((1, d), lambda i, k, j: (0, 0))],
        out_specs=pl.BlockSpec(memory_space=pl.ANY),
        out_shape=jax.ShapeDtypeStruct((m, d), F32),
        scratch_shapes=[pltpu.VMEM((tm, d), F32), pltpu.SemaphoreType.DMA(())],
        compiler_params=_params(3),
        name="ffn_down",
    )(act, w_t, x1, g)


def _softplus(x):
    return jnp.maximum(x, 0.0) + jnp.log1p(jnp.exp(-jnp.abs(x)))


def _rglru_coeffs(xc, wa_ref, ba_ref, wi_ref, bi_ref, lam_ref):
    xb = xc.astype(BF16)
    heads, blk = wa_ref.shape[0], wa_ref.shape[1]
    ga = jnp.concatenate([_dot(xb[:, h * blk:(h + 1) * blk], wa_ref[h]) for h in range(heads)], axis=1)
    gi = jnp.concatenate([_dot(xb[:, h * blk:(h + 1) * blk], wi_ref[h]) for h in range(heads)], axis=1)
    gate_r = jax.nn.sigmoid(ga + ba_ref[...])
    gate_i = jax.nn.sigmoid(gi + bi_ref[...])
    log_a = (-RG_C * gate_r) * _softplus(-lam_ref[...])
    a = jnp.exp(log_a)
    bx = jnp.sqrt(-jnp.tanh(log_a) * (a * a + 1.0)) * (gate_i * xc)
    return a, bx


def _rg_out(h, gate, g_ref):
    return _rms_rows(h * jax.nn.gelu(gate), g_ref[...]).astype(BF16)


def _conv_tile(x, prev, w_ref, b_ref, k_taps):
    t_len, width = x.shape
    groups = t_len // SUBLANES
    x3 = x.reshape(groups, SUBLANES, width)
    pos = lax.broadcasted_iota(jnp.int32, x3.shape, 1)
    out = b_ref[...].reshape(1, 1, width)
    for k in range(k_taps):
        s = k_taps - 1 - k
        wk = w_ref[k:k + 1, :].reshape(1, 1, width)
        if s == 0:
            shifted = x3
        else:
            rot = pltpu.roll(x3, s, 1)
            rot_prev = jnp.concatenate([pltpu.roll(prev, s, 0)[None], rot[:groups - 1]], axis=0)
            shifted = jnp.where(pos >= s, rot, rot_prev)
        out = out + shifted * wk
    return out.reshape(t_len, width)


def _rglru_prompt_kernel(x_ref, gate_ref, cw_ref, cb_ref, wa_ref, ba_ref, wi_ref, bi_ref, lam_ref,
                         g_ref, wsrc_ref, out_ref, hlast_ref, wdst_ref, xprev_ref, hc_ref, a_s, b_s, h_s,
                         *, k_taps):
    c = pl.program_id(1)
    _cast_col_tiles(wsrc_ref, wdst_ref)

    @pl.when(c == 0)
    def _():
        xprev_ref[...] = jnp.zeros_like(xprev_ref)
        hc_ref[...] = jnp.zeros_like(hc_ref)

    x = x_ref[...]
    t_len, width = x.shape
    xc = _conv_tile(x, xprev_ref[...], cw_ref, cb_ref, k_taps)
    xprev_ref[...] = x[t_len - SUBLANES:t_len]
    a, b = _rglru_coeffs(xc, wa_ref, ba_ref, wi_ref, bi_ref, lam_ref)
    groups = t_len // SUBLANES
    a = a.reshape(groups, SUBLANES, width)
    b = b.reshape(groups, SUBLANES, width)
    pos = lax.broadcasted_iota(jnp.int32, a.shape, 1)
    for s in (1, 2, 4):
        a_sh = pltpu.roll(a, s, 1)
        b_sh = pltpu.roll(b, s, 1)
        m = pos >= s
        b = jnp.where(m, a * b_sh + b, b)
        a = jnp.where(m, a * a_sh, a)
    a_s[...] = a.reshape(t_len, width)
    b_s[...] = b.reshape(t_len, width)
    h = hc_ref[...]
    for grp in range(t_len // SUBLANES):
        lo, hi = grp * SUBLANES, (grp + 1) * SUBLANES
        hl = jnp.broadcast_to(h[SUBLANES - 1:SUBLANES, :], (SUBLANES, width))
        h = a_s[lo:hi, :] * hl + b_s[lo:hi, :]
        h_s[lo:hi, :] = h
    hc_ref[...] = h
    hlast_ref[...] = h
    out_ref[...] = _rg_out(h_s[...], gate_ref[...], g_ref)


def _rglru_prompt(proj, nb, seq, d_rnn, cw, cb, wa, ba, wi, bi, lam, g, w_next, tn_next, rb, *, tt):
    nc = seq // tt
    heads, blk = wa.shape[0], wa.shape[1]
    k_next, n_next = w_next.shape
    vec = pl.BlockSpec((1, d_rnn), lambda b, c: (0, 0))
    gate_w = pl.BlockSpec((heads, blk, blk), lambda b, c: (0, 0, 0))
    return pl.pallas_call(
        functools.partial(_rglru_prompt_kernel, k_taps=cw.shape[0]),
        grid=(nb, nc),
        in_specs=[pl.BlockSpec((tt, d_rnn), lambda b, c: (b * nc + c, 0)),
                  pl.BlockSpec((tt, d_rnn), lambda b, c: (b * nc + c, 1)),
                  pl.BlockSpec((cw.shape[0], d_rnn), lambda b, c: (0, 0)),
                  vec, gate_w, vec, gate_w, vec, vec, vec,
                  pl.BlockSpec((rb, n_next), lambda b, c: (b * nc + c, 0))],
        out_specs=[pl.BlockSpec((tt, d_rnn), lambda b, c: (b * nc + c, 0)),
                   pl.BlockSpec((None, SUBLANES, d_rnn), lambda b, c: (b, 0, 0)),
                   pl.BlockSpec((n_next // tn_next, rb, tn_next), lambda b, c: (0, b * nc + c, 0))],
        out_shape=[jax.ShapeDtypeStruct((nb * seq, d_rnn), BF16),
                   jax.ShapeDtypeStruct((nb, SUBLANES, d_rnn), F32),
                   jax.ShapeDtypeStruct((n_next // tn_next, nb * nc * rb, tn_next), BF16)],
        scratch_shapes=[pltpu.VMEM((SUBLANES, d_rnn), F32), pltpu.VMEM((SUBLANES, d_rnn), F32),
                        pltpu.VMEM((tt, d_rnn), F32), pltpu.VMEM((tt, d_rnn), F32),
                        pltpu.VMEM((tt, d_rnn), F32)],
        compiler_params=_params(2),
        name="rglru_prompt",
    )(proj, proj, cw, cb, wa, ba, wi, bi, lam, g, w_next)


def _conv_slabs(st_ref, x_ref, w_ref, b_ref):
    k_taps = w_ref.shape[0]
    ext = [st_ref[s] for s in range(k_taps - 1)] + [x_ref[t] for t in range(x_ref.shape[0])]
    convs = []
    for t in range(x_ref.shape[0]):
        acc = b_ref[...]
        for k in range(k_taps):
            acc = acc + ext[t + k] * w_ref[k:k + 1, :]
        convs.append(acc)
    return jnp.concatenate(convs, axis=0)


def _rglru_sample_kernel(x_ref, gate_ref, st_ref, h0_ref, cw_ref, cb_ref, wa_ref, ba_ref, wi_ref,
                         bi_ref, lam_ref, g_ref, out_ref, hlast_ref, h_s):
    nt, bb, width = x_ref.shape
    xc = _conv_slabs(st_ref, x_ref, cw_ref, cb_ref)
    a, b = _rglru_coeffs(xc, wa_ref, ba_ref, wi_ref, bi_ref, lam_ref)
    h = h0_ref[...]
    for t in range(nt):
        h = a[t * bb:(t + 1) * bb] * h + b[t * bb:(t + 1) * bb]
        h_s[t * bb:(t + 1) * bb, :] = h
    hlast_ref[...] = h
    gate = gate_ref[...].reshape(nt * bb, width)
    out_ref[...] = _rg_out(h_s[...], gate, g_ref).reshape(nt, bb, width)


def _rglru_sample(proj3, st3, h0, d_rnn, cw, cb, wa, ba, wi, bi, lam, g, *, bb):
    nt, nb, _ = proj3.shape
    heads, blk = wa.shape[0], wa.shape[1]
    k_taps = cw.shape[0]
    vec = pl.BlockSpec((1, d_rnn), lambda i: (0, 0))
    gate_w = pl.BlockSpec((heads, blk, blk), lambda i: (0, 0, 0))
    return pl.pallas_call(
        _rglru_sample_kernel,
        grid=(nb // bb,),
        in_specs=[pl.BlockSpec((nt, bb, d_rnn), lambda i: (0, i, 0)),
                  pl.BlockSpec((nt, bb, d_rnn), lambda i: (0, i, 1)),
                  pl.BlockSpec((k_taps - 1, bb, d_rnn), lambda i: (0, i, 0)),
                  pl.BlockSpec((bb, d_rnn), lambda i: (i, 0)),
                  pl.BlockSpec((k_taps, d_rnn), lambda i: (0, 0)),
                  vec, gate_w, vec, gate_w, vec, vec, vec],
        out_specs=[pl.BlockSpec((nt, bb, d_rnn), lambda i: (0, i, 0)),
                   pl.BlockSpec((bb, d_rnn), lambda i: (i, 0))],
        out_shape=[jax.ShapeDtypeStruct((nt, nb, d_rnn), BF16),
                   jax.ShapeDtypeStruct((nb, d_rnn), F32)],
        scratch_shapes=[pltpu.VMEM((nt * bb, d_rnn), F32)],
        compiler_params=_params(1),
        name="rglru_sample",
    )(proj3, proj3, st3, h0, cw, cb, wa, ba, wi, bi, lam, g)


def _silu(x):
    return x * jax.nn.sigmoid(x)


def _group_norm_out(y, z, gn_ref, groups):
    u = y * _silu(z)
    gw = u.shape[1] // groups
    outs = []
    for g in range(groups):
        ug = u[:, g * gw:(g + 1) * gw]
        ms = jnp.mean(ug * ug, axis=-1, keepdims=True)
        outs.append((ug * lax.rsqrt(ms + EPS)) * gn_ref[:, g * gw:(g + 1) * gw])
    return jnp.concatenate(outs, axis=1).astype(BF16)


def _ssd_prompt_kernel(z_ref, xbc_ref, dt_ref, cw_ref, cb_ref, dtb_ref, alog_ref, dexp_ref, gn_ref,
                       e_ref, wsrc_ref, wsrc2_ref, out_ref, hfin_ref, wdst_ref, wdst2_ref,
                       carry_ref, s_ref, *, k_taps, d_ssm, n_state, groups, headdim, nc):
    c = pl.program_id(1)
    _cast_col_tiles(wsrc_ref, wdst_ref)
    wdst2_ref[...] = wsrc2_ref[...].astype(BF16)

    @pl.when(c == 0)
    def _():
        carry_ref[...] = jnp.zeros_like(carry_ref)
        s_ref[...] = jnp.zeros_like(s_ref)

    xbc = xbc_ref[...]
    q = xbc.shape[0]
    xc = _silu(_conv_tile(xbc, carry_ref[...], cw_ref, cb_ref, k_taps))
    carry_ref[...] = xbc[q - SUBLANES:q]
    gn_w = groups * n_state
    bm = xc[:, d_ssm:d_ssm + gn_w]
    cm = xc[:, d_ssm + gn_w:d_ssm + 2 * gn_w]

    dt = _softplus(dt_ref[...] + dtb_ref[...])
    dta = dt * (-jnp.exp(alog_ref[...]))
    row = lax.broadcasted_iota(jnp.int32, (q, q), 0)
    col = lax.broadcasted_iota(jnp.int32, (q, q), 1)
    causal = row >= col
    tril = jnp.where(causal, 1.0, 0.0).astype(BF16)
    cum = _dot_sel_left(tril, dta)
    cum_t = cum.T

    heads_per_group = d_ssm // headdim // groups
    pair_w = 2 * headdim
    gw = heads_per_group * headdim
    lane = lax.broadcasted_iota(jnp.int32, (q, pair_w), 1)
    for g in range(groups):
        gs = slice(g * gw, (g + 1) * gw)
        xs = xc[:, gs]
        e = e_ref[:, gs]
        dt_e = _dot_sel_right(dt, e)
        cum_e = _dot_sel_right(cum, e)
        xdt = xs * dt_e
        cum_last = cum_e[q - 1:q, :]
        xdd = (xdt * jnp.exp(cum_last - cum_e)).astype(BF16)
        cg = cm[:, g * n_state:(g + 1) * n_state].astype(BF16)
        bg32 = bm[:, g * n_state:(g + 1) * n_state]
        bg = bg32.astype(BF16)
        cb_g = lax.dot_general(cg, bg, (((1,), (1,)), ((), ())), preferred_element_type=F32)
        s_old = s_ref[:, gs]
        y_state = _dot(cg, s_old.astype(BF16)) * jnp.exp(cum_e)
        s_ref[:, gs] = s_old * jnp.exp(cum_last) + _dot(bg32.T.astype(BF16), xdd)
        y_pairs = []
        for pr in range(heads_per_group // 2):
            h0 = g * heads_per_group + 2 * pr
            ms = []
            for h in (h0, h0 + 1):
                seg = (jnp.broadcast_to(cum[:, h:h + 1], (q, q))
                       - jnp.broadcast_to(cum_t[h:h + 1, :], (q, q)))
                ms.append(cb_g * jnp.exp(jnp.where(causal, seg, -jnp.inf)))
            xp = xdt[:, 2 * pr * headdim:2 * pr * headdim + pair_w]
            lhs = jnp.concatenate(ms, axis=1).astype(BF16)
            rhs = jnp.concatenate([jnp.where(lane < headdim, xp, 0.0),
                                   jnp.where(lane >= headdim, xp, 0.0)], axis=0).astype(BF16)
            y_pairs.append(_dot(lhs, rhs))
        y = (y_state + jnp.concatenate(y_pairs, axis=1)) + dexp_ref[:, gs] * xs
        u = y * _silu(z_ref[:, gs])
        ms_g = jnp.mean(u * u, axis=-1, keepdims=True)
        out_ref[:, gs] = ((u * lax.rsqrt(ms_g + EPS)) * gn_ref[:, gs]).astype(BF16)

    @pl.when(c == nc - 1)
    def _():
        for blk in range(d_ssm // LANES):
            hfin_ref[blk * LANES:(blk + 1) * LANES, :] = s_ref[:, blk * LANES:(blk + 1) * LANES].T


def _ssd_prompt(proj, dt_raw, nb, seq, d_ssm, d_xbc, n_state, headdim, cw, cb, dtb, alog, dexp, gn, e,
                w_next, tn_next, rb, row0, w_plain):
    q = SSD_CHUNK
    nc = seq // q
    k_plain, n_plain = w_plain.shape
    rb2 = k_plain // (nb * nc)
    assert rb2 * nb * nc == k_plain and rb2 % 16 == 0
    z_blk = (proj.shape[1] - d_xbc - d_ssm) // d_ssm
    x_blk = (proj.shape[1] - d_xbc) // d_xbc
    assert z_blk * d_ssm + d_ssm + d_xbc == proj.shape[1] and x_blk * d_xbc + d_xbc == proj.shape[1]
    k_next, n_next = w_next.shape
    n_tiles = n_next // tn_next
    blk0 = row0 // rb
    assert blk0 * rb == row0 and row0 + nb * nc * rb == k_next
    one = lambda w: pl.BlockSpec((1, w), lambda b, c: (0, 0))
    return pl.pallas_call(
        functools.partial(_ssd_prompt_kernel, k_taps=cw.shape[0], d_ssm=d_ssm, n_state=n_state,
                          groups=SSD_GROUPS, headdim=headdim, nc=nc),
        grid=(nb, nc),
        in_specs=[pl.BlockSpec((q, d_ssm), lambda b, c: (b * nc + c, z_blk)),
                  pl.BlockSpec((q, d_xbc), lambda b, c: (b * nc + c, x_blk)),
                  pl.BlockSpec((q, LANES), lambda b, c: (b * nc + c, 0)),
                  pl.BlockSpec((cw.shape[0], d_xbc), lambda b, c: (0, 0)),
                  one(d_xbc), one(LANES), one(LANES), one(d_ssm), one(d_ssm),
                  pl.BlockSpec((LANES, d_ssm), lambda b, c: (0, 0)),
                  pl.BlockSpec((rb, n_next), lambda b, c: (blk0 + b * nc + c, 0)),
                  pl.BlockSpec((rb2, n_plain), lambda b, c: (b * nc + c, 0))],
        out_specs=[pl.BlockSpec((q, d_ssm), lambda b, c: (b * nc + c, 0)),
                   pl.BlockSpec((None, d_ssm, n_state), lambda b, c: (b, 0, 0)),
                   pl.BlockSpec((n_tiles, rb, tn_next), lambda b, c: (0, b * nc + c, 0)),
                   pl.BlockSpec((rb2, n_plain), lambda b, c: (b * nc + c, 0))],
        out_shape=[jax.ShapeDtypeStruct((nb * seq, d_ssm), BF16),
                   jax.ShapeDtypeStruct((nb, d_ssm, n_state), F32),
                   jax.ShapeDtypeStruct((n_tiles, k_next - row0, tn_next), BF16),
                   jax.ShapeDtypeStruct((k_plain, n_plain), BF16)],
        scratch_shapes=[pltpu.VMEM((SUBLANES, d_xbc), F32), pltpu.VMEM((n_state, d_ssm), F32)],
        compiler_params=_params(2),
        name="ssd_prompt",
    )(proj, proj, dt_raw, cw, cb, dtb, alog, dexp, gn, e, w_next, w_plain)


def _ssd_sample_pre_kernel(xbc_ref, st_ref, dt_ref, cw_ref, cb_ref, dtb_ref, alog_ref, dexp_ref,
                           e_ref, gsum_ref, gexp_ref, ydp_ref, ecum_ref, xdd_ref, bc_ref, cd_ref,
                           *, d_ssm, n_state, groups):
    nt, bb, _ = xbc_ref.shape
    xc = _silu(_conv_slabs(st_ref, xbc_ref, cw_ref, cb_ref))
    gn_w = groups * n_state
    xs = xc[:, :d_ssm]
    bm = xc[:, d_ssm:d_ssm + gn_w]
    cm = xc[:, d_ssm + gn_w:d_ssm + 2 * gn_w]
    bc_ref[...] = xc[:, d_ssm:d_ssm + 2 * gn_w].reshape(nt, bb, 2 * gn_w)

    dt = _softplus(dt_ref[...].reshape(nt * bb, LANES) + dtb_ref[...])
    dta = dt * (-jnp.exp(alog_ref[...]))
    cums = [dta[0:bb]]
    for t in range(1, nt):
        cums.append(cums[-1] + dta[t * bb:(t + 1) * bb])
    cum = jnp.concatenate(cums, axis=0)
    cd_ref[...] = jnp.exp(cums[-1])
    e = e_ref[...]
    dt_e = _dot_sel_right(dt, e)
    cum_e = _dot_sel_right(cum, e)
    ecum_ref[...] = jnp.exp(cum_e).reshape(nt, bb, d_ssm)
    xdt = xs * dt_e
    sl = lambda v, t: v[t * bb:(t + 1) * bb]
    cum_last = sl(cum_e, nt - 1)
    gsum, gexp = gsum_ref[...], gexp_ref[...]
    for t in range(nt):
        xdd_ref[t] = sl(xdt, t) * jnp.exp(cum_last - sl(cum_e, t))
        acc = dexp_ref[...] * sl(xs, t)
        for s in range(t + 1):
            cb_e = _dot_sel_right(_dot_sel_right(sl(cm, t) * sl(bm, s), gsum), gexp)
            acc = acc + (cb_e * jnp.exp(sl(cum_e, t) - sl(cum_e, s))) * sl(xdt, s)
        ydp_ref[t] = acc


def _ssd_sample_pre(proj3, st3, dt3, d_ssm, d_xbc, n_state, cw, cb, dtb, alog, dexp, e, gsum, gexp,
                    *, bb):
    nt, nb, n_main = proj3.shape
    k_taps = cw.shape[0]
    x_blk = (n_main - d_xbc) // d_xbc
    gn_w = SSD_GROUPS * n_state
    full = lambda r, w: pl.BlockSpec((r, w), lambda i: (0, 0))
    slab = lambda n, w: pl.BlockSpec((n, bb, w), lambda i: (0, i, 0))
    f32 = lambda *s: jax.ShapeDtypeStruct(s, F32)
    return pl.pallas_call(
        functools.partial(_ssd_sample_pre_kernel, d_ssm=d_ssm, n_state=n_state, groups=SSD_GROUPS),
        grid=(nb // bb,),
        in_specs=[pl.BlockSpec((nt, bb, d_xbc), lambda i: (0, i, x_blk)),
                  slab(k_taps - 1, d_xbc), slab(nt, LANES), full(k_taps, d_xbc),
                  full(1, d_xbc), full(1, LANES), full(1, LANES), full(1, d_ssm),
                  full(LANES, d_ssm), full(gn_w, LANES), full(LANES, d_ssm)],
        out_specs=[slab(nt, d_ssm), slab(nt, d_ssm), slab(nt, d_ssm), slab(nt, 2 * gn_w),
                   pl.BlockSpec((bb, LANES), lambda i: (i, 0))],
        out_shape=[f32(nt, nb, d_ssm), f32(nt, nb, d_ssm), f32(nt, nb, d_ssm),
                   f32(nt, nb, 2 * gn_w), f32(nb, LANES)],
        compiler_params=_params(1),
        name="ssd_sample_pre",
    )(proj3, st3, dt3, cw, cb, dtb, alog, dexp, e, gsum, gexp)


def _ssd_sample_state_kernel(s0_ref, c_ref, b_ref, xdd_ref, cd_ref, snew_ref, yoff_ref,
                             *, bt, groups, n_state, headdim):
    heads = s0_ref.shape[1]
    hpg = heads // groups
    gw = hpg * headdim

    def body(b, carry):
        cd_row = cd_ref[pl.ds(b, 1), :]
        cb = c_ref[b].astype(BF16)
        bb = b_ref[b].astype(BF16)
        xb = xdd_ref[b].astype(BF16)
        for g in range(groups):
            s0 = s0_ref[b, g * hpg:(g + 1) * hpg].reshape(gw, n_state)
            yoff = lax.dot_general(cb[:, g * n_state:(g + 1) * n_state], s0.astype(BF16),
                                   (((1,), (1,)), ((), ())), preferred_element_type=F32)
            yoff_ref[b, :, g * gw:(g + 1) * gw] = yoff
            upd = lax.dot_general(xb[:, g * gw:(g + 1) * gw], bb[:, g * n_state:(g + 1) * n_state],
                                  (((0,), (0,)), ((), ())), preferred_element_type=F32)
            dec = jnp.concatenate(
                [jnp.broadcast_to(cd_row[:, h:h + 1], (headdim, n_state))
                 for h in range(g * hpg, (g + 1) * hpg)], axis=0)
            snew_ref[b, g * hpg:(g + 1) * hpg] = (s0 * dec + upd).reshape(hpg, headdim, n_state)
        return carry

    lax.fori_loop(0, bt, body, 0)


def _ssd_sample_state(s0, c_b, b_b, xdd_b, cd, *, bt):
    nb, heads, headdim, n_state = s0.shape
    nt = c_b.shape[1]
    gn_w = c_b.shape[2]
    d_ssm = xdd_b.shape[2]
    return pl.pallas_call(
        functools.partial(_ssd_sample_state_kernel, bt=bt, groups=SSD_GROUPS, n_state=n_state,
                          headdim=headdim),
        grid=(nb // bt,),
        in_specs=[pl.BlockSpec((bt, heads, headdim, n_state), lambda i: (i, 0, 0, 0)),
                  pl.BlockSpec((bt, nt, gn_w), lambda i: (i, 0, 0)),
                  pl.BlockSpec((bt, nt, gn_w), lambda i: (i, 0, 0)),
                  pl.BlockSpec((bt, nt, d_ssm), lambda i: (i, 0, 0)),
                  pl.BlockSpec((bt, LANES), lambda i: (i, 0))],
        out_specs=[pl.BlockSpec((bt, heads, headdim, n_state), lambda i: (i, 0, 0, 0)),
                   pl.BlockSpec((bt, nt, d_ssm), lambda i: (i, 0, 0))],
        out_shape=[jax.ShapeDtypeStruct(s0.shape, F32),
                   jax.ShapeDtypeStruct((nb, nt, d_ssm), F32)],
        compiler_params=_params(1),
        name="ssd_sample_state",
    )(s0, c_b, b_b, xdd_b, cd)


def _ssd_sample_post_kernel(ydp_ref, yoff_ref, ecum_ref, z_ref, gn_ref, out_ref, *, groups):
    nt, bb, d_ssm = ydp_ref.shape
    y = (ydp_ref[...] + yoff_ref[...] * ecum_ref[...]).reshape(nt * bb, d_ssm)
    z = z_ref[...].reshape(nt * bb, d_ssm)
    out_ref[...] = _group_norm_out(y, z, gn_ref, groups).reshape(nt, bb, d_ssm)


def _ssd_sample_post(ydp, yoff, ecum, proj3, gn, d_ssm, d_xbc, *, bb):
    nt, nb, n_main = proj3.shape
    z_blk = (n_main - d_xbc - d_ssm) // d_ssm
    slab = pl.BlockSpec((nt, bb, d_ssm), lambda i: (0, i, 0))
    return pl.pallas_call(
        functools.partial(_ssd_sample_post_kernel, groups=SSD_GROUPS),
        grid=(nb // bb,),
        in_specs=[slab, slab, slab, pl.BlockSpec((nt, bb, d_ssm), lambda i: (0, i, z_blk)),
                  pl.BlockSpec((1, d_ssm), lambda i: (0, 0))],
        out_specs=slab,
        out_shape=jax.ShapeDtypeStruct((nt, nb, d_ssm), BF16),
        compiler_params=_params(1),
        name="ssd_sample_post",
    )(ydp, yoff, ecum, proj3, gn)


def _pad_lanes(v):
    return jnp.pad(v.astype(F32), (0, LANES - v.shape[0])).reshape(1, LANES)


def kernel(x_prompt, x_sample, state_rglru_h, state_rglru_conv, state_ssd_h, state_ssd_conv, state_ffn_conv, g_mix, w_in, rg_conv_w, rg_conv_b, rg_gate_a_w, rg_gate_a_b, rg_gate_i_w, rg_gate_i_b, rg_lambda, g_rg_out, ssd_conv_w, ssd_conv_b, ssd_dt_bias, ssd_A_log, ssd_D, g_ssd_norm, w_out, g_ffn, w_ffn_up, ffn_conv_w, ffn_conv_b, w_ffn_down, g_final):
    depth = g_mix.shape[0]
    pb, seq, d_model = x_prompt.shape
    sb, dec_seq, _ = x_sample.shape
    d_rnn = rg_lambda.shape[1]
    d_ssm = g_ssd_norm.shape[1]
    d_xbc = ssd_conv_w.shape[2]
    d_ff = ffn_conv_w.shape[2]
    heads = ssd_A_log.shape[1]
    headdim = d_ssm // heads
    n_state = state_ssd_h.shape[-1]
    n_main = 2 * d_rnn + d_ssm + d_xbc
    k_rg = rg_conv_w.shape[1]
    k_ssd = ssd_conv_w.shape[1]
    k_ffn = ffn_conv_w.shape[1]
    gn_w = SSD_GROUPS * n_state
    assert heads <= LANES and n_state == LANES and k_ffn == 3 and dec_seq >= max(k_rg, k_ssd) - 1

    hid = lax.broadcasted_iota(jnp.int32, (LANES, d_ssm), 0)
    cid = lax.broadcasted_iota(jnp.int32, (LANES, d_ssm), 1)
    e_sel = (hid == cid // headdim).astype(BF16)
    g_exp = (hid == cid // (d_ssm // SSD_GROUPS)).astype(BF16)
    g_sum = (lax.broadcasted_iota(jnp.int32, (gn_w, LANES), 0) // n_state
             == lax.broadcasted_iota(jnp.int32, (gn_w, LANES), 1)).astype(BF16)

    yp = x_prompt.reshape(pb * seq, d_model)
    ys = jnp.transpose(x_sample, (1, 0, 2)).reshape(dec_seq * sb, d_model)
    p_new = [[], [], [], [], []]
    s_new = [[], [], [], [], []]
    row = lambda v: v.reshape(1, -1)
    swap = lambda a3: jnp.transpose(a3, (1, 0, 2))
    bb = 32

    for l in range(depth):
        wdt_b = jnp.pad(w_in[l][:, n_main:], ((0, 0), (0, LANES - heads))).astype(BF16)
        wa_b = rg_gate_a_w[l].astype(BF16)
        wi_b = rg_gate_i_w[l].astype(BF16)
        rg_args = (rg_conv_w[l], row(rg_conv_b[l]), wa_b, row(rg_gate_a_b[l]), wi_b,
                   row(rg_gate_i_b[l]), row(rg_lambda[l]), row(g_rg_out[l]))
        dtb = _pad_lanes(ssd_dt_bias[l])
        alog = _pad_lanes(ssd_A_log[l])
        dexp = jnp.repeat(ssd_D[l].astype(F32), headdim).reshape(1, d_ssm)
        gn = row(g_ssd_norm[l])

        w_in_b = w_in[l].astype(BF16)
        ms = ys.shape[0]
        proj_s, dt_s = _in_proj(ys, row(g_mix[l]), w_in_b, wdt_b, n_main, tm=ms, tn=TN_IN)

        proj_p, dt_p = _in_proj(yp, row(g_mix[l]), w_in_b, wdt_b, n_main, tm=TM, tn=TN_IN)
        rg_steps = pb * (seq // TT_RG)
        ssd_steps = pb * (seq // SSD_CHUNK)
        rb_ssd = d_model // (2 * rg_steps + ssd_steps)
        assert rb_ssd % 16 == 0 and (2 * rg_steps + ssd_steps) * rb_ssd == d_model
        rg_p, hl_p, w_up_ta = _rglru_prompt(proj_p, pb, seq, d_rnn, *rg_args, w_ffn_up[l], TN_UP,
                                            2 * rb_ssd, tt=TT_RG)
        ssd_p, hfin_p, w_up_tb, w_out_b = _ssd_prompt(
            proj_p, dt_p, pb, seq, d_ssm, d_xbc, n_state, headdim, ssd_conv_w[l], row(ssd_conv_b[l]),
            dtb, alog, dexp, gn, e_sel, w_ffn_up[l], TN_UP, rb_ssd, 2 * rb_ssd * rg_steps, w_out[l])
        x1_p, h2_p, ss_p = _out_proj(rg_p, ssd_p, w_out_b, yp, row(g_ffn[l]), tm=TM, tn=TN_OUT)
        act_p, gst_p, w_down_t = _ffn_up_prompt(h2_p, ss_p, w_up_ta, w_up_tb, ffn_conv_w[l],
                                                row(ffn_conv_b[l]), seq, w_ffn_down[l], TN_DOWN, tm=TM)
        proj_p3 = proj_p.reshape(pb, seq, n_main)
        p_new[0].append(hl_p[:, SUBLANES - 1, :])
        p_new[1].append(proj_p3[:, seq - (k_rg - 1):, :d_rnn])
        p_new[2].append(hfin_p.reshape(pb, heads, headdim, n_state))
        p_new[3].append(proj_p3[:, seq - (k_ssd - 1):, n_main - d_xbc:])
        tiles = seq // TM
        p_new[4].append(gst_p[tiles - 1::tiles, SUBLANES - (k_ffn - 1):, :])

        proj_s3 = proj_s.reshape(dec_seq, sb, n_main)
        rg_s, hl_s = _rglru_sample(proj_s3, swap(state_rglru_conv[l]), state_rglru_h[l], d_rnn,
                                   *rg_args, bb=bb)
        ydp, ecum, xdd, bc, cd = _ssd_sample_pre(
            proj_s3, swap(state_ssd_conv[l]), dt_s.reshape(dec_seq, sb, LANES), d_ssm, d_xbc, n_state,
            ssd_conv_w[l], row(ssd_conv_b[l]), dtb, alog, dexp, e_sel, g_sum, g_exp, bb=bb)
        bc_b = swap(bc)
        snew, yoff_b = _ssd_sample_state(state_ssd_h[l], bc_b[:, :, gn_w:], bc_b[:, :, :gn_w],
                                         swap(xdd), cd, bt=8)
        ssd_s = _ssd_sample_post(ydp, swap(yoff_b), ecum, proj_s3, gn, d_ssm, d_xbc, bb=bb)
        x1_s, h2_s, ss_s = _out_proj(rg_s.reshape(ms, d_rnn), ssd_s.reshape(ms, d_ssm), w_out_b, ys,
                                     row(g_ffn[l]), tm=ms, tn=TN_OUT)
        act_s, gst_s = _ffn_up_sample(h2_s, ss_s, w_up_ta, w_up_tb, ffn_conv_w[l], row(ffn_conv_b[l]),
                                      swap(state_ffn_conv[l]).reshape(-1, d_ff), sb)
        s_new[0].append(hl_s)
        s_new[1].append(swap(proj_s3[dec_seq - (k_rg - 1):, :, :d_rnn]))
        s_new[2].append(snew)
        s_new[3].append(swap(proj_s3[dec_seq - (k_ssd - 1):, :, n_main - d_xbc:]))
        s_new[4].append(swap(gst_s.reshape(k_ffn - 1, sb, d_ff)))

        last = l == depth - 1
        yp = _ffn_down(act_p, w_down_t, x1_p, row(g_final), last, tm=TM, tk=TK_DOWN)
        ys = _ffn_down(act_s, w_down_t, x1_s, row(g_final), last, tm=ms, tk=TK_DOWN)

    y_prompt = yp.reshape(pb, seq, d_model)
    y_sample = jnp.transpose(ys.reshape(dec_seq, sb, d_model), (1, 0, 2))
    return (y_prompt, y_sample,
            jnp.stack(p_new[0]), jnp.stack(p_new[1]), jnp.stack(p_new[2]), jnp.stack(p_new[3]),
            jnp.stack(p_new[4]),
            jnp.stack(s_new[0]), jnp.stack(s_new[1]), jnp.stack(s_new[2]), jnp.stack(s_new[3]),
            jnp.stack(s_new[4]))
```

```python
import functools

import jax
import jax.numpy as jnp
from jax import lax
from jax.experimental import pallas as pl
from jax.experimental.pallas import tpu as pltpu

F32 = jnp.float32
BF16 = jnp.bfloat16

EPS = 1e-6
RG_C = 8.0
SSD_CHUNK = 128
SSD_GROUPS = 4
LANES = 128
SUBLANES = 8
VMEM_LIMIT = 56 * 2**20

TM = 1024
PROJ_DTYPE = BF16
ROW_CHUNK = 256
TT_RG = 256
TN_IN = 512
TN_OUT = 512
TN_UP = 512
TN_DOWN = 512
TK_DOWN = 4096


def _params(n_axes):
    return pltpu.CompilerParams(dimension_semantics=("arbitrary",) * n_axes,
                                vmem_limit_bytes=VMEM_LIMIT)


def _dot(a, b):
    return jnp.dot(a, b, preferred_element_type=F32)


def _rms_rows(x, g):
    ms = jnp.mean(x * x, axis=-1, keepdims=True)
    return (x * lax.rsqrt(ms + EPS)) * g


def _split3(x):
    hi = x.astype(BF16)
    r = x - hi.astype(F32)
    mid = r.astype(BF16)
    lo = (r - mid.astype(F32)).astype(BF16)
    return hi, mid, lo


def _dot_sel_right(x, sel):
    hi, mid, lo = _split3(x)
    return (_dot(hi, sel) + _dot(mid, sel)) + _dot(lo, sel)


def _dot_sel_left(sel, x):
    hi, mid, lo = _split3(x)
    return (_dot(sel, hi) + _dot(sel, mid)) + _dot(sel, lo)


def _inproj_kernel(x_hbm, g_ref, w_ref, wdt_ref, o_ref, dt_ref, x_buf, hn_ref, sem, *, ni, tm):
    i = pl.program_id(0)
    j = pl.program_id(1)

    def x_in(tile):
        return pltpu.make_async_copy(x_hbm.at[pl.ds(tile * tm, tm)], x_buf, sem)

    @pl.when(j == 0)
    def _():
        @pl.when(i == 0)
        def _():
            x_in(0).start()

        x_in(i).wait()
        for r in range(0, tm, ROW_CHUNK):
            hn_ref[r:r + ROW_CHUNK, :] = _rms_rows(x_buf[r:r + ROW_CHUNK, :], g_ref[...]).astype(BF16)
        dt_ref[...] = _dot(hn_ref[...], wdt_ref[...])

        @pl.when(i + 1 < ni)
        def _():
            x_in(i + 1).start()

    o_ref[...] = _dot(hn_ref[...], w_ref[...]).astype(o_ref.dtype)


def _cast_col_tiles(src_ref, dst_ref):
    n_tiles, _, tn = dst_ref.shape
    for t in range(n_tiles):
        dst_ref[t] = src_ref[:, t * tn:(t + 1) * tn].astype(BF16)


def _in_proj(x2d, g, w_b, wdt_b, n_main, *, tm, tn):
    m, d = x2d.shape
    ni = m // tm
    return pl.pallas_call(
        functools.partial(_inproj_kernel, ni=ni, tm=tm),
        grid=(ni, n_main // tn),
        in_specs=[pl.BlockSpec(memory_space=pl.ANY),
                  pl.BlockSpec((1, d), lambda i, j: (0, 0)),
                  pl.BlockSpec((d, tn), lambda i, j: (0, j)),
                  pl.BlockSpec((d, LANES), lambda i, j: (0, 0))],
        out_specs=[pl.BlockSpec((tm, tn), lambda i, j: (i, j)),
                   pl.BlockSpec((tm, LANES), lambda i, j: (i, 0))],
        out_shape=[jax.ShapeDtypeStruct((m, n_main), PROJ_DTYPE), jax.ShapeDtypeStruct((m, LANES), F32)],
        scratch_shapes=[pltpu.VMEM((tm, d), F32), pltpu.VMEM((tm, d), BF16), pltpu.SemaphoreType.DMA(())],
        compiler_params=_params(2),
        name="in_proj",
    )(x2d, g, w_b, wdt_b)


def _outproj_kernel(rg_ref, ssd_ref, wt_ref, wb_ref, x_ref, g_ref, x1_ref, xg_ref, ss_ref):
    j = pl.program_id(1)
    x1 = x_ref[...] + (_dot(rg_ref[...], wt_ref[...]) + _dot(ssd_ref[...], wb_ref[...]))
    x1_ref[...] = x1
    xg_ref[...] = (x1 * g_ref[...]).astype(BF16)
    part = jnp.broadcast_to(jnp.sum(x1 * x1, axis=-1, keepdims=True), ss_ref.shape)

    @pl.when(j == 0)
    def _():
        ss_ref[...] = part

    @pl.when(j > 0)
    def _():
        ss_ref[...] += part


def _out_proj(rg, ssd, w_b, x2d, g, *, tm, tn):
    m, d = x2d.shape
    k_half = rg.shape[1]
    return pl.pallas_call(
        _outproj_kernel,
        grid=(m // tm, d // tn),
        in_specs=[pl.BlockSpec((tm, k_half), lambda i, j: (i, 0)),
                  pl.BlockSpec((tm, k_half), lambda i, j: (i, 0)),
                  pl.BlockSpec((k_half, tn), lambda i, j: (0, j)),
                  pl.BlockSpec((k_half, tn), lambda i, j: (1, j)),
                  pl.BlockSpec((tm, tn), lambda i, j: (i, j)),
                  pl.BlockSpec((1, tn), lambda i, j: (0, j))],
        out_specs=[pl.BlockSpec((tm, tn), lambda i, j: (i, j)),
                   pl.BlockSpec((tm, tn), lambda i, j: (i, j)),
                   pl.BlockSpec((tm, LANES), lambda i, j: (i, 0))],
        out_shape=[jax.ShapeDtypeStruct((m, d), F32),
                   jax.ShapeDtypeStruct((m, d), BF16),
                   jax.ShapeDtypeStruct((m, LANES), F32)],
        compiler_params=_params(2),
        name="out_proj",
    )(rg, ssd, w_b, w_b, x2d, g)


def _glu(gate_c, val):
    return (jax.nn.gelu(gate_c) * val).astype(BF16)


def _dot_split_k(h_ref, wa_ref, wb_ref):
    ka = wa_ref.shape[0]
    return _dot(h_ref[:, :ka], wa_ref[...]) + _dot(h_ref[:, ka:], wb_ref[...])


def _inv_rms(ss_ref, width):
    return lax.rsqrt(ss_ref[:, 0:1] / width + EPS)


def _ffn_up_prompt_kernel(xg_ref, ss_ref, wga_ref, wgb_ref, wva_ref, wvb_ref, cw_ref, cb_ref, wsrc_ref,
                          act_ref, gst_ref, wdst_ref, carry_ref, *, tiles_per_seq):
    i = pl.program_id(0)
    j = pl.program_id(1)
    _cast_col_tiles(wsrc_ref, wdst_ref)
    inv = _inv_rms(ss_ref, xg_ref.shape[1])
    gate = _dot_split_k(xg_ref, wga_ref, wgb_ref) * inv
    val = _dot_split_k(xg_ref, wva_ref, wvb_ref) * inv
    tm, tn = gate.shape
    w0, w1, w2 = cw_ref[0:1, :], cw_ref[1:2, :], cw_ref[2:3, :]
    cb = cb_ref[...]
    g1 = pltpu.roll(gate, 1, 0)
    g2 = pltpu.roll(gate, 2, 0)
    act_ref[...] = _glu(((cb + g2 * w0) + g1 * w1) + gate * w2, val)
    @pl.when((i % tiles_per_seq) == 0)
    def _():
        carry_ref[j] = jnp.zeros((SUBLANES, tn), F32)

    prev = carry_ref[j]
    rows = lax.broadcasted_iota(jnp.int32, (SUBLANES, tn), 0)
    t1 = jnp.where(rows < 1, pltpu.roll(prev, 1, 0), g1[0:SUBLANES])
    t2 = jnp.where(rows < 2, pltpu.roll(prev, 2, 0), g2[0:SUBLANES])
    act_ref[0:SUBLANES, :] = _glu(((cb + t2 * w0) + t1 * w1) + gate[0:SUBLANES] * w2,
                                  val[0:SUBLANES])
    last = gate[tm - SUBLANES:tm]
    carry_ref[j] = last
    gst_ref[...] = last


def _ffn_up_prompt(h2, ss, w_ta, w_tb, cw, cb, seq, w_next, tn_next, *, tm):
    m, d = h2.shape
    nj, ka, tn = w_ta.shape[0] // 2, w_ta.shape[1], w_ta.shape[2]
    kb = w_tb.shape[1]
    assert ka + kb == d and ka % LANES == 0
    d_ff = nj * tn
    steps = (m // tm) * nj
    k_next, n_next = w_next.shape
    rb = k_next // steps
    assert rb * steps == k_next and rb % 16 == 0 and n_next % tn_next == 0
    return pl.pallas_call(
        functools.partial(_ffn_up_prompt_kernel, tiles_per_seq=seq // tm),
        grid=(m // tm, nj),
        in_specs=[pl.BlockSpec((tm, d), lambda i, j: (i, 0)),
                  pl.BlockSpec((tm, LANES), lambda i, j: (i, 0)),
                  pl.BlockSpec((None, ka, tn), lambda i, j: (j, 0, 0)),
                  pl.BlockSpec((None, kb, tn), lambda i, j: (j, 0, 0)),
                  pl.BlockSpec((None, ka, tn), lambda i, j: (j + nj, 0, 0)),
                  pl.BlockSpec((None, kb, tn), lambda i, j: (j + nj, 0, 0)),
                  pl.BlockSpec((3, tn), lambda i, j: (0, j)),
                  pl.BlockSpec((1, tn), lambda i, j: (0, j)),
                  pl.BlockSpec((rb, n_next), lambda i, j: (i * nj + j, 0))],
        out_specs=[pl.BlockSpec((tm, tn), lambda i, j: (i, j)),
                   pl.BlockSpec((None, SUBLANES, tn), lambda i, j: (i, 0, j)),
                   pl.BlockSpec((n_next // tn_next, rb, tn_next), lambda i, j: (0, i * nj + j, 0))],
        out_shape=[jax.ShapeDtypeStruct((m, d_ff), BF16),
                   jax.ShapeDtypeStruct((m // tm, SUBLANES, d_ff), F32),
                   jax.ShapeDtypeStruct((n_next // tn_next, k_next, tn_next), BF16)],
        scratch_shapes=[pltpu.VMEM((nj, SUBLANES, tn), F32)],
        compiler_params=_params(2),
        name="ffn_up_prompt",
    )(h2, ss, w_ta, w_tb, w_ta, w_tb, cw, cb, w_next)


def _ffn_up_sample_kernel(xg_ref, ss_ref, wga_ref, wgb_ref, wva_ref, wvb_ref, cw_ref, cb_ref, st_ref,
                          act_ref, nst_ref, *, nb, nt):
    inv = _inv_rms(ss_ref, xg_ref.shape[1])
    gate = _dot_split_k(xg_ref, wga_ref, wgb_ref) * inv
    val = _dot_split_k(xg_ref, wva_ref, wvb_ref) * inv
    w0, w1, w2 = cw_ref[0:1, :], cw_ref[1:2, :], cw_ref[2:3, :]
    cb = cb_ref[...]
    ext = [st_ref[0:nb, :], st_ref[nb:2 * nb, :]] + [gate[t * nb:(t + 1) * nb] for t in range(nt)]
    for t in range(nt):
        gc = ((cb + ext[t] * w0) + ext[t + 1] * w1) + ext[t + 2] * w2
        act_ref[t * nb:(t + 1) * nb, :] = _glu(gc, val[t * nb:(t + 1) * nb])
    nst_ref[0:nb, :] = ext[nt]
    nst_ref[nb:2 * nb, :] = ext[nt + 1]


def _ffn_up_sample(h2, ss, w_ta, w_tb, cw, cb, st, nb):
    m, d = h2.shape
    nj, ka, tn = w_ta.shape[0] // 2, w_ta.shape[1], w_ta.shape[2]
    kb = w_tb.shape[1]
    d_ff = nj * tn
    return pl.pallas_call(
        functools.partial(_ffn_up_sample_kernel, nb=nb, nt=m // nb),
        grid=(nj,),
        in_specs=[pl.BlockSpec((m, d), lambda j: (0, 0)),
                  pl.BlockSpec((m, LANES), lambda j: (0, 0)),
                  pl.BlockSpec((None, ka, tn), lambda j: (j, 0, 0)),
                  pl.BlockSpec((None, kb, tn), lambda j: (j, 0, 0)),
                  pl.BlockSpec((None, ka, tn), lambda j: (j + nj, 0, 0)),
                  pl.BlockSpec((None, kb, tn), lambda j: (j + nj, 0, 0)),
                  pl.BlockSpec((3, tn), lambda j: (0, j)),
                  pl.BlockSpec((1, tn), lambda j: (0, j)),
                  pl.BlockSpec((2 * nb, tn), lambda j: (0, j))],
        out_specs=[pl.BlockSpec((m, tn), lambda j: (0, j)),
                   pl.BlockSpec((2 * nb, tn), lambda j: (0, j))],
        out_shape=[jax.ShapeDtypeStruct((m, d_ff), BF16),
                   jax.ShapeDtypeStruct((2 * nb, d_ff), F32)],
        compiler_params=_params(1),
        name="ffn_up_sample",
    )(h2, ss, w_ta, w_tb, w_ta, w_tb, cw, cb, st)


def _ffn_down_kernel(act_ref, w_ref, x1_ref, g_ref, y_hbm, rows_ref, sem,
                     *, ni, nk, nj, tm, tn, final_norm):
    i = pl.program_id(0)
    k = pl.program_id(1)
    j = pl.program_id(2)
    cols = pl.ds(pl.multiple_of(j * tn, tn), tn)

    def rows_out(tile):
        return pltpu.make_async_copy(rows_ref, y_hbm.at[pl.ds(tile * tm, tm)], sem)

    @pl.when(k == 0)
    def _():
        @pl.when((j == 0) & (i > 0))
        def _():
            rows_out(i - 1).wait()

        rows_ref[:, cols] = x1_ref[...]

    rows_ref[:, cols] += _dot(act_ref[...], w_ref[...])

    @pl.when((k == nk - 1) & (j == nj - 1))
    def _():
        if final_norm:
            for r in range(0, tm, ROW_CHUNK):
                rows_ref[r:r + ROW_CHUNK, :] = _rms_rows(rows_ref[r:r + ROW_CHUNK, :], g_ref[...])
        rows_out(i).start()

        @pl.when(i == ni - 1)
        def _():
            rows_out(i).wait()


def _ffn_down(act, w_t, x1, g, final_norm, *, tm, tk):
    m, d = x1.shape
    nj, d_ff, tn = w_t.shape
    ni, nk = m // tm, d_ff // tk
    return pl.pallas_call(
        functools.partial(_ffn_down_kernel, ni=ni, nk=nk, nj=nj, tm=tm, tn=tn, final_norm=final_norm),
        grid=(ni, nk, nj),
        in_specs=[pl.BlockSpec((tm, tk), lambda i, k, j: (i, k)),
                  pl.BlockSpec((None, tk, tn), lambda i, k, j: (j, k, 0)),
                  pl.BlockSpec((tm, tn), lambda i, k, j: (i, jnp.where(k == 0, j, nj - 1))),
                  pl.BlockSpec((1, d), lambda i, k, j: (0, 0))],
        out_specs=pl.BlockSpec(memory_space=pl.ANY),
        out_shape=jax.ShapeDtypeStruct((m, d), F32),
        scratch_shapes=[pltpu.VMEM((tm, d), F32), pltpu.SemaphoreType.DMA(())],
        compiler_params=_params(3),
        name="ffn_down",
    )(act, w_t, x1, g)


def _softplus(x):
    return jnp.maximum(x, 0.0) + jnp.log1p(jnp.exp(-jnp.abs(x)))


def _rglru_coeffs(xc, wa_ref, ba_ref, wi_ref, bi_ref, lam_ref):
    xb = xc.astype(BF16)
    heads, blk = wa_ref.shape[0], wa_ref.shape[1]
    ga = jnp.concatenate([_dot(xb[:, h * blk:(h + 1) * blk], wa_ref[h]) for h in range(heads)], axis=1)
    gi = jnp.concatenate([_dot(xb[:, h * blk:(h + 1) * blk], wi_ref[h]) for h in range(heads)], axis=1)
    gate_r = jax.nn.sigmoid(ga + ba_ref[...])
    gate_i = jax.nn.sigmoid(gi + bi_ref[...])
    log_a = (-RG_C * gate_r) * _softplus(-lam_ref[...])
    a = jnp.exp(log_a)
    bx = jnp.sqrt(-jnp.tanh(log_a) * (a * a + 1.0)) * (gate_i * xc)
    return a, bx


def _rg_out(h, gate, g_ref):
    return _rms_rows(h * jax.nn.gelu(gate), g_ref[...]).astype(BF16)


def _conv_tile(x, prev, w_ref, b_ref, k_taps):
    t_len, width = x.shape
    groups = t_len // SUBLANES
    x3 = x.reshape(groups, SUBLANES, width)
    pos = lax.broadcasted_iota(jnp.int32, x3.shape, 1)
    out = b_ref[...].reshape(1, 1, width)
    for k in range(k_taps):
        s = k_taps - 1 - k
        wk = w_ref[k:k + 1, :].reshape(1, 1, width)
        if s == 0:
            shifted = x3
        else:
            rot = pltpu.roll(x3, s, 1)
            rot_prev = jnp.concatenate([pltpu.roll(prev, s, 0)[None], rot[:groups - 1]], axis=0)
            shifted = jnp.where(pos >= s, rot, rot_prev)
        out = out + shifted * wk
    return out.reshape(t_len, width)


def _rglru_prompt_kernel(x_ref, gate_ref, cw_ref, cb_ref, wa_ref, ba_ref, wi_ref, bi_ref, lam_ref,
                         g_ref, wsrc_ref, out_ref, hlast_ref, wdst_ref, xprev_ref, hc_ref, a_s, b_s, h_s,
                         *, k_taps):
    c = pl.program_id(1)
    _cast_col_tiles(wsrc_ref, wdst_ref)

    @pl.when(c == 0)
    def _():
        xprev_ref[...] = jnp.zeros_like(xprev_ref)
        hc_ref[...] = jnp.zeros_like(hc_ref)

    x = x_ref[...].astype(F32)
    t_len, width = x.shape
    xc = _conv_tile(x, xprev_ref[...], cw_ref, cb_ref, k_taps)
    xprev_ref[...] = x[t_len - SUBLANES:t_len]
    a, b = _rglru_coeffs(xc, wa_ref, ba_ref, wi_ref, bi_ref, lam_ref)
    groups = t_len // SUBLANES
    a = a.reshape(groups, SUBLANES, width)
    b = b.reshape(groups, SUBLANES, width)
    pos = lax.broadcasted_iota(jnp.int32, a.shape, 1)
    for s in (1, 2, 4):
        a_sh = pltpu.roll(a, s, 1)
        b_sh = pltpu.roll(b, s, 1)
        m = pos >= s
        b = jnp.where(m, a * b_sh + b, b)
        a = jnp.where(m, a * a_sh, a)
    a_s[...] = a.reshape(t_len, width)
    b_s[...] = b.reshape(t_len, width)
    h = hc_ref[...]
    for grp in range(t_len // SUBLANES):
        lo, hi = grp * SUBLANES, (grp + 1) * SUBLANES
        hl = jnp.broadcast_to(h[SUBLANES - 1:SUBLANES, :], (SUBLANES, width))
        h = a_s[lo:hi, :] * hl + b_s[lo:hi, :]
        h_s[lo:hi, :] = h
    hc_ref[...] = h
    hlast_ref[...] = h
    out_ref[...] = _rg_out(h_s[...], gate_ref[...].astype(F32), g_ref)


def _rglru_prompt(proj, nb, seq, d_rnn, cw, cb, wa, ba, wi, bi, lam, g, w_next, tn_next, rb, *, tt):
    nc = seq // tt
    heads, blk = wa.shape[0], wa.shape[1]
    k_next, n_next = w_next.shape
    vec = pl.BlockSpec((1, d_rnn), lambda b, c: (0, 0))
    gate_w = pl.BlockSpec((heads, blk, blk), lambda b, c: (0, 0, 0))
    return pl.pallas_call(
        functools.partial(_rglru_prompt_kernel, k_taps=cw.shape[0]),
        grid=(nb, nc),
        in_specs=[pl.BlockSpec((tt, d_rnn), lambda b, c: (b * nc + c, 0)),
                  pl.BlockSpec((tt, d_rnn), lambda b, c: (b * nc + c, 1)),
                  pl.BlockSpec((cw.shape[0], d_rnn), lambda b, c: (0, 0)),
                  vec, gate_w, vec, gate_w, vec, vec, vec,
                  pl.BlockSpec((rb, n_next), lambda b, c: (b * nc + c, 0))],
        out_specs=[pl.BlockSpec((tt, d_rnn), lambda b, c: (b * nc + c, 0)),
                   pl.BlockSpec((None, SUBLANES, d_rnn), lambda b, c: (b, 0, 0)),
                   pl.BlockSpec((n_next // tn_next, rb, tn_next), lambda b, c: (0, b * nc + c, 0))],
        out_shape=[jax.ShapeDtypeStruct((nb * seq, d_rnn), BF16),
                   jax.ShapeDtypeStruct((nb, SUBLANES, d_rnn), F32),
                   jax.ShapeDtypeStruct((n_next // tn_next, nb * nc * rb, tn_next), BF16)],
        scratch_shapes=[pltpu.VMEM((SUBLANES, d_rnn), F32), pltpu.VMEM((SUBLANES, d_rnn), F32),
                        pltpu.VMEM((tt, d_rnn), F32), pltpu.VMEM((tt, d_rnn), F32),
                        pltpu.VMEM((tt, d_rnn), F32)],
        compiler_params=_params(2),
        name="rglru_prompt",
    )(proj, proj, cw, cb, wa, ba, wi, bi, lam, g, w_next)


def _conv_slabs(st_ref, x_ref, w_ref, b_ref):
    k_taps = w_ref.shape[0]
    ext = [st_ref[s] for s in range(k_taps - 1)]
    ext += [x_ref[t].astype(F32) for t in range(x_ref.shape[0])]
    convs = []
    for t in range(x_ref.shape[0]):
        acc = b_ref[...]
        for k in range(k_taps):
            acc = acc + ext[t + k] * w_ref[k:k + 1, :]
        convs.append(acc)
    return jnp.concatenate(convs, axis=0)


def _rglru_sample_kernel(x_ref, gate_ref, st_ref, h0_ref, cw_ref, cb_ref, wa_ref, ba_ref, wi_ref,
                         bi_ref, lam_ref, g_ref, out_ref, hlast_ref, h_s):
    nt, bb, width = x_ref.shape
    xc = _conv_slabs(st_ref, x_ref, cw_ref, cb_ref)
    a, b = _rglru_coeffs(xc, wa_ref, ba_ref, wi_ref, bi_ref, lam_ref)
    h = h0_ref[...]
    for t in range(nt):
        h = a[t * bb:(t + 1) * bb] * h + b[t * bb:(t + 1) * bb]
        h_s[t * bb:(t + 1) * bb, :] = h
    hlast_ref[...] = h
    gate = gate_ref[...].astype(F32).reshape(nt * bb, width)
    out_ref[...] = _rg_out(h_s[...], gate, g_ref).reshape(nt, bb, width)


def _rglru_sample(proj3, st3, h0, d_rnn, cw, cb, wa, ba, wi, bi, lam, g, *, bb):
    nt, nb, _ = proj3.shape
    heads, blk = wa.shape[0], wa.shape[1]
    k_taps = cw.shape[0]
    vec = pl.BlockSpec((1, d_rnn), lambda i: (0, 0))
    gate_w = pl.BlockSpec((heads, blk, blk), lambda i: (0, 0, 0))
    return pl.pallas_call(
        _rglru_sample_kernel,
        grid=(nb // bb,),
        in_specs=[pl.BlockSpec((nt, bb, d_rnn), lambda i: (0, i, 0)),
                  pl.BlockSpec((nt, bb, d_rnn), lambda i: (0, i, 1)),
                  pl.BlockSpec((k_taps - 1, bb, d_rnn), lambda i: (0, i, 0)),
                  pl.BlockSpec((bb, d_rnn), lambda i: (i, 0)),
                  pl.BlockSpec((k_taps, d_rnn), lambda i: (0, 0)),
                  vec, gate_w, vec, gate_w, vec, vec, vec],
        out_specs=[pl.BlockSpec((nt, bb, d_rnn), lambda i: (0, i, 0)),
                   pl.BlockSpec((bb, d_rnn), lambda i: (i, 0))],
        out_shape=[jax.ShapeDtypeStruct((nt, nb, d_rnn), BF16),
                   jax.ShapeDtypeStruct((nb, d_rnn), F32)],
        scratch_shapes=[pltpu.VMEM((nt * bb, d_rnn), F32)],
        compiler_params=_params(1),
        name="rglru_sample",
    )(proj3, proj3, st3, h0, cw, cb, wa, ba, wi, bi, lam, g)


def _silu(x):
    return x * jax.nn.sigmoid(x)


def _group_norm_out(y, z, gn_ref, groups):
    u = y * _silu(z)
    gw = u.shape[1] // groups
    outs = []
    for g in range(groups):
        ug = u[:, g * gw:(g + 1) * gw]
        ms = jnp.mean(ug * ug, axis=-1, keepdims=True)
        outs.append((ug * lax.rsqrt(ms + EPS)) * gn_ref[:, g * gw:(g + 1) * gw])
    return jnp.concatenate(outs, axis=1).astype(BF16)


def _ssd_prompt_kernel(z_ref, xbc_ref, dt_ref, cw_ref, cb_ref, dtb_ref, alog_ref, dexp_ref, gn_ref,
                       e_ref, wsrc_ref, wsrc2_ref, out_ref, hfin_ref, wdst_ref, wdst2_ref,
                       carry_ref, s_ref, *, k_taps, d_ssm, n_state, groups, headdim, nc):
    c = pl.program_id(1)
    _cast_col_tiles(wsrc_ref, wdst_ref)
    wdst2_ref[...] = wsrc2_ref[...].astype(BF16)

    @pl.when(c == 0)
    def _():
        carry_ref[...] = jnp.zeros_like(carry_ref)
        s_ref[...] = jnp.zeros_like(s_ref)

    xbc = xbc_ref[...].astype(F32)
    q = xbc.shape[0]
    xc = _silu(_conv_tile(xbc, carry_ref[...], cw_ref, cb_ref, k_taps))
    carry_ref[...] = xbc[q - SUBLANES:q]
    gn_w = groups * n_state
    bm = xc[:, d_ssm:d_ssm + gn_w]
    cm = xc[:, d_ssm + gn_w:d_ssm + 2 * gn_w]

    dt = _softplus(dt_ref[...] + dtb_ref[...])
    dta = dt * (-jnp.exp(alog_ref[...]))
    row = lax.broadcasted_iota(jnp.int32, (q, q), 0)
    col = lax.broadcasted_iota(jnp.int32, (q, q), 1)
    causal = row >= col
    tril = jnp.where(causal, 1.0, 0.0).astype(BF16)
    cum = _dot_sel_left(tril, dta)
    cum_t = cum.T

    heads_per_group = d_ssm // headdim // groups
    pair_w = 2 * headdim
    gw = heads_per_group * headdim
    lane = lax.broadcasted_iota(jnp.int32, (q, pair_w), 1)
    for g in range(groups):
        gs = slice(g * gw, (g + 1) * gw)
        xs = xc[:, gs]
        e = e_ref[:, gs]
        dt_e = _dot_sel_right(dt, e)
        cum_e = _dot_sel_right(cum, e)
        xdt = xs * dt_e
        cum_last = cum_e[q - 1:q, :]
        xdd = (xdt * jnp.exp(cum_last - cum_e)).astype(BF16)
        cg = cm[:, g * n_state:(g + 1) * n_state].astype(BF16)
        bg32 = bm[:, g * n_state:(g + 1) * n_state]
        bg = bg32.astype(BF16)
        cb_g = lax.dot_general(cg, bg, (((1,), (1,)), ((), ())), preferred_element_type=F32)
        s_old = s_ref[:, gs]
        y_state = _dot(cg, s_old.astype(BF16)) * jnp.exp(cum_e)
        s_ref[:, gs] = s_old * jnp.exp(cum_last) + _dot(bg32.T.astype(BF16), xdd)
        y_pairs = []
        for pr in range(heads_per_group // 2):
            h0 = g * heads_per_group + 2 * pr
            ms = []
            for h in (h0, h0 + 1):
                seg = (jnp.broadcast_to(cum[:, h:h + 1], (q, q))
                       - jnp.broadcast_to(cum_t[h:h + 1, :], (q, q)))
                ms.append(cb_g * jnp.exp(jnp.where(causal, seg, -jnp.inf)))
            xp = xdt[:, 2 * pr * headdim:2 * pr * headdim + pair_w]
            lhs = jnp.concatenate(ms, axis=1).astype(BF16)
            rhs = jnp.concatenate([jnp.where(lane < headdim, xp, 0.0),
                                   jnp.where(lane >= headdim, xp, 0.0)], axis=0).astype(BF16)
            y_pairs.append(_dot(lhs, rhs))
        y = (y_state + jnp.concatenate(y_pairs, axis=1)) + dexp_ref[:, gs] * xs
        u = y * _silu(z_ref[:, gs].astype(F32))
        ms_g = jnp.mean(u * u, axis=-1, keepdims=True)
        out_ref[:, gs] = ((u * lax.rsqrt(ms_g + EPS)) * gn_ref[:, gs]).astype(BF16)

    @pl.when(c == nc - 1)
    def _():
        for blk in range(d_ssm // LANES):
            hfin_ref[blk * LANES:(blk + 1) * LANES, :] = s_ref[:, blk * LANES:(blk + 1) * LANES].T


def _ssd_prompt(proj, dt_raw, nb, seq, d_ssm, d_xbc, n_state, headdim, cw, cb, dtb, alog, dexp, gn, e,
                w_next, tn_next, rb, row0, w_plain):
    q = SSD_CHUNK
    nc = seq // q
    k_plain, n_plain = w_plain.shape
    rb2 = k_plain // (nb * nc)
    assert rb2 * nb * nc == k_plain and rb2 % 16 == 0
    z_blk = (proj.shape[1] - d_xbc - d_ssm) // d_ssm
    x_blk = (proj.shape[1] - d_xbc) // d_xbc
    assert z_blk * d_ssm + d_ssm + d_xbc == proj.shape[1] and x_blk * d_xbc + d_xbc == proj.shape[1]
    k_next, n_next = w_next.shape
    n_tiles = n_next // tn_next
    blk0 = row0 // rb
    assert blk0 * rb == row0 and row0 + nb * nc * rb == k_next
    one = lambda w: pl.BlockSpec((1, w), lambda b, c: (0, 0))
    return pl.pallas_call(
        functools.partial(_ssd_prompt_kernel, k_taps=cw.shape[0], d_ssm=d_ssm, n_state=n_state,
                          groups=SSD_GROUPS, headdim=headdim, nc=nc),
        grid=(nb, nc),
        in_specs=[pl.BlockSpec((q, d_ssm), lambda b, c: (b * nc + c, z_blk)),
                  pl.BlockSpec((q, d_xbc), lambda b, c: (b * nc + c, x_blk)),
                  pl.BlockSpec((q, LANES), lambda b, c: (b * nc + c, 0)),
                  pl.BlockSpec((cw.shape[0], d_xbc), lambda b, c: (0, 0)),
                  one(d_xbc), one(LANES), one(LANES), one(d_ssm), one(d_ssm),
                  pl.BlockSpec((LANES, d_ssm), lambda b, c: (0, 0)),
                  pl.BlockSpec((rb, n_next), lambda b, c: (blk0 + b * nc + c, 0)),
                  pl.BlockSpec((rb2, n_plain), lambda b, c: (b * nc + c, 0))],
        out_specs=[pl.BlockSpec((q, d_ssm), lambda b, c: (b * nc + c, 0)),
                   pl.BlockSpec((None, d_ssm, n_state), lambda b, c: (b, 0, 0)),
                   pl.BlockSpec((n_tiles, rb, tn_next), lambda b, c: (0, b * nc + c, 0)),
                   pl.BlockSpec((rb2, n_plain), lambda b, c: (b * nc + c, 0))],
        out_shape=[jax.ShapeDtypeStruct((nb * seq, d_ssm), BF16),
                   jax.ShapeDtypeStruct((nb, d_ssm, n_state), F32),
                   jax.ShapeDtypeStruct((n_tiles, k_next - row0, tn_next), BF16),
                   jax.ShapeDtypeStruct((k_plain, n_plain), BF16)],
        scratch_shapes=[pltpu.VMEM((SUBLANES, d_xbc), F32), pltpu.VMEM((n_state, d_ssm), F32)],
        compiler_params=_params(2),
        name="ssd_prompt",
    )(proj, proj, dt_raw, cw, cb, dtb, alog, dexp, gn, e, w_next, w_plain)


def _ssd_sample_pre_kernel(xbc_ref, st_ref, dt_ref, cw_ref, cb_ref, dtb_ref, alog_ref, dexp_ref,
                           e_ref, gsum_ref, gexp_ref, ydp_ref, ecum_ref, xdd_ref, bc_ref, cd_ref,
                           *, d_ssm, n_state, groups):
    nt, bb, _ = xbc_ref.shape
    xc = _silu(_conv_slabs(st_ref, xbc_ref, cw_ref, cb_ref))
    gn_w = groups * n_state
    xs = xc[:, :d_ssm]
    bm = xc[:, d_ssm:d_ssm + gn_w]
    cm = xc[:, d_ssm + gn_w:d_ssm + 2 * gn_w]
    bc_ref[...] = xc[:, d_ssm:d_ssm + 2 * gn_w].reshape(nt, bb, 2 * gn_w)

    dt = _softplus(dt_ref[...].reshape(nt * bb, LANES) + dtb_ref[...])
    dta = dt * (-jnp.exp(alog_ref[...]))
    cums = [dta[0:bb]]
    for t in range(1, nt):
        cums.append(cums[-1] + dta[t * bb:(t + 1) * bb])
    cum = jnp.concatenate(cums, axis=0)
    cd_ref[...] = jnp.exp(cums[-1])
    e = e_ref[...]
    dt_e = _dot_sel_right(dt, e)
    cum_e = _dot_sel_right(cum, e)
    ecum_ref[...] = jnp.exp(cum_e).reshape(nt, bb, d_ssm)
    xdt = xs * dt_e
    sl = lambda v, t: v[t * bb:(t + 1) * bb]
    cum_last = sl(cum_e, nt - 1)
    gsum, gexp = gsum_ref[...], gexp_ref[...]
    for t in range(nt):
        xdd_ref[t] = sl(xdt, t) * jnp.exp(cum_last - sl(cum_e, t))
        acc = dexp_ref[...] * sl(xs, t)
        for s in range(t + 1):
            cb_e = _dot_sel_right(_dot_sel_right(sl(cm, t) * sl(bm, s), gsum), gexp)
            acc = acc + (cb_e * jnp.exp(sl(cum_e, t) - sl(cum_e, s))) * sl(xdt, s)
        ydp_ref[t] = acc


def _ssd_sample_pre(proj3, st3, dt3, d_ssm, d_xbc, n_state, cw, cb, dtb, alog, dexp, e, gsum, gexp,
                    *, bb):
    nt, nb, n_main = proj3.shape
    k_taps = cw.shape[0]
    x_blk = (n_main - d_xbc) // d_xbc
    gn_w = SSD_GROUPS * n_state
    full = lambda r, w: pl.BlockSpec((r, w), lambda i: (0, 0))
    slab = lambda n, w: pl.BlockSpec((n, bb, w), lambda i: (0, i, 0))
    f32 = lambda *s: jax.ShapeDtypeStruct(s, F32)
    return pl.pallas_call(
        functools.partial(_ssd_sample_pre_kernel, d_ssm=d_ssm, n_state=n_state, groups=SSD_GROUPS),
        grid=(nb // bb,),
        in_specs=[pl.BlockSpec((nt, bb, d_xbc), lambda i: (0, i, x_blk)),
                  slab(k_taps - 1, d_xbc), slab(nt, LANES), full(k_taps, d_xbc),
                  full(1, d_xbc), full(1, LANES), full(1, LANES), full(1, d_ssm),
                  full(LANES, d_ssm), full(gn_w, LANES), full(LANES, d_ssm)],
        out_specs=[slab(nt, d_ssm), slab(nt, d_ssm), slab(nt, d_ssm), slab(nt, 2 * gn_w),
                   pl.BlockSpec((bb, LANES), lambda i: (i, 0))],
        out_shape=[f32(nt, nb, d_ssm), f32(nt, nb, d_ssm), f32(nt, nb, d_ssm),
                   f32(nt, nb, 2 * gn_w), f32(nb, LANES)],
        compiler_params=_params(1),
        name="ssd_sample_pre",
    )(proj3, st3, dt3, cw, cb, dtb, alog, dexp, e, gsum, gexp)


def _ssd_sample_state_kernel(s0_ref, c_ref, b_ref, xdd_ref, cd_ref, snew_ref, yoff_ref,
                             *, bt, groups, n_state, headdim):
    heads = s0_ref.shape[1]
    hpg = heads // groups
    gw = hpg * headdim

    def body(b, carry):
        cd_row = cd_ref[pl.ds(b, 1), :]
        cb = c_ref[b].astype(BF16)
        bb = b_ref[b].astype(BF16)
        xb = xdd_ref[b].astype(BF16)
        for g in range(groups):
            s0 = s0_ref[b, g * hpg:(g + 1) * hpg].reshape(gw, n_state)
            yoff = lax.dot_general(cb[:, g * n_state:(g + 1) * n_state], s0.astype(BF16),
                                   (((1,), (1,)), ((), ())), preferred_element_type=F32)
            yoff_ref[b, :, g * gw:(g + 1) * gw] = yoff
            upd = lax.dot_general(xb[:, g * gw:(g + 1) * gw], bb[:, g * n_state:(g + 1) * n_state],
                                  (((0,), (0,)), ((), ())), preferred_element_type=F32)
            dec = jnp.concatenate(
                [jnp.broadcast_to(cd_row[:, h:h + 1], (headdim, n_state))
                 for h in range(g * hpg, (g + 1) * hpg)], axis=0)
            snew_ref[b, g * hpg:(g + 1) * hpg] = (s0 * dec + upd).reshape(hpg, headdim, n_state)
        return carry

    lax.fori_loop(0, bt, body, 0)


def _ssd_sample_state(s0, c_b, b_b, xdd_b, cd, *, bt):
    nb, heads, headdim, n_state = s0.shape
    nt = c_b.shape[1]
    gn_w = c_b.shape[2]
    d_ssm = xdd_b.shape[2]
    return pl.pallas_call(
        functools.partial(_ssd_sample_state_kernel, bt=bt, groups=SSD_GROUPS, n_state=n_state,
                          headdim=headdim),
        grid=(nb // bt,),
        in_specs=[pl.BlockSpec((bt, heads, headdim, n_state), lambda i: (i, 0, 0, 0)),
                  pl.BlockSpec((bt, nt, gn_w), lambda i: (i, 0, 0)),
                  pl.BlockSpec((bt, nt, gn_w), lambda i: (i, 0, 0)),
                  pl.BlockSpec((bt, nt, d_ssm), lambda i: (i, 0, 0)),
                  pl.BlockSpec((bt, LANES), lambda i: (i, 0))],
        out_specs=[pl.BlockSpec((bt, heads, headdim, n_state), lambda i: (i, 0, 0, 0)),
                   pl.BlockSpec((bt, nt, d_ssm), lambda i: (i, 0, 0))],
        out_shape=[jax.ShapeDtypeStruct(s0.shape, F32),
                   jax.ShapeDtypeStruct((nb, nt, d_ssm), F32)],
        compiler_params=_params(1),
        name="ssd_sample_state",
    )(s0, c_b, b_b, xdd_b, cd)


def _ssd_sample_post_kernel(ydp_ref, yoff_ref, ecum_ref, z_ref, gn_ref, out_ref, *, groups):
    nt, bb, d_ssm = ydp_ref.shape
    y = (ydp_ref[...] + yoff_ref[...] * ecum_ref[...]).reshape(nt * bb, d_ssm)
    z = z_ref[...].astype(F32).reshape(nt * bb, d_ssm)
    out_ref[...] = _group_norm_out(y, z, gn_ref, groups).reshape(nt, bb, d_ssm)


def _ssd_sample_post(ydp, yoff, ecum, proj3, gn, d_ssm, d_xbc, *, bb):
    nt, nb, n_main = proj3.shape
    z_blk = (n_main - d_xbc - d_ssm) // d_ssm
    slab = pl.BlockSpec((nt, bb, d_ssm), lambda i: (0, i, 0))
    return pl.pallas_call(
        functools.partial(_ssd_sample_post_kernel, groups=SSD_GROUPS),
        grid=(nb // bb,),
        in_specs=[slab, slab, slab, pl.BlockSpec((nt, bb, d_ssm), lambda i: (0, i, z_blk)),
                  pl.BlockSpec((1, d_ssm), lambda i: (0, 0))],
        out_specs=slab,
        out_shape=jax.ShapeDtypeStruct((nt, nb, d_ssm), BF16),
        compiler_params=_params(1),
        name="ssd_sample_post",
    )(ydp, yoff, ecum, proj3, gn)


def _pad_lanes(v):
    return jnp.pad(v.astype(F32), (0, LANES - v.shape[0])).reshape(1, LANES)


def kernel(x_prompt, x_sample, state_rglru_h, state_rglru_conv, state_ssd_h, state_ssd_conv, state_ffn_conv, g_mix, w_in, rg_conv_w, rg_conv_b, rg_gate_a_w, rg_gate_a_b, rg_gate_i_w, rg_gate_i_b, rg_lambda, g_rg_out, ssd_conv_w, ssd_conv_b, ssd_dt_bias, ssd_A_log, ssd_D, g_ssd_norm, w_out, g_ffn, w_ffn_up, ffn_conv_w, ffn_conv_b, w_ffn_down, g_final):
    depth = g_mix.shape[0]
    pb, seq, d_model = x_prompt.shape
    sb, dec_seq, _ = x_sample.shape
    d_rnn = rg_lambda.shape[1]
    d_ssm = g_ssd_norm.shape[1]
    d_xbc = ssd_conv_w.shape[2]
    d_ff = ffn_conv_w.shape[2]
    heads = ssd_A_log.shape[1]
    headdim = d_ssm // heads
    n_state = state_ssd_h.shape[-1]
    n_main = 2 * d_rnn + d_ssm + d_xbc
    k_rg = rg_conv_w.shape[1]
    k_ssd = ssd_conv_w.shape[1]
    k_ffn = ffn_conv_w.shape[1]
    gn_w = SSD_GROUPS * n_state
    assert heads <= LANES and n_state == LANES and k_ffn == 3 and dec_seq >= max(k_rg, k_ssd) - 1

    hid = lax.broadcasted_iota(jnp.int32, (LANES, d_ssm), 0)
    cid = lax.broadcasted_iota(jnp.int32, (LANES, d_ssm), 1)
    e_sel = (hid == cid // headdim).astype(BF16)
    g_exp = (hid == cid // (d_ssm // SSD_GROUPS)).astype(BF16)
    g_sum = (lax.broadcasted_iota(jnp.int32, (gn_w, LANES), 0) // n_state
             == lax.broadcasted_iota(jnp.int32, (gn_w, LANES), 1)).astype(BF16)

    yp = x_prompt.reshape(pb * seq, d_model)
    ys = jnp.transpose(x_sample, (1, 0, 2)).reshape(dec_seq * sb, d_model)
    p_new = [[], [], [], [], []]
    s_new = [[], [], [], [], []]
    row = lambda v: v.reshape(1, -1)
    swap = lambda a3: jnp.transpose(a3, (1, 0, 2))
    bb = 32

    for l in range(depth):
        wdt_b = jnp.pad(w_in[l][:, n_main:], ((0, 0), (0, LANES - heads))).astype(BF16)
        wa_b = rg_gate_a_w[l].astype(BF16)
        wi_b = rg_gate_i_w[l].astype(BF16)
        rg_args = (rg_conv_w[l], row(rg_conv_b[l]), wa_b, row(rg_gate_a_b[l]), wi_b,
                   row(rg_gate_i_b[l]), row(rg_lambda[l]), row(g_rg_out[l]))
        dtb = _pad_lanes(ssd_dt_bias[l])
        alog = _pad_lanes(ssd_A_log[l])
        dexp = jnp.repeat(ssd_D[l].astype(F32), headdim).reshape(1, d_ssm)
        gn = row(g_ssd_norm[l])

        w_in_b = w_in[l].astype(BF16)
        ms = ys.shape[0]
        proj_s, dt_s = _in_proj(ys, row(g_mix[l]), w_in_b, wdt_b, n_main, tm=ms, tn=TN_IN)

        proj_p, dt_p = _in_proj(yp, row(g_mix[l]), w_in_b, wdt_b, n_main, tm=TM, tn=TN_IN)
        rg_steps = pb * (seq // TT_RG)
        ssd_steps = pb * (seq // SSD_CHUNK)
        rb_ssd = d_model // (2 * rg_steps + ssd_steps)
        assert rb_ssd % 16 == 0 and (2 * rg_steps + ssd_steps) * rb_ssd == d_model
        rg_p, hl_p, w_up_ta = _rglru_prompt(proj_p, pb, seq, d_rnn, *rg_args, w_ffn_up[l], TN_UP,
                                            2 * rb_ssd, tt=TT_RG)
        ssd_p, hfin_p, w_up_tb, w_out_b = _ssd_prompt(
            proj_p, dt_p, pb, seq, d_ssm, d_xbc, n_state, headdim, ssd_conv_w[l], row(ssd_conv_b[l]),
            dtb, alog, dexp, gn, e_sel, w_ffn_up[l], TN_UP, rb_ssd, 2 * rb_ssd * rg_steps, w_out[l])
        x1_p, h2_p, ss_p = _out_proj(rg_p, ssd_p, w_out_b, yp, row(g_ffn[l]), tm=TM, tn=TN_OUT)
        act_p, gst_p, w_down_t = _ffn_up_prompt(h2_p, ss_p, w_up_ta, w_up_tb, ffn_conv_w[l],
                                                row(ffn_conv_b[l]), seq, w_ffn_down[l], TN_DOWN, tm=TM)
        proj_p3 = proj_p.reshape(pb, seq, n_main)
        p_new[0].append(hl_p[:, SUBLANES - 1, :])
        p_new[1].append(proj_p3[:, seq - (k_rg - 1):, :d_rnn].astype(F32))
        p_new[2].append(hfin_p.reshape(pb, heads, headdim, n_state))
        p_new[3].append(proj_p3[:, seq - (k_ssd - 1):, n_main - d_xbc:].astype(F32))
        tiles = seq // TM
        p_new[4].append(gst_p[tiles - 1::tiles, SUBLANES - (k_ffn - 1):, :])

        proj_s3 = proj_s.reshape(dec_seq, sb, n_main)
        rg_s, hl_s = _rglru_sample(proj_s3, swap(state_rglru_conv[l]), state_rglru_h[l], d_rnn,
                                   *rg_args, bb=bb)
        ydp, ecum, xdd, bc, cd = _ssd_sample_pre(
            proj_s3, swap(state_ssd_conv[l]), dt_s.reshape(dec_seq, sb, LANES), d_ssm, d_xbc, n_state,
            ssd_conv_w[l], row(ssd_conv_b[l]), dtb, alog, dexp, e_sel, g_sum, g_exp, bb=bb)
        bc_b = swap(bc)
        snew, yoff_b = _ssd_sample_state(state_ssd_h[l], bc_b[:, :, gn_w:], bc_b[:, :, :gn_w],
                                         swap(xdd), cd, bt=8)
        ssd_s = _ssd_sample_post(ydp, swap(yoff_b), ecum, proj_s3, gn, d_ssm, d_xbc, bb=bb)
        x1_s, h2_s, ss_s = _out_proj(rg_s.reshape(ms, d_rnn), ssd_s.reshape(ms, d_ssm), w_out_b, ys,
                                     row(g_ffn[l]), tm=ms, tn=TN_OUT)
        act_s, gst_s = _ffn_up_sample(h2_s, ss_s, w_up_ta, w_up_tb, ffn_conv_w[l], row(ffn_conv_b[l]),
                                      swap(state_ffn_conv[l]).reshape(-1, d_ff), sb)
        s_new[0].append(hl_s)
        s_new[1].append(swap(proj_s3[dec_seq - (k_rg - 1):, :, :d_rnn]).astype(F32))
        s_new[2].append(snew)
        s_new[3].append(swap(proj_s3[dec_seq - (k_ssd - 1):, :, n_main - d_xbc:]).astype(F32))
        s_new[4].append(swap(gst_s.reshape(k_ffn - 1, sb, d_ff)))

        last = l == depth - 1
        yp = _ffn_down(act_p, w_down_t, x1_p, row(g_final), last, tm=TM, tk=TK_DOWN)
        ys = _ffn_down(act_s, w_down_t, x1_s, row(g_final), last, tm=ms, tk=TK_DOWN)

    y_prompt = yp.reshape(pb, seq, d_model)
    y_sample = jnp.transpose(ys.reshape(dec_seq, sb, d_model), (1, 0, 2))
    return (y_prompt, y_sample,
            jnp.stack(p_new[0]), jnp.stack(p_new[1]), jnp.stack(p_new[2]), jnp.stack(p_new[3]),
            jnp.stack(p_new[4]),
            jnp.stack(s_new[0]), jnp.stack(s_new[1]), jnp.stack(s_new[2]), jnp.stack(s_new[3]),
            jnp.stack(s_new[4]))
```

```python
import functools

import jax
import jax.numpy as jnp
from jax import lax
from jax.experimental import pallas as pl
from jax.experimental.pallas import tpu as pltpu

F32 = jnp.float32
BF16 = jnp.bfloat16

EPS = 1e-6
RG_C = 8.0
SSD_CHUNK = 128
SSD_GROUPS = 4
LANES = 128
SUBLANES = 8
VMEM_LIMIT = 56 * 2**20

TM = 1024
ROW_CHUNK = 256
TT_RG = 256
TN_IN = 512
TN_OUT = 512
TN_UP = 512
TN_DOWN = 512
TK_DOWN = 4096


def _params(n_axes):
    return pltpu.CompilerParams(dimension_semantics=("arbitrary",) * n_axes,
                                vmem_limit_bytes=VMEM_LIMIT)


def _dot(a, b):
    return jnp.dot(a, b, preferred_element_type=F32)


def _rms_rows(x, g):
    ms = jnp.mean(x * x, axis=-1, keepdims=True)
    return (x * lax.rsqrt(ms + EPS)) * g


def _split3(x):
    hi = x.astype(BF16)
    r = x - hi.astype(F32)
    mid = r.astype(BF16)
    lo = (r - mid.astype(F32)).astype(BF16)
    return hi, mid, lo


def _dot_sel_right(x, sel):
    hi, mid, lo = _split3(x)
    return (_dot(hi, sel) + _dot(mid, sel)) + _dot(lo, sel)


def _dot_sel_left(sel, x):
    hi, mid, lo = _split3(x)
    return (_dot(sel, hi) + _dot(sel, mid)) + _dot(sel, lo)


def _inproj_kernel(x_hbm, g_ref, w_ref, wdt_ref, o_ref, dt_ref, x_buf, hn_ref, sem, *, ni, tm):
    i = pl.program_id(0)
    j = pl.program_id(1)

    def x_in(tile):
        return pltpu.make_async_copy(x_hbm.at[pl.ds(tile * tm, tm)], x_buf, sem)

    @pl.when(j == 0)
    def _():
        @pl.when(i == 0)
        def _():
            x_in(0).start()

        x_in(i).wait()
        for r in range(0, tm, ROW_CHUNK):
            hn = _rms_rows(x_buf[r:r + ROW_CHUNK, :], g_ref[...]).astype(BF16)
            hn_ref[r:r + ROW_CHUNK, :] = hn
            dt_ref[r:r + ROW_CHUNK, :] = _dot(hn, wdt_ref[...])
            o_ref[r:r + ROW_CHUNK, :] = _dot(hn, w_ref[...])

        @pl.when(i + 1 < ni)
        def _():
            x_in(i + 1).start()

    @pl.when(j > 0)
    def _():
        o_ref[...] = _dot(hn_ref[...], w_ref[...])


def _cast_col_tiles(src_ref, dst_ref):
    n_tiles, _, tn = dst_ref.shape
    for t in range(n_tiles):
        dst_ref[t] = src_ref[:, t * tn:(t + 1) * tn].astype(BF16)


def _in_proj(x2d, g, w_b, wdt_b, n_main, *, tm, tn):
    m, d = x2d.shape
    ni = m // tm
    return pl.pallas_call(
        functools.partial(_inproj_kernel, ni=ni, tm=tm),
        grid=(ni, n_main // tn),
        in_specs=[pl.BlockSpec(memory_space=pl.ANY),
                  pl.BlockSpec((1, d), lambda i, j: (0, 0)),
                  pl.BlockSpec((d, tn), lambda i, j: (0, j)),
                  pl.BlockSpec((d, LANES), lambda i, j: (0, 0))],
        out_specs=[pl.BlockSpec((tm, tn), lambda i, j: (i, j)),
                   pl.BlockSpec((tm, LANES), lambda i, j: (i, 0))],
        out_shape=[jax.ShapeDtypeStruct((m, n_main), F32), jax.ShapeDtypeStruct((m, LANES), F32)],
        scratch_shapes=[pltpu.VMEM((tm, d), F32), pltpu.VMEM((tm, d), BF16), pltpu.SemaphoreType.DMA(())],
        compiler_params=_params(2),
        name="in_proj",
    )(x2d, g, w_b, wdt_b)


def _outproj_kernel(rg_ref, ssd_ref, wt_ref, wb_ref, x_ref, g_ref, x1_ref, xg_ref, ss_ref):
    j = pl.program_id(1)
    x1 = x_ref[...] + (_dot(rg_ref[...], wt_ref[...]) + _dot(ssd_ref[...], wb_ref[...]))
    x1_ref[...] = x1
    xg_ref[...] = (x1 * g_ref[...]).astype(BF16)
    part = jnp.broadcast_to(jnp.sum(x1 * x1, axis=-1, keepdims=True), ss_ref.shape)

    @pl.when(j == 0)
    def _():
        ss_ref[...] = part

    @pl.when(j > 0)
    def _():
        ss_ref[...] += part


def _out_proj(rg, ssd, w_b, x2d, g, *, tm, tn):
    m, d = x2d.shape
    k_half = rg.shape[1]
    return pl.pallas_call(
        _outproj_kernel,
        grid=(m // tm, d // tn),
        in_specs=[pl.BlockSpec((tm, k_half), lambda i, j: (i, 0)),
                  pl.BlockSpec((tm, k_half), lambda i, j: (i, 0)),
                  pl.BlockSpec((k_half, tn), lambda i, j: (0, j)),
                  pl.BlockSpec((k_half, tn), lambda i, j: (1, j)),
                  pl.BlockSpec((tm, tn), lambda i, j: (i, j)),
                  pl.BlockSpec((1, tn), lambda i, j: (0, j))],
        out_specs=[pl.BlockSpec((tm, tn), lambda i, j: (i, j)),
                   pl.BlockSpec((tm, tn), lambda i, j: (i, j)),
                   pl.BlockSpec((tm, LANES), lambda i, j: (i, 0))],
        out_shape=[jax.ShapeDtypeStruct((m, d), F32),
                   jax.ShapeDtypeStruct((m, d), BF16),
                   jax.ShapeDtypeStruct((m, LANES), F32)],
        compiler_params=_params(2),
        name="out_proj",
    )(rg, ssd, w_b, w_b, x2d, g)


def _glu(gate_c, val):
    return (jax.nn.gelu(gate_c) * val).astype(BF16)


def _dot_split_k(h_ref, wa_ref, wb_ref):
    ka = wa_ref.shape[0]
    return _dot(h_ref[:, :ka], wa_ref[...]) + _dot(h_ref[:, ka:], wb_ref[...])


def _inv_rms(ss_ref, width):
    return lax.rsqrt(ss_ref[:, 0:1] / width + EPS)


def _ffn_up_prompt_kernel(xg_ref, ss_ref, wga_ref, wgb_ref, wva_ref, wvb_ref, cw_ref, cb_ref, wsrc_ref,
                          act_ref, gst_ref, wdst_ref, carry_ref, *, tiles_per_seq):
    i = pl.program_id(0)
    j = pl.program_id(1)
    _cast_col_tiles(wsrc_ref, wdst_ref)
    inv = _inv_rms(ss_ref, xg_ref.shape[1])
    gate = _dot_split_k(xg_ref, wga_ref, wgb_ref) * inv
    val = _dot_split_k(xg_ref, wva_ref, wvb_ref) * inv
    tm, tn = gate.shape
    w0, w1, w2 = cw_ref[0:1, :], cw_ref[1:2, :], cw_ref[2:3, :]
    cb = cb_ref[...]
    g1 = pltpu.roll(gate, 1, 0)
    g2 = pltpu.roll(gate, 2, 0)
    act_ref[...] = _glu(((cb + g2 * w0) + g1 * w1) + gate * w2, val)
    @pl.when((i % tiles_per_seq) == 0)
    def _():
        carry_ref[j] = jnp.zeros((SUBLANES, tn), F32)

    prev = carry_ref[j]
    rows = lax.broadcasted_iota(jnp.int32, (SUBLANES, tn), 0)
    t1 = jnp.where(rows < 1, pltpu.roll(prev, 1, 0), g1[0:SUBLANES])
    t2 = jnp.where(rows < 2, pltpu.roll(prev, 2, 0), g2[0:SUBLANES])
    act_ref[0:SUBLANES, :] = _glu(((cb + t2 * w0) + t1 * w1) + gate[0:SUBLANES] * w2,
                                  val[0:SUBLANES])
    last = gate[tm - SUBLANES:tm]
    carry_ref[j] = last
    gst_ref[...] = last


def _ffn_up_prompt(h2, ss, w_ta, w_tb, cw, cb, seq, w_next, tn_next, *, tm):
    m, d = h2.shape
    nj, ka, tn = w_ta.shape[0] // 2, w_ta.shape[1], w_ta.shape[2]
    kb = w_tb.shape[1]
    assert ka + kb == d and ka % LANES == 0
    d_ff = nj * tn
    steps = (m // tm) * nj
    k_next, n_next = w_next.shape
    rb = k_next // steps
    assert rb * steps == k_next and rb % 16 == 0 and n_next % tn_next == 0
    return pl.pallas_call(
        functools.partial(_ffn_up_prompt_kernel, tiles_per_seq=seq // tm),
        grid=(m // tm, nj),
        in_specs=[pl.BlockSpec((tm, d), lambda i, j: (i, 0)),
                  pl.BlockSpec((tm, LANES), lambda i, j: (i, 0)),
                  pl.BlockSpec((None, ka, tn), lambda i, j: (j, 0, 0)),
                  pl.BlockSpec((None, kb, tn), lambda i, j: (j, 0, 0)),
                  pl.BlockSpec((None, ka, tn), lambda i, j: (j + nj, 0, 0)),
                  pl.BlockSpec((None, kb, tn), lambda i, j: (j + nj, 0, 0)),
                  pl.BlockSpec((3, tn), lambda i, j: (0, j)),
                  pl.BlockSpec((1, tn), lambda i, j: (0, j)),
                  pl.BlockSpec((rb, n_next), lambda i, j: (i * nj + j, 0))],
        out_specs=[pl.BlockSpec((tm, tn), lambda i, j: (i, j)),
                   pl.BlockSpec((None, SUBLANES, tn), lambda i, j: (i, 0, j)),
                   pl.BlockSpec((n_next // tn_next, rb, tn_next), lambda i, j: (0, i * nj + j, 0))],
        out_shape=[jax.ShapeDtypeStruct((m, d_ff), BF16),
                   jax.ShapeDtypeStruct((m // tm, SUBLANES, d_ff), F32),
                   jax.ShapeDtypeStruct((n_next // tn_next, k_next, tn_next), BF16)],
        scratch_shapes=[pltpu.VMEM((nj, SUBLANES, tn), F32)],
        compiler_params=_params(2),
        name="ffn_up_prompt",
    )(h2, ss, w_ta, w_tb, w_ta, w_tb, cw, cb, w_next)


def _ffn_up_sample_kernel(xg_ref, ss_ref, wga_ref, wgb_ref, wva_ref, wvb_ref, cw_ref, cb_ref, st_ref,
                          act_ref, nst_ref, *, nb, nt):
    inv = _inv_rms(ss_ref, xg_ref.shape[1])
    gate = _dot_split_k(xg_ref, wga_ref, wgb_ref) * inv
    val = _dot_split_k(xg_ref, wva_ref, wvb_ref) * inv
    w0, w1, w2 = cw_ref[0:1, :], cw_ref[1:2, :], cw_ref[2:3, :]
    cb = cb_ref[...]
    ext = [st_ref[0:nb, :], st_ref[nb:2 * nb, :]] + [gate[t * nb:(t + 1) * nb] for t in range(nt)]
    for t in range(nt):
        gc = ((cb + ext[t] * w0) + ext[t + 1] * w1) + ext[t + 2] * w2
        act_ref[t * nb:(t + 1) * nb, :] = _glu(gc, val[t * nb:(t + 1) * nb])
    nst_ref[0:nb, :] = ext[nt]
    nst_ref[nb:2 * nb, :] = ext[nt + 1]


def _ffn_up_sample(h2, ss, w_ta, w_tb, cw, cb, st, nb):
    m, d = h2.shape
    nj, ka, tn = w_ta.shape[0] // 2, w_ta.shape[1], w_ta.shape[2]
    kb = w_tb.shape[1]
    d_ff = nj * tn
    return pl.pallas_call(
        functools.partial(_ffn_up_sample_kernel, nb=nb, nt=m // nb),
        grid=(nj,),
        in_specs=[pl.BlockSpec((m, d), lambda j: (0, 0)),
                  pl.BlockSpec((m, LANES), lambda j: (0, 0)),
                  pl.BlockSpec((None, ka, tn), lambda j: (j, 0, 0)),
                  pl.BlockSpec((None, kb, tn), lambda j: (j, 0, 0)),
                  pl.BlockSpec((None, ka, tn), lambda j: (j + nj, 0, 0)),
                  pl.BlockSpec((None, kb, tn), lambda j: (j + nj, 0, 0)),
                  pl.BlockSpec((3, tn), lambda j: (0, j)),
                  pl.BlockSpec((1, tn), lambda j: (0, j)),
                  pl.BlockSpec((2 * nb, tn), lambda j: (0, j))],
        out_specs=[pl.BlockSpec((m, tn), lambda j: (0, j)),
                   pl.BlockSpec((2 * nb, tn), lambda j: (0, j))],
        out_shape=[jax.ShapeDtypeStruct((m, d_ff), BF16),
                   jax.ShapeDtypeStruct((2 * nb, d_ff), F32)],
        compiler_params=_params(1),
        name="ffn_up_sample",
    )(h2, ss, w_ta, w_tb, w_ta, w_tb, cw, cb, st)


def _ffn_down_kernel(act_ref, w_ref, x1_ref, g_ref, y_hbm, rows_ref, sem,
                     *, ni, nk, nj, tm, tn, final_norm):
    i = pl.program_id(0)
    k = pl.program_id(1)
    j = pl.program_id(2)
    cols = pl.ds(pl.multiple_of(j * tn, tn), tn)

    def rows_out(tile):
        return pltpu.make_async_copy(rows_ref, y_hbm.at[pl.ds(tile * tm, tm)], sem)

    @pl.when(k == 0)
    def _():
        @pl.when((j == 0) & (i > 0))
        def _():
            rows_out(i - 1).wait()

        rows_ref[:, cols] = x1_ref[...]

    rows_ref[:, cols] += _dot(act_ref[...], w_ref[...])

    @pl.when((k == nk - 1) & (j == nj - 1))
    def _():
        if final_norm:
            for r in range(0, tm, ROW_CHUNK):
                rows_ref[r:r + ROW_CHUNK, :] = _rms_rows(rows_ref[r:r + ROW_CHUNK, :], g_ref[...])
        rows_out(i).start()

        @pl.when(i == ni - 1)
        def _():
            rows_out(i).wait()


def _ffn_down(act, w_t, x1, g, final_norm, *, tm, tk):
    m, d = x1.shape
    nj, d_ff, tn = w_t.shape
    ni, nk = m // tm, d_ff // tk
    return pl.pallas_call(
        functools.partial(_ffn_down_kernel, ni=ni, nk=nk, nj=nj, tm=tm, tn=tn, final_norm=final_norm),
        grid=(ni, nk, nj),
        in_specs=[pl.BlockSpec((tm, tk), lambda i, k, j: (i, k)),
                  pl.BlockSpec((None, tk, tn), lambda i, k, j: (j, k, 0)),
                  pl.BlockSpec((tm, tn), lambda i, k, j: (i, jnp.where(k == 0, j, nj - 1))),
                  pl.BlockSpec((1, d), lambda i, k, j: (0, 0))],
        out_specs=pl.BlockSpec(memory_space=pl.ANY),
        out_shape=jax.ShapeDtypeStruct((m, d), F32),
        scratch_shapes=[pltpu.VMEM((tm, d), F32), pltpu.SemaphoreType.DMA(())],
        compiler_params=_params(3),
        name="ffn_down",
    )(act, w_t, x1, g)


def _softplus(x):
    return jnp.maximum(x, 0.0) + jnp.log1p(jnp.exp(-jnp.abs(x)))


def _rglru_coeffs(xc, wa_ref, ba_ref, wi_ref, bi_ref, lam_ref):
    xb = xc.astype(BF16)
    heads, blk = wa_ref.shape[0], wa_ref.shape[1]
    ga = jnp.concatenate([_dot(xb[:, h * blk:(h + 1) * blk], wa_ref[h]) for h in range(heads)], axis=1)
    gi = jnp.concatenate([_dot(xb[:, h * blk:(h + 1) * blk], wi_ref[h]) for h in range(heads)], axis=1)
    gate_r = jax.nn.sigmoid(ga + ba_ref[...])
    gate_i = jax.nn.sigmoid(gi + bi_ref[...])
    log_a = (-RG_C * gate_r) * _softplus(-lam_ref[...])
    a = jnp.exp(log_a)
    bx = jnp.sqrt(-jnp.tanh(log_a) * (a * a + 1.0)) * (gate_i * xc)
    return a, bx


def _rg_out(h, gate, g_ref):
    return _rms_rows(h * jax.nn.gelu(gate), g_ref[...]).astype(BF16)


def _conv_tile(x, prev, w_ref, b_ref, k_taps):
    t_len, width = x.shape
    groups = t_len // SUBLANES
    x3 = x.reshape(groups, SUBLANES, width)
    pos = lax.broadcasted_iota(jnp.int32, x3.shape, 1)
    out = b_ref[...].reshape(1, 1, width)
    for k in range(k_taps):
        s = k_taps - 1 - k
        wk = w_ref[k:k + 1, :].reshape(1, 1, width)
        if s == 0:
            shifted = x3
        else:
            rot = pltpu.roll(x3, s, 1)
            rot_prev = jnp.concatenate([pltpu.roll(prev, s, 0)[None], rot[:groups - 1]], axis=0)
            shifted = jnp.where(pos >= s, rot, rot_prev)
        out = out + shifted * wk
    return out.reshape(t_len, width)


def _rglru_prompt_kernel(x_ref, gate_ref, cw_ref, cb_ref, wa_ref, ba_ref, wi_ref, bi_ref, lam_ref,
                         g_ref, wsrc_ref, out_ref, hlast_ref, wdst_ref, xprev_ref, hc_ref, a_s, b_s, h_s,
                         *, k_taps):
    c = pl.program_id(1)
    _cast_col_tiles(wsrc_ref, wdst_ref)

    @pl.when(c == 0)
    def _():
        xprev_ref[...] = jnp.zeros_like(xprev_ref)
        hc_ref[...] = jnp.zeros_like(hc_ref)

    x = x_ref[...]
    t_len, width = x.shape
    xc = _conv_tile(x, xprev_ref[...], cw_ref, cb_ref, k_taps)
    xprev_ref[...] = x[t_len - SUBLANES:t_len]
    a, b = _rglru_coeffs(xc, wa_ref, ba_ref, wi_ref, bi_ref, lam_ref)
    groups = t_len // SUBLANES
    a = a.reshape(groups, SUBLANES, width)
    b = b.reshape(groups, SUBLANES, width)
    pos = lax.broadcasted_iota(jnp.int32, a.shape, 1)
    for s in (1, 2, 4):
        a_sh = pltpu.roll(a, s, 1)
        b_sh = pltpu.roll(b, s, 1)
        m = pos >= s
        b = jnp.where(m, a * b_sh + b, b)
        a = jnp.where(m, a * a_sh, a)
    a_s[...] = a.reshape(t_len, width)
    b_s[...] = b.reshape(t_len, width)
    h = hc_ref[...]
    for grp in range(t_len // SUBLANES):
        lo, hi = grp * SUBLANES, (grp + 1) * SUBLANES
        hl = jnp.broadcast_to(h[SUBLANES - 1:SUBLANES, :], (SUBLANES, width))
        h = a_s[lo:hi, :] * hl + b_s[lo:hi, :]
        h_s[lo:hi, :] = h
    hc_ref[...] = h
    hlast_ref[...] = h
    out_ref[...] = _rg_out(h_s[...], gate_ref[...], g_ref)


def _rglru_prompt(proj, nb, seq, d_rnn, cw, cb, wa, ba, wi, bi, lam, g, w_next, tn_next, rb, *, tt):
    nc = seq // tt
    heads, blk = wa.shape[0], wa.shape[1]
    k_next, n_next = w_next.shape
    vec = pl.BlockSpec((1, d_rnn), lambda b, c: (0, 0))
    gate_w = pl.BlockSpec((heads, blk, blk), lambda b, c: (0, 0, 0))
    return pl.pallas_call(
        functools.partial(_rglru_prompt_kernel, k_taps=cw.shape[0]),
        grid=(nb, nc),
        in_specs=[pl.BlockSpec((tt, d_rnn), lambda b, c: (b * nc + c, 0)),
                  pl.BlockSpec((tt, d_rnn), lambda b, c: (b * nc + c, 1)),
                  pl.BlockSpec((cw.shape[0], d_rnn), lambda b, c: (0, 0)),
                  vec, gate_w, vec, gate_w, vec, vec, vec,
                  pl.BlockSpec((rb, n_next), lambda b, c: (b * nc + c, 0))],
        out_specs=[pl.BlockSpec((tt, d_rnn), lambda b, c: (b * nc + c, 0)),
                   pl.BlockSpec((None, SUBLANES, d_rnn), lambda b, c: (b, 0, 0)),
                   pl.BlockSpec((n_next // tn_next, rb, tn_next), lambda b, c: (0, b * nc + c, 0))],
        out_shape=[jax.ShapeDtypeStruct((nb * seq, d_rnn), BF16),
                   jax.ShapeDtypeStruct((nb, SUBLANES, d_rnn), F32),
                   jax.ShapeDtypeStruct((n_next // tn_next, nb * nc * rb, tn_next), BF16)],
        scratch_shapes=[pltpu.VMEM((SUBLANES, d_rnn), F32), pltpu.VMEM((SUBLANES, d_rnn), F32),
                        pltpu.VMEM((tt, d_rnn), F32), pltpu.VMEM((tt, d_rnn), F32),
                        pltpu.VMEM((tt, d_rnn), F32)],
        compiler_params=_params(2),
        name="rglru_prompt",
    )(proj, proj, cw, cb, wa, ba, wi, bi, lam, g, w_next)


def _conv_slabs(st_ref, x_ref, w_ref, b_ref):
    k_taps = w_ref.shape[0]
    ext = [st_ref[s] for s in range(k_taps - 1)] + [x_ref[t] for t in range(x_ref.shape[0])]
    convs = []
    for t in range(x_ref.shape[0]):
        acc = b_ref[...]
        for k in range(k_taps):
            acc = acc + ext[t + k] * w_ref[k:k + 1, :]
        convs.append(acc)
    return jnp.concatenate(convs, axis=0)


def _rglru_sample_kernel(x_ref, gate_ref, st_ref, h0_ref, cw_ref, cb_ref, wa_ref, ba_ref, wi_ref,
                         bi_ref, lam_ref, g_ref, out_ref, hlast_ref, h_s):
    nt, bb, width = x_ref.shape
    xc = _conv_slabs(st_ref, x_ref, cw_ref, cb_ref)
    a, b = _rglru_coeffs(xc, wa_ref, ba_ref, wi_ref, bi_ref, lam_ref)
    h = h0_ref[...]
    for t in range(nt):
        h = a[t * bb:(t + 1) * bb] * h + b[t * bb:(t + 1) * bb]
        h_s[t * bb:(t + 1) * bb, :] = h
    hlast_ref[...] = h
    gate = gate_ref[...].reshape(nt * bb, width)
    out_ref[...] = _rg_out(h_s[...], gate, g_ref).reshape(nt, bb, width)


def _rglru_sample(proj3, st3, h0, d_rnn, cw, cb, wa, ba, wi, bi, lam, g, *, bb):
    nt, nb, _ = proj3.shape
    heads, blk = wa.shape[0], wa.shape[1]
    k_taps = cw.shape[0]
    vec = pl.BlockSpec((1, d_rnn), lambda i: (0, 0))
    gate_w = pl.BlockSpec((heads, blk, blk), lambda i: (0, 0, 0))
    return pl.pallas_call(
        _rglru_sample_kernel,
        grid=(nb // bb,),
        in_specs=[pl.BlockSpec((nt, bb, d_rnn), lambda i: (0, i, 0)),
                  pl.BlockSpec((nt, bb, d_rnn), lambda i: (0, i, 1)),
                  pl.BlockSpec((k_taps - 1, bb, d_rnn), lambda i: (0, i, 0)),
                  pl.BlockSpec((bb, d_rnn), lambda i: (i, 0)),
                  pl.BlockSpec((k_taps, d_rnn), lambda i: (0, 0)),
                  vec, gate_w, vec, gate_w, vec, vec, vec],
        out_specs=[pl.BlockSpec((nt, bb, d_rnn), lambda i: (0, i, 0)),
                   pl.BlockSpec((bb, d_rnn), lambda i: (i, 0))],
        out_shape=[jax.ShapeDtypeStruct((nt, nb, d_rnn), BF16),
                   jax.ShapeDtypeStruct((nb, d_rnn), F32)],
        scratch_shapes=[pltpu.VMEM((nt * bb, d_rnn), F32)],
        compiler_params=_params(1),
        name="rglru_sample",
    )(proj3, proj3, st3, h0, cw, cb, wa, ba, wi, bi, lam, g)


def _silu(x):
    return x * jax.nn.sigmoid(x)


def _group_norm_out(y, z, gn_ref, groups):
    u = y * _silu(z)
    gw = u.shape[1] // groups
    outs = []
    for g in range(groups):
        ug = u[:, g * gw:(g + 1) * gw]
        ms = jnp.mean(ug * ug, axis=-1, keepdims=True)
        outs.append((ug * lax.rsqrt(ms + EPS)) * gn_ref[:, g * gw:(g + 1) * gw])
    return jnp.concatenate(outs, axis=1).astype(BF16)


def _ssd_prompt_kernel(z_ref, xbc_ref, dt_ref, cw_ref, cb_ref, dtb_ref, alog_ref, dexp_ref, gn_ref,
                       e_ref, wsrc_ref, wsrc2_ref, out_ref, hfin_ref, wdst_ref, wdst2_ref,
                       carry_ref, s_ref, *, k_taps, d_ssm, n_state, groups, headdim, nc):
    c = pl.program_id(1)
    _cast_col_tiles(wsrc_ref, wdst_ref)
    wdst2_ref[...] = wsrc2_ref[...].astype(BF16)

    @pl.when(c == 0)
    def _():
        carry_ref[...] = jnp.zeros_like(carry_ref)
        s_ref[...] = jnp.zeros_like(s_ref)

    xbc = xbc_ref[...]
    q = xbc.shape[0]
    xc = _silu(_conv_tile(xbc, carry_ref[...], cw_ref, cb_ref, k_taps))
    carry_ref[...] = xbc[q - SUBLANES:q]
    gn_w = groups * n_state
    bm = xc[:, d_ssm:d_ssm + gn_w]
    cm = xc[:, d_ssm + gn_w:d_ssm + 2 * gn_w]

    dt = _softplus(dt_ref[...] + dtb_ref[...])
    dta = dt * (-jnp.exp(alog_ref[...]))
    row = lax.broadcasted_iota(jnp.int32, (q, q), 0)
    col = lax.broadcasted_iota(jnp.int32, (q, q), 1)
    causal = row >= col
    tril = jnp.where(causal, 1.0, 0.0).astype(BF16)
    cum = _dot_sel_left(tril, dta)
    cum_t = cum.T

    heads_per_group = d_ssm // headdim // groups
    pair_w = 2 * headdim
    gw = heads_per_group * headdim
    lane = lax.broadcasted_iota(jnp.int32, (q, pair_w), 1)
    for g in range(groups):
        gs = slice(g * gw, (g + 1) * gw)
        xs = xc[:, gs]
        e = e_ref[:, gs]
        dt_e = _dot_sel_right(dt, e)
        cum_e = _dot_sel_right(cum, e)
        xdt = xs * dt_e
        cum_last = cum_e[q - 1:q, :]
        xdd = (xdt * jnp.exp(cum_last - cum_e)).astype(BF16)
        cg = cm[:, g * n_state:(g + 1) * n_state].astype(BF16)
        bg32 = bm[:, g * n_state:(g + 1) * n_state]
        bg = bg32.astype(BF16)
        cb_g = lax.dot_general(cg, bg, (((1,), (1,)), ((), ())), preferred_element_type=F32)
        s_old = s_ref[:, gs]
        y_state = _dot(cg, s_old.astype(BF16)) * jnp.exp(cum_e)
        s_ref[:, gs] = s_old * jnp.exp(cum_last) + _dot(bg32.T.astype(BF16), xdd)
        y_pairs = []
        for pr in range(heads_per_group // 2):
            h0 = g * heads_per_group + 2 * pr
            ms = []
            for h in (h0, h0 + 1):
                seg = (jnp.broadcast_to(cum[:, h:h + 1], (q, q))
                       - jnp.broadcast_to(cum_t[h:h + 1, :], (q, q)))
                ms.append(cb_g * jnp.exp(jnp.where(causal, seg, -jnp.inf)))
            xp = xdt[:, 2 * pr * headdim:2 * pr * headdim + pair_w]
            lhs = jnp.concatenate(ms, axis=1).astype(BF16)
            rhs = jnp.concatenate([jnp.where(lane < headdim, xp, 0.0),
                                   jnp.where(lane >= headdim, xp, 0.0)], axis=0).astype(BF16)
            y_pairs.append(_dot(lhs, rhs))
        y = (y_state + jnp.concatenate(y_pairs, axis=1)) + dexp_ref[:, gs] * xs
        u = y * _silu(z_ref[:, gs])
        ms_g = jnp.mean(u * u, axis=-1, keepdims=True)
        out_ref[:, gs] = ((u * lax.rsqrt(ms_g + EPS)) * gn_ref[:, gs]).astype(BF16)

    @pl.when(c == nc - 1)
    def _():
        for blk in range(d_ssm // LANES):
            hfin_ref[blk * LANES:(blk + 1) * LANES, :] = s_ref[:, blk * LANES:(blk + 1) * LANES].T


def _ssd_prompt(proj, dt_raw, nb, seq, d_ssm, d_xbc, n_state, headdim, cw, cb, dtb, alog, dexp, gn, e,
                w_next, tn_next, rb, row0, w_plain):
    q = SSD_CHUNK
    nc = seq // q
    k_plain, n_plain = w_plain.shape
    rb2 = k_plain // (nb * nc)
    assert rb2 * nb * nc == k_plain and rb2 % 16 == 0
    z_blk = (proj.shape[1] - d_xbc - d_ssm) // d_ssm
    x_blk = (proj.shape[1] - d_xbc) // d_xbc
    assert z_blk * d_ssm + d_ssm + d_xbc == proj.shape[1] and x_blk * d_xbc + d_xbc == proj.shape[1]
    k_next, n_next = w_next.shape
    n_tiles = n_next // tn_next
    blk0 = row0 // rb
    assert blk0 * rb == row0 and row0 + nb * nc * rb == k_next
    one = lambda w: pl.BlockSpec((1, w), lambda b, c: (0, 0))
    return pl.pallas_call(
        functools.partial(_ssd_prompt_kernel, k_taps=cw.shape[0], d_ssm=d_ssm, n_state=n_state,
                          groups=SSD_GROUPS, headdim=headdim, nc=nc),
        grid=(nb, nc),
        in_specs=[pl.BlockSpec((q, d_ssm), lambda b, c: (b * nc + c, z_blk)),
                  pl.BlockSpec((q, d_xbc), lambda b, c: (b * nc + c, x_blk)),
                  pl.BlockSpec((q, LANES), lambda b, c: (b * nc + c, 0)),
                  pl.BlockSpec((cw.shape[0], d_xbc), lambda b, c: (0, 0)),
                  one(d_xbc), one(LANES), one(LANES), one(d_ssm), one(d_ssm),
                  pl.BlockSpec((LANES, d_ssm), lambda b, c: (0, 0)),
                  pl.BlockSpec((rb, n_next), lambda b, c: (blk0 + b * nc + c, 0)),
                  pl.BlockSpec((rb2, n_plain), lambda b, c: (b * nc + c, 0))],
        out_specs=[pl.BlockSpec((q, d_ssm), lambda b, c: (b * nc + c, 0)),
                   pl.BlockSpec((None, d_ssm, n_state), lambda b, c: (b, 0, 0)),
                   pl.BlockSpec((n_tiles, rb, tn_next), lambda b, c: (0, b * nc + c, 0)),
                   pl.BlockSpec((rb2, n_plain), lambda b, c: (b * nc + c, 0))],
        out_shape=[jax.ShapeDtypeStruct((nb * seq, d_ssm), BF16),
                   jax.ShapeDtypeStruct((nb, d_ssm, n_state), F32),
                   jax.ShapeDtypeStruct((n_tiles, k_next - row0, tn_next), BF16),
                   jax.ShapeDtypeStruct((k_plain, n_plain), BF16)],
        scratch_shapes=[pltpu.VMEM((SUBLANES, d_xbc), F32), pltpu.VMEM((n_state, d_ssm), F32)],
        compiler_params=_params(2),
        name="ssd_prompt",
    )(proj, proj, dt_raw, cw, cb, dtb, alog, dexp, gn, e, w_next, w_plain)


def _ssd_sample_pre_kernel(xbc_ref, st_ref, dt_ref, cw_ref, cb_ref, dtb_ref, alog_ref, dexp_ref,
                           e_ref, gsum_ref, gexp_ref, ydp_ref, ecum_ref, xdd_ref, bc_ref, cd_ref,
                           *, d_ssm, n_state, groups):
    nt, bb, _ = xbc_ref.shape
    xc = _silu(_conv_slabs(st_ref, xbc_ref, cw_ref, cb_ref))
    gn_w = groups * n_state
    xs = xc[:, :d_ssm]
    bm = xc[:, d_ssm:d_ssm + gn_w]
    cm = xc[:, d_ssm + gn_w:d_ssm + 2 * gn_w]
    bc_ref[...] = xc[:, d_ssm:d_ssm + 2 * gn_w].reshape(nt, bb, 2 * gn_w)

    dt = _softplus(dt_ref[...].reshape(nt * bb, LANES) + dtb_ref[...])
    dta = dt * (-jnp.exp(alog_ref[...]))
    cums = [dta[0:bb]]
    for t in range(1, nt):
        cums.append(cums[-1] + dta[t * bb:(t + 1) * bb])
    cum = jnp.concatenate(cums, axis=0)
    cd_ref[...] = jnp.exp(cums[-1])
    e = e_ref[...]
    dt_e = _dot_sel_right(dt, e)
    cum_e = _dot_sel_right(cum, e)
    ecum_ref[...] = jnp.exp(cum_e).reshape(nt, bb, d_ssm)
    xdt = xs * dt_e
    sl = lambda v, t: v[t * bb:(t + 1) * bb]
    cum_last = sl(cum_e, nt - 1)
    gsum, gexp = gsum_ref[...], gexp_ref[...]
    for t in range(nt):
        xdd_ref[t] = sl(xdt, t) * jnp.exp(cum_last - sl(cum_e, t))
        acc = dexp_ref[...] * sl(xs, t)
        for s in range(t + 1):
            cb_e = _dot_sel_right(_dot_sel_right(sl(cm, t) * sl(bm, s), gsum), gexp)
            acc = acc + (cb_e * jnp.exp(sl(cum_e, t) - sl(cum_e, s))) * sl(xdt, s)
        ydp_ref[t] = acc


def _ssd_sample_pre(proj3, st3, dt3, d_ssm, d_xbc, n_state, cw, cb, dtb, alog, dexp, e, gsum, gexp,
                    *, bb):
    nt, nb, n_main = proj3.shape
    k_taps = cw.shape[0]
    x_blk = (n_main - d_xbc) // d_xbc
    gn_w = SSD_GROUPS * n_state
    full = lambda r, w: pl.BlockSpec((r, w), lambda i: (0, 0))
    slab = lambda n, w: pl.BlockSpec((n, bb, w), lambda i: (0, i, 0))
    f32 = lambda *s: jax.ShapeDtypeStruct(s, F32)
    return pl.pallas_call(
        functools.partial(_ssd_sample_pre_kernel, d_ssm=d_ssm, n_state=n_state, groups=SSD_GROUPS),
        grid=(nb // bb,),
        in_specs=[pl.BlockSpec((nt, bb, d_xbc), lambda i: (0, i, x_blk)),
                  slab(k_taps - 1, d_xbc), slab(nt, LANES), full(k_taps, d_xbc),
                  full(1, d_xbc), full(1, LANES), full(1, LANES), full(1, d_ssm),
                  full(LANES, d_ssm), full(gn_w, LANES), full(LANES, d_ssm)],
        out_specs=[slab(nt, d_ssm), slab(nt, d_ssm), slab(nt, d_ssm), slab(nt, 2 * gn_w),
                   pl.BlockSpec((bb, LANES), lambda i: (i, 0))],
        out_shape=[f32(nt, nb, d_ssm), f32(nt, nb, d_ssm), f32(nt, nb, d_ssm),
                   f32(nt, nb, 2 * gn_w), f32(nb, LANES)],
        compiler_params=_params(1),
        name="ssd_sample_pre",
    )(proj3, st3, dt3, cw, cb, dtb, alog, dexp, e, gsum, gexp)


def _ssd_sample_state_kernel(s0_ref, c_ref, b_ref, xdd_ref, cd_ref, snew_ref, yoff_ref,
                             *, bt, groups, n_state, headdim):
    heads = s0_ref.shape[1]
    hpg = heads // groups
    gw = hpg * headdim

    def body(b, carry):
        cd_row = cd_ref[pl.ds(b, 1), :]
        cb = c_ref[b].astype(BF16)
        bb = b_ref[b].astype(BF16)
        xb = xdd_ref[b].astype(BF16)
        for g in range(groups):
            s0 = s0_ref[b, g * hpg:(g + 1) * hpg].reshape(gw, n_state)
            yoff = lax.dot_general(cb[:, g * n_state:(g + 1) * n_state], s0.astype(BF16),
                                   (((1,), (1,)), ((), ())), preferred_element_type=F32)
            yoff_ref[b, :, g * gw:(g + 1) * gw] = yoff
            upd = lax.dot_general(xb[:, g * gw:(g + 1) * gw], bb[:, g * n_state:(g + 1) * n_state],
                                  (((0,), (0,)), ((), ())), preferred_element_type=F32)
            dec = jnp.concatenate(
                [jnp.broadcast_to(cd_row[:, h:h + 1], (headdim, n_state))
                 for h in range(g * hpg, (g + 1) * hpg)], axis=0)
            snew_ref[b, g * hpg:(g + 1) * hpg] = (s0 * dec + upd).reshape(hpg, headdim, n_state)
        return carry

    lax.fori_loop(0, bt, body, 0)


def _ssd_sample_state(s0, c_b, b_b, xdd_b, cd, *, bt):
    nb, heads, headdim, n_state = s0.shape
    nt = c_b.shape[1]
    gn_w = c_b.shape[2]
    d_ssm = xdd_b.shape[2]
    return pl.pallas_call(
        functools.partial(_ssd_sample_state_kernel, bt=bt, groups=SSD_GROUPS, n_state=n_state,
                          headdim=headdim),
        grid=(nb // bt,),
        in_specs=[pl.BlockSpec((bt, heads, headdim, n_state), lambda i: (i, 0, 0, 0)),
                  pl.BlockSpec((bt, nt, gn_w), lambda i: (i, 0, 0)),
                  pl.BlockSpec((bt, nt, gn_w), lambda i: (i, 0, 0)),
                  pl.BlockSpec((bt, nt, d_ssm), lambda i: (i, 0, 0)),
                  pl.BlockSpec((bt, LANES), lambda i: (i, 0))],
        out_specs=[pl.BlockSpec((bt, heads, headdim, n_state), lambda i: (i, 0, 0, 0)),
                   pl.BlockSpec((bt, nt, d_ssm), lambda i: (i, 0, 0))],
        out_shape=[jax.ShapeDtypeStruct(s0.shape, F32),
                   jax.ShapeDtypeStruct((nb, nt, d_ssm), F32)],
        compiler_params=_params(1),
        name="ssd_sample_state",
    )(s0, c_b, b_b, xdd_b, cd)


def _ssd_sample_post_kernel(ydp_ref, yoff_ref, ecum_ref, z_ref, gn_ref, out_ref, *, groups):
    nt, bb, d_ssm = ydp_ref.shape
    y = (ydp_ref[...] + yoff_ref[...] * ecum_ref[...]).reshape(nt * bb, d_ssm)
    z = z_ref[...].reshape(nt * bb, d_ssm)
    out_ref[...] = _group_norm_out(y, z, gn_ref, groups).reshape(nt, bb, d_ssm)


def _ssd_sample_post(ydp, yoff, ecum, proj3, gn, d_ssm, d_xbc, *, bb):
    nt, nb, n_main = proj3.shape
    z_blk = (n_main - d_xbc - d_ssm) // d_ssm
    slab = pl.BlockSpec((nt, bb, d_ssm), lambda i: (0, i, 0))
    return pl.pallas_call(
        functools.partial(_ssd_sample_post_kernel, groups=SSD_GROUPS),
        grid=(nb // bb,),
        in_specs=[slab, slab, slab, pl.BlockSpec((nt, bb, d_ssm), lambda i: (0, i, z_blk)),
                  pl.BlockSpec((1, d_ssm), lambda i: (0, 0))],
        out_specs=slab,
        out_shape=jax.ShapeDtypeStruct((nt, nb, d_ssm), BF16),
        compiler_params=_params(1),
        name="ssd_sample_post",
    )(ydp, yoff, ecum, proj3, gn)


def _pad_lanes(v):
    return jnp.pad(v.astype(F32), (0, LANES - v.shape[0])).reshape(1, LANES)


def kernel(x_prompt, x_sample, state_rglru_h, state_rglru_conv, state_ssd_h, state_ssd_conv, state_ffn_conv, g_mix, w_in, rg_conv_w, rg_conv_b, rg_gate_a_w, rg_gate_a_b, rg_gate_i_w, rg_gate_i_b, rg_lambda, g_rg_out, ssd_conv_w, ssd_conv_b, ssd_dt_bias, ssd_A_log, ssd_D, g_ssd_norm, w_out, g_ffn, w_ffn_up, ffn_conv_w, ffn_conv_b, w_ffn_down, g_final):
    depth = g_mix.shape[0]
    pb, seq, d_model = x_prompt.shape
    sb, dec_seq, _ = x_sample.shape
    d_rnn = rg_lambda.shape[1]
    d_ssm = g_ssd_norm.shape[1]
    d_xbc = ssd_conv_w.shape[2]
    d_ff = ffn_conv_w.shape[2]
    heads = ssd_A_log.shape[1]
    headdim = d_ssm // heads
    n_state = state_ssd_h.shape[-1]
    n_main = 2 * d_rnn + d_ssm + d_xbc
    k_rg = rg_conv_w.shape[1]
    k_ssd = ssd_conv_w.shape[1]
    k_ffn = ffn_conv_w.shape[1]
    gn_w = SSD_GROUPS * n_state
    assert heads <= LANES and n_state == LANES and k_ffn == 3 and dec_seq >= max(k_rg, k_ssd) - 1

    hid = lax.broadcasted_iota(jnp.int32, (LANES, d_ssm), 0)
    cid = lax.broadcasted_iota(jnp.int32, (LANES, d_ssm), 1)
    e_sel = (hid == cid // headdim).astype(BF16)
    g_exp = (hid == cid // (d_ssm // SSD_GROUPS)).astype(BF16)
    g_sum = (lax.broadcasted_iota(jnp.int32, (gn_w, LANES), 0) // n_state
             == lax.broadcasted_iota(jnp.int32, (gn_w, LANES), 1)).astype(BF16)

    yp = x_prompt.reshape(pb * seq, d_model)
    ys = jnp.transpose(x_sample, (1, 0, 2)).reshape(dec_seq * sb, d_model)
    p_new = [[], [], [], [], []]
    s_new = [[], [], [], [], []]
    row = lambda v: v.reshape(1, -1)
    swap = lambda a3: jnp.transpose(a3, (1, 0, 2))
    bb = 32

    for l in range(depth):
        wdt_b = jnp.pad(w_in[l][:, n_main:], ((0, 0), (0, LANES - heads))).astype(BF16)
        wa_b = rg_gate_a_w[l].astype(BF16)
        wi_b = rg_gate_i_w[l].astype(BF16)
        rg_args = (rg_conv_w[l], row(rg_conv_b[l]), wa_b, row(rg_gate_a_b[l]), wi_b,
                   row(rg_gate_i_b[l]), row(rg_lambda[l]), row(g_rg_out[l]))
        dtb = _pad_lanes(ssd_dt_bias[l])
        alog = _pad_lanes(ssd_A_log[l])
        dexp = jnp.repeat(ssd_D[l].astype(F32), headdim).reshape(1, d_ssm)
        gn = row(g_ssd_norm[l])

        w_in_b = w_in[l].astype(BF16)
        ms = ys.shape[0]
        proj_s, dt_s = _in_proj(ys, row(g_mix[l]), w_in_b, wdt_b, n_main, tm=ms, tn=TN_IN)

        proj_p, dt_p = _in_proj(yp, row(g_mix[l]), w_in_b, wdt_b, n_main, tm=TM, tn=TN_IN)
        rg_steps = pb * (seq // TT_RG)
        ssd_steps = pb * (seq // SSD_CHUNK)
        rb_ssd = d_model // (2 * rg_steps + ssd_steps)
        assert rb_ssd % 16 == 0 and (2 * rg_steps + ssd_steps) * rb_ssd == d_model
        rg_p, hl_p, w_up_ta = _rglru_prompt(proj_p, pb, seq, d_rnn, *rg_args, w_ffn_up[l], TN_UP,
                                            2 * rb_ssd, tt=TT_RG)
        ssd_p, hfin_p, w_up_tb, w_out_b = _ssd_prompt(
            proj_p, dt_p, pb, seq, d_ssm, d_xbc, n_state, headdim, ssd_conv_w[l], row(ssd_conv_b[l]),
            dtb, alog, dexp, gn, e_sel, w_ffn_up[l], TN_UP, rb_ssd, 2 * rb_ssd * rg_steps, w_out[l])
        x1_p, h2_p, ss_p = _out_proj(rg_p, ssd_p, w_out_b, yp, row(g_ffn[l]), tm=TM, tn=TN_OUT)
        act_p, gst_p, w_down_t = _ffn_up_prompt(h2_p, ss_p, w_up_ta, w_up_tb, ffn_conv_w[l],
                                                row(ffn_conv_b[l]), seq, w_ffn_down[l], TN_DOWN, tm=TM)
        proj_p3 = proj_p.reshape(pb, seq, n_main)
        p_new[0].append(hl_p[:, SUBLANES - 1, :])
        p_new[1].append(proj_p3[:, seq - (k_rg - 1):, :d_rnn])
        p_new[2].append(hfin_p.reshape(pb, heads, headdim, n_state))
        p_new[3].append(proj_p3[:, seq - (k_ssd - 1):, n_main - d_xbc:])
        tiles = seq // TM
        p_new[4].append(gst_p[tiles - 1::tiles, SUBLANES - (k_ffn - 1):, :])

        proj_s3 = proj_s.reshape(dec_seq, sb, n_main)
        rg_s, hl_s = _rglru_sample(proj_s3, swap(state_rglru_conv[l]), state_rglru_h[l], d_rnn,
                                   *rg_args, bb=bb)
        ydp, ecum, xdd, bc, cd = _ssd_sample_pre(
            proj_s3, swap(state_ssd_conv[l]), dt_s.reshape(dec_seq, sb, LANES), d_ssm, d_xbc, n_state,
            ssd_conv_w[l], row(ssd_conv_b[l]), dtb, alog, dexp, e_sel, g_sum, g_exp, bb=bb)
        bc_b = swap(bc)
        snew, yoff_b = _ssd_sample_state(state_ssd_h[l], bc_b[:, :, gn_w:], bc_b[:, :, :gn_w],
                                         swap(xdd), cd, bt=8)
        ssd_s = _ssd_sample_post(ydp, swap(yoff_b), ecum, proj_s3, gn, d_ssm, d_xbc, bb=bb)
        x1_s, h2_s, ss_s = _out_proj(rg_s.reshape(ms, d_rnn), ssd_s.reshape(ms, d_ssm), w_out_b, ys,
                                     row(g_ffn[l]), tm=ms, tn=TN_OUT)
        act_s, gst_s = _ffn_up_sample(h2_s, ss_s, w_up_ta, w_up_tb, ffn_conv_w[l], row(ffn_conv_b[l]),
                                      swap(state_ffn_conv[l]).reshape(-1, d_ff), sb)
        s_new[0].append(hl_s)
        s_new[1].append(swap(proj_s3[dec_seq - (k_rg - 1):, :, :d_rnn]))
        s_new[2].append(snew)
        s_new[3].append(swap(proj_s3[dec_seq - (k_ssd - 1):, :, n_main - d_xbc:]))
        s_new[4].append(swap(gst_s.reshape(k_ffn - 1, sb, d_ff)))

        last = l == depth - 1
        yp = _ffn_down(act_p, w_down_t, x1_p, row(g_final), last, tm=TM, tk=TK_DOWN)
        ys = _ffn_down(act_s, w_down_t, x1_s, row(g_final), last, tm=ms, tk=TK_DOWN)

    y_prompt = yp.reshape(pb, seq, d_model)
    y_sample = jnp.transpose(ys.reshape(dec_seq, sb, d_model), (1, 0, 2))
    return (y_prompt, y_sample,
            jnp.stack(p_new[0]), jnp.stack(p_new[1]), jnp.stack(p_new[2]), jnp.stack(p_new[3]),
            jnp.stack(p_new[4]),
            jnp.stack(s_new[0]), jnp.stack(s_new[1]), jnp.stack(s_new[2]), jnp.stack(s_new[3]),
            jnp.stack(s_new[4]))
```

```python
import functools

import jax
import jax.numpy as jnp
from jax import lax
from jax.experimental import pallas as pl
from jax.experimental.pallas import tpu as pltpu

F32 = jnp.float32
BF16 = jnp.bfloat16

EPS = 1e-6
RG_C = 8.0
SSD_CHUNK = 128
SSD_GROUPS = 4
LANES = 128
SUBLANES = 8
VMEM_LIMIT = 56 * 2**20

TM = 1024
ROW_CHUNK = 256
TT_RG = 256
SAMPLE_BB = 32
SAMPLE_BT = 8
TN_IN = 512
TN_OUT = 512
TN_UP = 512
TN_DOWN = 512
TK_DOWN = 4096


def _params(n_axes):
    return pltpu.CompilerParams(dimension_semantics=("arbitrary",) * n_axes,
                                vmem_limit_bytes=VMEM_LIMIT)


def _dot(a, b):
    return jnp.dot(a, b, preferred_element_type=F32)


def _rms_rows(x, g):
    ms = jnp.mean(x * x, axis=-1, keepdims=True)
    return (x * lax.rsqrt(ms + EPS)) * g


def _split3(x):
    hi = x.astype(BF16)
    r = x - hi.astype(F32)
    mid = r.astype(BF16)
    lo = (r - mid.astype(F32)).astype(BF16)
    return hi, mid, lo


def _dot_sel_right(x, sel):
    hi, mid, lo = _split3(x)
    return (_dot(hi, sel) + _dot(mid, sel)) + _dot(lo, sel)


def _dot_sel_left(sel, x):
    hi, mid, lo = _split3(x)
    return (_dot(sel, hi) + _dot(sel, mid)) + _dot(sel, lo)


def _inproj_kernel(x_hbm, g_ref, w_ref, wdt_ref, o_ref, dt_ref, x_buf, hn_ref, sem, *, ni, tm):
    i = pl.program_id(0)
    j = pl.program_id(1)

    def x_in(tile):
        return pltpu.make_async_copy(x_hbm.at[pl.ds(tile * tm, tm)], x_buf, sem)

    @pl.when(j == 0)
    def _():
        @pl.when(i == 0)
        def _():
            x_in(0).start()

        x_in(i).wait()
        for r in range(0, tm, ROW_CHUNK):
            hn_ref[r:r + ROW_CHUNK, :] = _rms_rows(x_buf[r:r + ROW_CHUNK, :], g_ref[...]).astype(BF16)
        dt_ref[...] = _dot(hn_ref[...], wdt_ref[...])

        @pl.when(i + 1 < ni)
        def _():
            x_in(i + 1).start()

    o_ref[...] = _dot(hn_ref[...], w_ref[...])


def _cast_col_tiles(src_ref, dst_ref):
    n_tiles, _, tn = dst_ref.shape
    for t in range(n_tiles):
        dst_ref[t] = src_ref[:, t * tn:(t + 1) * tn].astype(BF16)


def _in_proj(x2d, g, w_b, wdt_b, n_main, *, tm, tn):
    m, d = x2d.shape
    ni = m // tm
    return pl.pallas_call(
        functools.partial(_inproj_kernel, ni=ni, tm=tm),
        grid=(ni, n_main // tn),
        in_specs=[pl.BlockSpec(memory_space=pl.ANY),
                  pl.BlockSpec((1, d), lambda i, j: (0, 0)),
                  pl.BlockSpec((d, tn), lambda i, j: (0, j)),
                  pl.BlockSpec((d, LANES), lambda i, j: (0, 0))],
        out_specs=[pl.BlockSpec((tm, tn), lambda i, j: (i, j)),
                   pl.BlockSpec((tm, LANES), lambda i, j: (i, 0))],
        out_shape=[jax.ShapeDtypeStruct((m, n_main), F32), jax.ShapeDtypeStruct((m, LANES), F32)],
        scratch_shapes=[pltpu.VMEM((tm, d), F32), pltpu.VMEM((tm, d), BF16), pltpu.SemaphoreType.DMA(())],
        compiler_params=_params(2),
        name="in_proj",
    )(x2d, g, w_b, wdt_b)


def _outproj_kernel(rg_ref, ssd_ref, wt_ref, wb_ref, x_ref, g_ref, x1_ref, xg_ref, ss_ref):
    j = pl.program_id(1)
    x1 = x_ref[...] + (_dot(rg_ref[...], wt_ref[...]) + _dot(ssd_ref[...], wb_ref[...]))
    x1_ref[...] = x1
    xg_ref[...] = (x1 * g_ref[...]).astype(BF16)
    part = jnp.broadcast_to(jnp.sum(x1 * x1, axis=-1, keepdims=True), ss_ref.shape)

    @pl.when(j == 0)
    def _():
        ss_ref[...] = part

    @pl.when(j > 0)
    def _():
        ss_ref[...] += part


def _out_proj(rg, ssd, w_b, x2d, g, *, tm, tn):
    m, d = x2d.shape
    k_half = rg.shape[1]
    return pl.pallas_call(
        _outproj_kernel,
        grid=(m // tm, d // tn),
        in_specs=[pl.BlockSpec((tm, k_half), lambda i, j: (i, 0)),
                  pl.BlockSpec((tm, k_half), lambda i, j: (i, 0)),
                  pl.BlockSpec((k_half, tn), lambda i, j: (0, j)),
                  pl.BlockSpec((k_half, tn), lambda i, j: (1, j)),
                  pl.BlockSpec((tm, tn), lambda i, j: (i, j)),
                  pl.BlockSpec((1, tn), lambda i, j: (0, j))],
        out_specs=[pl.BlockSpec((tm, tn), lambda i, j: (i, j)),
                   pl.BlockSpec((tm, tn), lambda i, j: (i, j)),
                   pl.BlockSpec((tm, LANES), lambda i, j: (i, 0))],
        out_shape=[jax.ShapeDtypeStruct((m, d), F32),
                   jax.ShapeDtypeStruct((m, d), BF16),
                   jax.ShapeDtypeStruct((m, LANES), F32)],
        compiler_params=_params(2),
        name="out_proj",
    )(rg, ssd, w_b, w_b, x2d, g)


def _glu(gate_c, val):
    return (jax.nn.gelu(gate_c) * val).astype(BF16)


def _dot_split_k(h_ref, wa_ref, wb_ref):
    ka = wa_ref.shape[0]
    return _dot(h_ref[:, :ka], wa_ref[...]) + _dot(h_ref[:, ka:], wb_ref[...])


def _inv_rms(ss_ref, width):
    return lax.rsqrt(ss_ref[:, 0:1] / width + EPS)


def _ffn_up_prompt_kernel(xg_ref, ss_ref, wga_ref, wgb_ref, wva_ref, wvb_ref, cw_ref, cb_ref, wsrc_ref,
                          act_ref, gst_ref, wdst_ref, carry_ref, *, tiles_per_seq):
    i = pl.program_id(0)
    j = pl.program_id(1)
    _cast_col_tiles(wsrc_ref, wdst_ref)
    inv = _inv_rms(ss_ref, xg_ref.shape[1])
    gate = _dot_split_k(xg_ref, wga_ref, wgb_ref) * inv
    val = _dot_split_k(xg_ref, wva_ref, wvb_ref) * inv
    tm, tn = gate.shape
    w0, w1, w2 = cw_ref[0:1, :], cw_ref[1:2, :], cw_ref[2:3, :]
    cb = cb_ref[...]
    g1 = pltpu.roll(gate, 1, 0)
    g2 = pltpu.roll(gate, 2, 0)
    act_ref[...] = _glu(((cb + g2 * w0) + g1 * w1) + gate * w2, val)
    @pl.when((i % tiles_per_seq) == 0)
    def _():
        carry_ref[j] = jnp.zeros((SUBLANES, tn), F32)

    prev = carry_ref[j]
    rows = lax.broadcasted_iota(jnp.int32, (SUBLANES, tn), 0)
    t1 = jnp.where(rows < 1, pltpu.roll(prev, 1, 0), g1[0:SUBLANES])
    t2 = jnp.where(rows < 2, pltpu.roll(prev, 2, 0), g2[0:SUBLANES])
    act_ref[0:SUBLANES, :] = _glu(((cb + t2 * w0) + t1 * w1) + gate[0:SUBLANES] * w2,
                                  val[0:SUBLANES])
    last = gate[tm - SUBLANES:tm]
    carry_ref[j] = last
    gst_ref[...] = last


def _ffn_up_prompt(h2, ss, w_ta, w_tb, cw, cb, seq, w_next, tn_next, *, tm):
    m, d = h2.shape
    nj, ka, tn = w_ta.shape[0] // 2, w_ta.shape[1], w_ta.shape[2]
    kb = w_tb.shape[1]
    assert ka + kb == d and ka % LANES == 0
    d_ff = nj * tn
    steps = (m // tm) * nj
    k_next, n_next = w_next.shape
    rb = k_next // steps
    assert rb * steps == k_next and rb % 16 == 0 and n_next % tn_next == 0
    return pl.pallas_call(
        functools.partial(_ffn_up_prompt_kernel, tiles_per_seq=seq // tm),
        grid=(m // tm, nj),
        in_specs=[pl.BlockSpec((tm, d), lambda i, j: (i, 0)),
                  pl.BlockSpec((tm, LANES), lambda i, j: (i, 0)),
                  pl.BlockSpec((None, ka, tn), lambda i, j: (j, 0, 0)),
                  pl.BlockSpec((None, kb, tn), lambda i, j: (j, 0, 0)),
                  pl.BlockSpec((None, ka, tn), lambda i, j: (j + nj, 0, 0)),
                  pl.BlockSpec((None, kb, tn), lambda i, j: (j + nj, 0, 0)),
                  pl.BlockSpec((3, tn), lambda i, j: (0, j)),
                  pl.BlockSpec((1, tn), lambda i, j: (0, j)),
                  pl.BlockSpec((rb, n_next), lambda i, j: (i * nj + j, 0))],
        out_specs=[pl.BlockSpec((tm, tn), lambda i, j: (i, j)),
                   pl.BlockSpec((None, SUBLANES, tn), lambda i, j: (i, 0, j)),
                   pl.BlockSpec((n_next // tn_next, rb, tn_next), lambda i, j: (0, i * nj + j, 0))],
        out_shape=[jax.ShapeDtypeStruct((m, d_ff), BF16),
                   jax.ShapeDtypeStruct((m // tm, SUBLANES, d_ff), F32),
                   jax.ShapeDtypeStruct((n_next // tn_next, k_next, tn_next), BF16)],
        scratch_shapes=[pltpu.VMEM((nj, SUBLANES, tn), F32)],
        compiler_params=_params(2),
        name="ffn_up_prompt",
    )(h2, ss, w_ta, w_tb, w_ta, w_tb, cw, cb, w_next)


def _ffn_up_sample_kernel(xg_ref, ss_ref, wga_ref, wgb_ref, wva_ref, wvb_ref, cw_ref, cb_ref, st_ref,
                          act_ref, nst_ref, *, nb, nt):
    inv = _inv_rms(ss_ref, xg_ref.shape[1])
    gate = _dot_split_k(xg_ref, wga_ref, wgb_ref) * inv
    val = _dot_split_k(xg_ref, wva_ref, wvb_ref) * inv
    w0, w1, w2 = cw_ref[0:1, :], cw_ref[1:2, :], cw_ref[2:3, :]
    cb = cb_ref[...]
    ext = [st_ref[0:nb, :], st_ref[nb:2 * nb, :]] + [gate[t * nb:(t + 1) * nb] for t in range(nt)]
    for t in range(nt):
        gc = ((cb + ext[t] * w0) + ext[t + 1] * w1) + ext[t + 2] * w2
        act_ref[t * nb:(t + 1) * nb, :] = _glu(gc, val[t * nb:(t + 1) * nb])
    nst_ref[0:nb, :] = ext[nt]
    nst_ref[nb:2 * nb, :] = ext[nt + 1]


def _ffn_up_sample(h2, ss, w_ta, w_tb, cw, cb, st, nb):
    m, d = h2.shape
    nj, ka, tn = w_ta.shape[0] // 2, w_ta.shape[1], w_ta.shape[2]
    kb = w_tb.shape[1]
    d_ff = nj * tn
    return pl.pallas_call(
        functools.partial(_ffn_up_sample_kernel, nb=nb, nt=m // nb),
        grid=(nj,),
        in_specs=[pl.BlockSpec((m, d), lambda j: (0, 0)),
                  pl.BlockSpec((m, LANES), lambda j: (0, 0)),
                  pl.BlockSpec((None, ka, tn), lambda j: (j, 0, 0)),
                  pl.BlockSpec((None, kb, tn), lambda j: (j, 0, 0)),
                  pl.BlockSpec((None, ka, tn), lambda j: (j + nj, 0, 0)),
                  pl.BlockSpec((None, kb, tn), lambda j: (j + nj, 0, 0)),
                  pl.BlockSpec((3, tn), lambda j: (0, j)),
                  pl.BlockSpec((1, tn), lambda j: (0, j)),
                  pl.BlockSpec((2 * nb, tn), lambda j: (0, j))],
        out_specs=[pl.BlockSpec((m, tn), lambda j: (0, j)),
                   pl.BlockSpec((2 * nb, tn), lambda j: (0, j))],
        out_shape=[jax.ShapeDtypeStruct((m, d_ff), BF16),
                   jax.ShapeDtypeStruct((2 * nb, d_ff), F32)],
        compiler_params=_params(1),
        name="ffn_up_sample",
    )(h2, ss, w_ta, w_tb, w_ta, w_tb, cw, cb, st)


def _ffn_down_kernel(act_ref, w_ref, x1_ref, g_ref, y_hbm, rows_ref, sem,
                     *, ni, nk, nj, tm, tn, final_norm):
    i = pl.program_id(0)
    k = pl.program_id(1)
    j = pl.program_id(2)
    cols = pl.ds(pl.multiple_of(j * tn, tn), tn)

    def rows_out(tile):
        return pltpu.make_async_copy(rows_ref, y_hbm.at[pl.ds(tile * tm, tm)], sem)

    @pl.when(k == 0)
    def _():
        @pl.when((j == 0) & (i > 0))
        def _():
            rows_out(i - 1).wait()

        rows_ref[:, cols] = x1_ref[...]

    rows_ref[:, cols] += _dot(act_ref[...], w_ref[...])

    @pl.when((k == nk - 1) & (j == nj - 1))
    def _():
        if final_norm:
            for r in range(0, tm, ROW_CHUNK):
                rows_ref[r:r + ROW_CHUNK, :] = _rms_rows(rows_ref[r:r + ROW_CHUNK, :], g_ref[...])
        rows_out(i).start()

        @pl.when(i == ni - 1)
        def _():
            rows_out(i).wait()


def _ffn_down(act, w_t, x1, g, final_norm, *, tm, tk):
    m, d = x1.shape
    nj, d_ff, tn = w_t.shape
    ni, nk = m // tm, d_ff // tk
    return pl.pallas_call(
        functools.partial(_ffn_down_kernel, ni=ni, nk=nk, nj=nj, tm=tm, tn=tn, final_norm=final_norm),
        grid=(ni, nk, nj),
        in_specs=[pl.BlockSpec((tm, tk), lambda i, k, j: (i, k)),
                  pl.BlockSpec((None, tk, tn), lambda i, k, j: (j, k, 0)),
                  pl.BlockSpec((tm, tn), lambda i, k, j: (i, jnp.where(k == 0, j, nj - 1))),
                  pl.BlockSpec((1, d), lambda i, k, j: (0, 0))],
        out_specs=pl.BlockSpec(memory_space=pl.ANY),
        out_shape=jax.ShapeDtypeStruct((m, d), F32),
        scratch_shapes=[pltpu.VMEM((tm, d), F32), pltpu.SemaphoreType.DMA(())],
        compiler_params=_params(3),
        name="ffn_down",
    )(act, w_t, x1, g)


def _softplus(x):
    return jnp.maximum(x, 0.0) + jnp.log1p(jnp.exp(-jnp.abs(x)))


def _rglru_coeffs(xc, wa_ref, ba_ref, wi_ref, bi_ref, lam_ref):
    xb = xc.astype(BF16)
    heads, blk = wa_ref.shape[0], wa_ref.shape[1]
    ga = jnp.concatenate([_dot(xb[:, h * blk:(h + 1) * blk], wa_ref[h]) for h in range(heads)], axis=1)
    gi = jnp.concatenate([_dot(xb[:, h * blk:(h + 1) * blk], wi_ref[h]) for h in range(heads)], axis=1)
    gate_r = jax.nn.sigmoid(ga + ba_ref[...])
    gate_i = jax.nn.sigmoid(gi + bi_ref[...])
    log_a = (-RG_C * gate_r) * _softplus(-lam_ref[...])
    a = jnp.exp(log_a)
    bx = jnp.sqrt(-jnp.tanh(log_a) * (a * a + 1.0)) * (gate_i * xc)
    return a, bx


def _rg_out(h, gate, g_ref):
    return _rms_rows(h * jax.nn.gelu(gate), g_ref[...]).astype(BF16)


def _conv_tile(x, prev, w_ref, b_ref, k_taps):
    t_len, width = x.shape
    groups = t_len // SUBLANES
    x3 = x.reshape(groups, SUBLANES, width)
    pos = lax.broadcasted_iota(jnp.int32, x3.shape, 1)
    out = b_ref[...].reshape(1, 1, width)
    for k in range(k_taps):
        s = k_taps - 1 - k
        wk = w_ref[k:k + 1, :].reshape(1, 1, width)
        if s == 0:
            shifted = x3
        else:
            rot = pltpu.roll(x3, s, 1)
            rot_prev = jnp.concatenate([pltpu.roll(prev, s, 0)[None], rot[:groups - 1]], axis=0)
            shifted = jnp.where(pos >= s, rot, rot_prev)
        out = out + shifted * wk
    return out.reshape(t_len, width)


def _rglru_prompt_kernel(x_ref, gate_ref, cw_ref, cb_ref, wa_ref, ba_ref, wi_ref, bi_ref, lam_ref,
                         g_ref, wsrc_ref, out_ref, hlast_ref, wdst_ref, xprev_ref, hc_ref, a_s, b_s, h_s,
                         *, k_taps):
    c = pl.program_id(1)
    _cast_col_tiles(wsrc_ref, wdst_ref)

    @pl.when(c == 0)
    def _():
        xprev_ref[...] = jnp.zeros_like(xprev_ref)
        hc_ref[...] = jnp.zeros_like(hc_ref)

    x = x_ref[...]
    t_len, width = x.shape
    xc = _conv_tile(x, xprev_ref[...], cw_ref, cb_ref, k_taps)
    xprev_ref[...] = x[t_len - SUBLANES:t_len]
    a, b = _rglru_coeffs(xc, wa_ref, ba_ref, wi_ref, bi_ref, lam_ref)
    groups = t_len // SUBLANES
    a = a.reshape(groups, SUBLANES, width)
    b = b.reshape(groups, SUBLANES, width)
    pos = lax.broadcasted_iota(jnp.int32, a.shape, 1)
    for s in (1, 2, 4):
        a_sh = pltpu.roll(a, s, 1)
        b_sh = pltpu.roll(b, s, 1)
        m = pos >= s
        b = jnp.where(m, a * b_sh + b, b)
        a = jnp.where(m, a * a_sh, a)
    a_s[...] = a.reshape(t_len, width)
    b_s[...] = b.reshape(t_len, width)
    h = hc_ref[...]
    for grp in range(t_len // SUBLANES):
        lo, hi = grp * SUBLANES, (grp + 1) * SUBLANES
        hl = jnp.broadcast_to(h[SUBLANES - 1:SUBLANES, :], (SUBLANES, width))
        h = a_s[lo:hi, :] * hl + b_s[lo:hi, :]
        h_s[lo:hi, :] = h
    hc_ref[...] = h
    hlast_ref[...] = h
    out_ref[...] = _rg_out(h_s[...], gate_ref[...], g_ref)


def _rglru_prompt(proj, nb, seq, d_rnn, cw, cb, wa, ba, wi, bi, lam, g, w_next, tn_next, rb, *, tt):
    nc = seq // tt
    heads, blk = wa.shape[0], wa.shape[1]
    k_next, n_next = w_next.shape
    vec = pl.BlockSpec((1, d_rnn), lambda b, c: (0, 0))
    gate_w = pl.BlockSpec((heads, blk, blk), lambda b, c: (0, 0, 0))
    return pl.pallas_call(
        functools.partial(_rglru_prompt_kernel, k_taps=cw.shape[0]),
        grid=(nb, nc),
        in_specs=[pl.BlockSpec((tt, d_rnn), lambda b, c: (b * nc + c, 0)),
                  pl.BlockSpec((tt, d_rnn), lambda b, c: (b * nc + c, 1)),
                  pl.BlockSpec((cw.shape[0], d_rnn), lambda b, c: (0, 0)),
                  vec, gate_w, vec, gate_w, vec, vec, vec,
                  pl.BlockSpec((rb, n_next), lambda b, c: (b * nc + c, 0))],
        out_specs=[pl.BlockSpec((tt, d_rnn), lambda b, c: (b * nc + c, 0)),
                   pl.BlockSpec((None, SUBLANES, d_rnn), lambda b, c: (b, 0, 0)),
                   pl.BlockSpec((n_next // tn_next, rb, tn_next), lambda b, c: (0, b * nc + c, 0))],
        out_shape=[jax.ShapeDtypeStruct((nb * seq, d_rnn), BF16),
                   jax.ShapeDtypeStruct((nb, SUBLANES, d_rnn), F32),
                   jax.ShapeDtypeStruct((n_next // tn_next, nb * nc * rb, tn_next), BF16)],
        scratch_shapes=[pltpu.VMEM((SUBLANES, d_rnn), F32), pltpu.VMEM((SUBLANES, d_rnn), F32),
                        pltpu.VMEM((tt, d_rnn), F32), pltpu.VMEM((tt, d_rnn), F32),
                        pltpu.VMEM((tt, d_rnn), F32)],
        compiler_params=_params(2),
        name="rglru_prompt",
    )(proj, proj, cw, cb, wa, ba, wi, bi, lam, g, w_next)


def _conv_slabs(st_ref, x_ref, w_ref, b_ref):
    k_taps = w_ref.shape[0]
    ext = [st_ref[s] for s in range(k_taps - 1)] + [x_ref[t] for t in range(x_ref.shape[0])]
    convs = []
    for t in range(x_ref.shape[0]):
        acc = b_ref[...]
        for k in range(k_taps):
            acc = acc + ext[t + k] * w_ref[k:k + 1, :]
        convs.append(acc)
    return jnp.concatenate(convs, axis=0)


def _rglru_sample_kernel(x_ref, gate_ref, st_ref, h0_ref, cw_ref, cb_ref, wa_ref, ba_ref, wi_ref,
                         bi_ref, lam_ref, g_ref, out_ref, hlast_ref, h_s):
    nt, bb, width = x_ref.shape
    xc = _conv_slabs(st_ref, x_ref, cw_ref, cb_ref)
    a, b = _rglru_coeffs(xc, wa_ref, ba_ref, wi_ref, bi_ref, lam_ref)
    h = h0_ref[...]
    for t in range(nt):
        h = a[t * bb:(t + 1) * bb] * h + b[t * bb:(t + 1) * bb]
        h_s[t * bb:(t + 1) * bb, :] = h
    hlast_ref[...] = h
    gate = gate_ref[...].reshape(nt * bb, width)
    out_ref[...] = _rg_out(h_s[...], gate, g_ref).reshape(nt, bb, width)


def _rglru_sample(proj3, st3, h0, d_rnn, cw, cb, wa, ba, wi, bi, lam, g, *, bb):
    nt, nb, _ = proj3.shape
    heads, blk = wa.shape[0], wa.shape[1]
    k_taps = cw.shape[0]
    vec = pl.BlockSpec((1, d_rnn), lambda i: (0, 0))
    gate_w = pl.BlockSpec((heads, blk, blk), lambda i: (0, 0, 0))
    return pl.pallas_call(
        _rglru_sample_kernel,
        grid=(nb // bb,),
        in_specs=[pl.BlockSpec((nt, bb, d_rnn), lambda i: (0, i, 0)),
                  pl.BlockSpec((nt, bb, d_rnn), lambda i: (0, i, 1)),
                  pl.BlockSpec((k_taps - 1, bb, d_rnn), lambda i: (0, i, 0)),
                  pl.BlockSpec((bb, d_rnn), lambda i: (i, 0)),
                  pl.BlockSpec((k_taps, d_rnn), lambda i: (0, 0)),
                  vec, gate_w, vec, gate_w, vec, vec, vec],
        out_specs=[pl.BlockSpec((nt, bb, d_rnn), lambda i: (0, i, 0)),
                   pl.BlockSpec((bb, d_rnn), lambda i: (i, 0))],
        out_shape=[jax.ShapeDtypeStruct((nt, nb, d_rnn), BF16),
                   jax.ShapeDtypeStruct((nb, d_rnn), F32)],
        scratch_shapes=[pltpu.VMEM((nt * bb, d_rnn), F32)],
        compiler_params=_params(1),
        name="rglru_sample",
    )(proj3, proj3, st3, h0, cw, cb, wa, ba, wi, bi, lam, g)


def _silu(x):
    return x * jax.nn.sigmoid(x)


def _group_norm_out(y, z, gn_ref, groups):
    u = y * _silu(z)
    gw = u.shape[1] // groups
    outs = []
    for g in range(groups):
        ug = u[:, g * gw:(g + 1) * gw]
        ms = jnp.mean(ug * ug, axis=-1, keepdims=True)
        outs.append((ug * lax.rsqrt(ms + EPS)) * gn_ref[:, g * gw:(g + 1) * gw])
    return jnp.concatenate(outs, axis=1).astype(BF16)


def _ssd_prompt_kernel(z_ref, xbc_ref, dt_ref, cw_ref, cb_ref, dtb_ref, alog_ref, dexp_ref, gn_ref,
                       e_ref, wsrc_ref, wsrc2_ref, out_ref, hfin_ref, wdst_ref, wdst2_ref,
                       carry_ref, s_ref, *, k_taps, d_ssm, n_state, groups, headdim, nc):
    c = pl.program_id(1)
    _cast_col_tiles(wsrc_ref, wdst_ref)
    wdst2_ref[...] = wsrc2_ref[...].astype(BF16)

    @pl.when(c == 0)
    def _():
        carry_ref[...] = jnp.zeros_like(carry_ref)
        s_ref[...] = jnp.zeros_like(s_ref)

    xbc = xbc_ref[...]
    q = xbc.shape[0]
    xc = _silu(_conv_tile(xbc, carry_ref[...], cw_ref, cb_ref, k_taps))
    carry_ref[...] = xbc[q - SUBLANES:q]
    gn_w = groups * n_state
    bm = xc[:, d_ssm:d_ssm + gn_w]
    cm = xc[:, d_ssm + gn_w:d_ssm + 2 * gn_w]

    dt = _softplus(dt_ref[...] + dtb_ref[...])
    dta = dt * (-jnp.exp(alog_ref[...]))
    row = lax.broadcasted_iota(jnp.int32, (q, q), 0)
    col = lax.broadcasted_iota(jnp.int32, (q, q), 1)
    causal = row >= col
    tril = jnp.where(causal, 1.0, 0.0).astype(BF16)
    cum = _dot_sel_left(tril, dta)
    cum_t = cum.T

    heads_per_group = d_ssm // headdim // groups
    pair_w = 2 * headdim
    gw = heads_per_group * headdim
    lane = lax.broadcasted_iota(jnp.int32, (q, pair_w), 1)
    for g in range(groups):
        gs = slice(g * gw, (g + 1) * gw)
        xs = xc[:, gs]
        e = e_ref[:, gs]
        dt_e = _dot_sel_right(dt, e)
        cum_e = _dot_sel_right(cum, e)
        xdt = xs * dt_e
        cum_last = cum_e[q - 1:q, :]
        xdd = (xdt * jnp.exp(cum_last - cum_e)).astype(BF16)
        cg = cm[:, g * n_state:(g + 1) * n_state].astype(BF16)
        bg32 = bm[:, g * n_state:(g + 1) * n_state]
        bg = bg32.astype(BF16)
        cb_g = lax.dot_general(cg, bg, (((1,), (1,)), ((), ())), preferred_element_type=F32)
        s_old = s_ref[:, gs]
        y_state = _dot(cg, s_old.astype(BF16)) * jnp.exp(cum_e)
        s_ref[:, gs] = s_old * jnp.exp(cum_last) + _dot(bg32.T.astype(BF16), xdd)
        y_pairs = []
        for pr in range(heads_per_group // 2):
            h0 = g * heads_per_group + 2 * pr
            ms = []
            for h in (h0, h0 + 1):
                seg = (jnp.broadcast_to(cum[:, h:h + 1], (q, q))
                       - jnp.broadcast_to(cum_t[h:h + 1, :], (q, q)))
                ms.append(cb_g * jnp.exp(jnp.where(causal, seg, -jnp.inf)))
            xp = xdt[:, 2 * pr * headdim:2 * pr * headdim + pair_w]
            lhs = jnp.concatenate(ms, axis=1).astype(BF16)
            rhs = jnp.concatenate([jnp.where(lane < headdim, xp, 0.0),
                                   jnp.where(lane >= headdim, xp, 0.0)], axis=0).astype(BF16)
            y_pairs.append(_dot(lhs, rhs))
        y = (y_state + jnp.concatenate(y_pairs, axis=1)) + dexp_ref[:, gs] * xs
        u = y * _silu(z_ref[:, gs])
        ms_g = jnp.mean(u * u, axis=-1, keepdims=True)
        out_ref[:, gs] = ((u * lax.rsqrt(ms_g + EPS)) * gn_ref[:, gs]).astype(BF16)

    @pl.when(c == nc - 1)
    def _():
        for blk in range(d_ssm // LANES):
            hfin_ref[blk * LANES:(blk + 1) * LANES, :] = s_ref[:, blk * LANES:(blk + 1) * LANES].T


def _ssd_prompt(proj, dt_raw, nb, seq, d_ssm, d_xbc, n_state, headdim, cw, cb, dtb, alog, dexp, gn, e,
                w_next, tn_next, rb, row0, w_plain):
    q = SSD_CHUNK
    nc = seq // q
    k_plain, n_plain = w_plain.shape
    rb2 = k_plain // (nb * nc)
    assert rb2 * nb * nc == k_plain and rb2 % 16 == 0
    z_blk = (proj.shape[1] - d_xbc - d_ssm) // d_ssm
    x_blk = (proj.shape[1] - d_xbc) // d_xbc
    assert z_blk * d_ssm + d_ssm + d_xbc == proj.shape[1] and x_blk * d_xbc + d_xbc == proj.shape[1]
    k_next, n_next = w_next.shape
    n_tiles = n_next // tn_next
    blk0 = row0 // rb
    assert blk0 * rb == row0 and row0 + nb * nc * rb == k_next
    one = lambda w: pl.BlockSpec((1, w), lambda b, c: (0, 0))
    return pl.pallas_call(
        functools.partial(_ssd_prompt_kernel, k_taps=cw.shape[0], d_ssm=d_ssm, n_state=n_state,
                          groups=SSD_GROUPS, headdim=headdim, nc=nc),
        grid=(nb, nc),
        in_specs=[pl.BlockSpec((q, d_ssm), lambda b, c: (b * nc + c, z_blk)),
                  pl.BlockSpec((q, d_xbc), lambda b, c: (b * nc + c, x_blk)),
                  pl.BlockSpec((q, LANES), lambda b, c: (b * nc + c, 0)),
                  pl.BlockSpec((cw.shape[0], d_xbc), lambda b, c: (0, 0)),
                  one(d_xbc), one(LANES), one(LANES), one(d_ssm), one(d_ssm),
                  pl.BlockSpec((LANES, d_ssm), lambda b, c: (0, 0)),
                  pl.BlockSpec((rb, n_next), lambda b, c: (blk0 + b * nc + c, 0)),
                  pl.BlockSpec((rb2, n_plain), lambda b, c: (b * nc + c, 0))],
        out_specs=[pl.BlockSpec((q, d_ssm), lambda b, c: (b * nc + c, 0)),
                   pl.BlockSpec((None, d_ssm, n_state), lambda b, c: (b, 0, 0)),
                   pl.BlockSpec((n_tiles, rb, tn_next), lambda b, c: (0, b * nc + c, 0)),
                   pl.BlockSpec((rb2, n_plain), lambda b, c: (b * nc + c, 0))],
        out_shape=[jax.ShapeDtypeStruct((nb * seq, d_ssm), BF16),
                   jax.ShapeDtypeStruct((nb, d_ssm, n_state), F32),
                   jax.ShapeDtypeStruct((n_tiles, k_next - row0, tn_next), BF16),
                   jax.ShapeDtypeStruct((k_plain, n_plain), BF16)],
        scratch_shapes=[pltpu.VMEM((SUBLANES, d_xbc), F32), pltpu.VMEM((n_state, d_ssm), F32)],
        compiler_params=_params(2),
        name="ssd_prompt",
    )(proj, proj, dt_raw, cw, cb, dtb, alog, dexp, gn, e, w_next, w_plain)


def _ssd_sample_pre_kernel(xbc_ref, st_ref, dt_ref, cw_ref, cb_ref, dtb_ref, alog_ref, dexp_ref,
                           e_ref, gsum_ref, gexp_ref, ydp_ref, ecum_ref, xdd_ref, bc_ref, cd_ref,
                           *, d_ssm, n_state, groups):
    nt, bb, _ = xbc_ref.shape
    xc = _silu(_conv_slabs(st_ref, xbc_ref, cw_ref, cb_ref))
    gn_w = groups * n_state
    xs = xc[:, :d_ssm]
    bm = xc[:, d_ssm:d_ssm + gn_w]
    cm = xc[:, d_ssm + gn_w:d_ssm + 2 * gn_w]
    bc_ref[...] = xc[:, d_ssm:d_ssm + 2 * gn_w].reshape(nt, bb, 2 * gn_w)

    dt = _softplus(dt_ref[...].reshape(nt * bb, LANES) + dtb_ref[...])
    dta = dt * (-jnp.exp(alog_ref[...]))
    cums = [dta[0:bb]]
    for t in range(1, nt):
        cums.append(cums[-1] + dta[t * bb:(t + 1) * bb])
    cum = jnp.concatenate(cums, axis=0)
    cd_ref[...] = jnp.exp(cums[-1])
    e = e_ref[...]
    dt_e = _dot_sel_right(dt, e)
    cum_e = _dot_sel_right(cum, e)
    ecum_ref[...] = jnp.exp(cum_e).reshape(nt, bb, d_ssm)
    xdt = xs * dt_e
    sl = lambda v, t: v[t * bb:(t + 1) * bb]
    cum_last = sl(cum_e, nt - 1)
    gsum, gexp = gsum_ref[...], gexp_ref[...]
    for t in range(nt):
        xdd_ref[t] = sl(xdt, t) * jnp.exp(cum_last - sl(cum_e, t))
        acc = dexp_ref[...] * sl(xs, t)
        for s in range(t + 1):
            cb_e = _dot_sel_right(_dot_sel_right(sl(cm, t) * sl(bm, s), gsum), gexp)
            acc = acc + (cb_e * jnp.exp(sl(cum_e, t) - sl(cum_e, s))) * sl(xdt, s)
        ydp_ref[t] = acc


def _ssd_sample_pre(proj3, st3, dt3, d_ssm, d_xbc, n_state, cw, cb, dtb, alog, dexp, e, gsum, gexp,
                    *, bb):
    nt, nb, n_main = proj3.shape
    k_taps = cw.shape[0]
    x_blk = (n_main - d_xbc) // d_xbc
    gn_w = SSD_GROUPS * n_state
    full = lambda r, w: pl.BlockSpec((r, w), lambda i: (0, 0))
    slab = lambda n, w: pl.BlockSpec((n, bb, w), lambda i: (0, i, 0))
    f32 = lambda *s: jax.ShapeDtypeStruct(s, F32)
    return pl.pallas_call(
        functools.partial(_ssd_sample_pre_kernel, d_ssm=d_ssm, n_state=n_state, groups=SSD_GROUPS),
        grid=(nb // bb,),
        in_specs=[pl.BlockSpec((nt, bb, d_xbc), lambda i: (0, i, x_blk)),
                  slab(k_taps - 1, d_xbc), slab(nt, LANES), full(k_taps, d_xbc),
                  full(1, d_xbc), full(1, LANES), full(1, LANES), full(1, d_ssm),
                  full(LANES, d_ssm), full(gn_w, LANES), full(LANES, d_ssm)],
        out_specs=[slab(nt, d_ssm), slab(nt, d_ssm), slab(nt, d_ssm), slab(nt, 2 * gn_w),
                   pl.BlockSpec((bb, LANES), lambda i: (i, 0))],
        out_shape=[f32(nt, nb, d_ssm), f32(nt, nb, d_ssm), f32(nt, nb, d_ssm),
                   f32(nt, nb, 2 * gn_w), f32(nb, LANES)],
        compiler_params=_params(1),
        name="ssd_sample_pre",
    )(proj3, st3, dt3, cw, cb, dtb, alog, dexp, e, gsum, gexp)


def _ssd_sample_state_kernel(s0_ref, c_ref, b_ref, xdd_ref, cd_ref, snew_ref, yoff_ref,
                             *, bt, groups, n_state, headdim):
    heads = s0_ref.shape[1]
    hpg = heads // groups
    gw = hpg * headdim

    def body(b, carry):
        cd_row = cd_ref[pl.ds(b, 1), :]
        cb = c_ref[b].astype(BF16)
        bb = b_ref[b].astype(BF16)
        xb = xdd_ref[b].astype(BF16)
        for g in range(groups):
            s0 = s0_ref[b, g * hpg:(g + 1) * hpg].reshape(gw, n_state)
            yoff = lax.dot_general(cb[:, g * n_state:(g + 1) * n_state], s0.astype(BF16),
                                   (((1,), (1,)), ((), ())), preferred_element_type=F32)
            yoff_ref[b, :, g * gw:(g + 1) * gw] = yoff
            upd = lax.dot_general(xb[:, g * gw:(g + 1) * gw], bb[:, g * n_state:(g + 1) * n_state],
                                  (((0,), (0,)), ((), ())), preferred_element_type=F32)
            dec = jnp.concatenate(
                [jnp.broadcast_to(cd_row[:, h:h + 1], (headdim, n_state))
                 for h in range(g * hpg, (g + 1) * hpg)], axis=0)
            snew_ref[b, g * hpg:(g + 1) * hpg] = (s0 * dec + upd).reshape(hpg, headdim, n_state)
        return carry

    lax.fori_loop(0, bt, body, 0)


def _ssd_sample_state(s0, c_b, b_b, xdd_b, cd, *, bt):
    nb, heads, headdim, n_state = s0.shape
    nt = c_b.shape[1]
    gn_w = c_b.shape[2]
    d_ssm = xdd_b.shape[2]
    return pl.pallas_call(
        functools.partial(_ssd_sample_state_kernel, bt=bt, groups=SSD_GROUPS, n_state=n_state,
                          headdim=headdim),
        grid=(nb // bt,),
        in_specs=[pl.BlockSpec((bt, heads, headdim, n_state), lambda i: (i, 0, 0, 0)),
                  pl.BlockSpec((bt, nt, gn_w), lambda i: (i, 0, 0)),
                  pl.BlockSpec((bt, nt, gn_w), lambda i: (i, 0, 0)),
                  pl.BlockSpec((bt, nt, d_ssm), lambda i: (i, 0, 0)),
                  pl.BlockSpec((bt, LANES), lambda i: (i, 0))],
        out_specs=[pl.BlockSpec((bt, heads, headdim, n_state), lambda i: (i, 0, 0, 0)),
                   pl.BlockSpec((bt, nt, d_ssm), lambda i: (i, 0, 0))],
        out_shape=[jax.ShapeDtypeStruct(s0.shape, F32),
                   jax.ShapeDtypeStruct((nb, nt, d_ssm), F32)],
        compiler_params=_params(1),
        name="ssd_sample_state",
    )(s0, c_b, b_b, xdd_b, cd)


def _ssd_sample_post_kernel(ydp_ref, yoff_ref, ecum_ref, z_ref, gn_ref, out_ref, *, groups):
    nt, bb, d_ssm = ydp_ref.shape
    y = (ydp_ref[...] + yoff_ref[...] * ecum_ref[...]).reshape(nt * bb, d_ssm)
    z = z_ref[...].reshape(nt * bb, d_ssm)
    out_ref[...] = _group_norm_out(y, z, gn_ref, groups).reshape(nt, bb, d_ssm)


def _ssd_sample_post(ydp, yoff, ecum, proj3, gn, d_ssm, d_xbc, *, bb):
    nt, nb, n_main = proj3.shape
    z_blk = (n_main - d_xbc - d_ssm) // d_ssm
    slab = pl.BlockSpec((nt, bb, d_ssm), lambda i: (0, i, 0))
    return pl.pallas_call(
        functools.partial(_ssd_sample_post_kernel, groups=SSD_GROUPS),
        grid=(nb // bb,),
        in_specs=[slab, slab, slab, pl.BlockSpec((nt, bb, d_ssm), lambda i: (0, i, z_blk)),
                  pl.BlockSpec((1, d_ssm), lambda i: (0, 0))],
        out_specs=slab,
        out_shape=jax.ShapeDtypeStruct((nt, nb, d_ssm), BF16),
        compiler_params=_params(1),
        name="ssd_sample_post",
    )(ydp, yoff, ecum, proj3, gn)


def _pad_lanes(v):
    return jnp.pad(v.astype(F32), (0, LANES - v.shape[0])).reshape(1, LANES)


def kernel(x_prompt, x_sample, state_rglru_h, state_rglru_conv, state_ssd_h, state_ssd_conv, state_ffn_conv, g_mix, w_in, rg_conv_w, rg_conv_b, rg_gate_a_w, rg_gate_a_b, rg_gate_i_w, rg_gate_i_b, rg_lambda, g_rg_out, ssd_conv_w, ssd_conv_b, ssd_dt_bias, ssd_A_log, ssd_D, g_ssd_norm, w_out, g_ffn, w_ffn_up, ffn_conv_w, ffn_conv_b, w_ffn_down, g_final):
    depth = g_mix.shape[0]
    pb, seq, d_model = x_prompt.shape
    sb, dec_seq, _ = x_sample.shape
    d_rnn = rg_lambda.shape[1]
    d_ssm = g_ssd_norm.shape[1]
    d_xbc = ssd_conv_w.shape[2]
    d_ff = ffn_conv_w.shape[2]
    heads = ssd_A_log.shape[1]
    headdim = d_ssm // heads
    n_state = state_ssd_h.shape[-1]
    n_main = 2 * d_rnn + d_ssm + d_xbc
    k_rg = rg_conv_w.shape[1]
    k_ssd = ssd_conv_w.shape[1]
    k_ffn = ffn_conv_w.shape[1]
    gn_w = SSD_GROUPS * n_state
    assert heads <= LANES and n_state == LANES and k_ffn == 3 and dec_seq >= max(k_rg, k_ssd) - 1

    hid = lax.broadcasted_iota(jnp.int32, (LANES, d_ssm), 0)
    cid = lax.broadcasted_iota(jnp.int32, (LANES, d_ssm), 1)
    e_sel = (hid == cid // headdim).astype(BF16)
    g_exp = (hid == cid // (d_ssm // SSD_GROUPS)).astype(BF16)
    g_sum = (lax.broadcasted_iota(jnp.int32, (gn_w, LANES), 0) // n_state
             == lax.broadcasted_iota(jnp.int32, (gn_w, LANES), 1)).astype(BF16)

    yp = x_prompt.reshape(pb * seq, d_model)
    ys = jnp.transpose(x_sample, (1, 0, 2)).reshape(dec_seq * sb, d_model)
    p_new = [[], [], [], [], []]
    s_new = [[], [], [], [], []]
    row = lambda v: v.reshape(1, -1)
    swap = lambda a3: jnp.transpose(a3, (1, 0, 2))
    bb = SAMPLE_BB

    for l in range(depth):
        wdt_b = jnp.pad(w_in[l][:, n_main:], ((0, 0), (0, LANES - heads))).astype(BF16)
        wa_b = rg_gate_a_w[l].astype(BF16)
        wi_b = rg_gate_i_w[l].astype(BF16)
        rg_args = (rg_conv_w[l], row(rg_conv_b[l]), wa_b, row(rg_gate_a_b[l]), wi_b,
                   row(rg_gate_i_b[l]), row(rg_lambda[l]), row(g_rg_out[l]))
        dtb = _pad_lanes(ssd_dt_bias[l])
        alog = _pad_lanes(ssd_A_log[l])
        dexp = jnp.repeat(ssd_D[l].astype(F32), headdim).reshape(1, d_ssm)
        gn = row(g_ssd_norm[l])

        w_in_b = w_in[l].astype(BF16)
        ms = ys.shape[0]
        proj_s, dt_s = _in_proj(ys, row(g_mix[l]), w_in_b, wdt_b, n_main, tm=ms, tn=TN_IN)

        proj_p, dt_p = _in_proj(yp, row(g_mix[l]), w_in_b, wdt_b, n_main, tm=TM, tn=TN_IN)
        rg_steps = pb * (seq // TT_RG)
        ssd_steps = pb * (seq // SSD_CHUNK)
        rb_ssd = d_model // (2 * rg_steps + ssd_steps)
        assert rb_ssd % 16 == 0 and (2 * rg_steps + ssd_steps) * rb_ssd == d_model
        rg_p, hl_p, w_up_ta = _rglru_prompt(proj_p, pb, seq, d_rnn, *rg_args, w_ffn_up[l], TN_UP,
                                            2 * rb_ssd, tt=TT_RG)
        ssd_p, hfin_p, w_up_tb, w_out_b = _ssd_prompt(
            proj_p, dt_p, pb, seq, d_ssm, d_xbc, n_state, headdim, ssd_conv_w[l], row(ssd_conv_b[l]),
            dtb, alog, dexp, gn, e_sel, w_ffn_up[l], TN_UP, rb_ssd, 2 * rb_ssd * rg_steps, w_out[l])
        x1_p, h2_p, ss_p = _out_proj(rg_p, ssd_p, w_out_b, yp, row(g_ffn[l]), tm=TM, tn=TN_OUT)
        act_p, gst_p, w_down_t = _ffn_up_prompt(h2_p, ss_p, w_up_ta, w_up_tb, ffn_conv_w[l],
                                                row(ffn_conv_b[l]), seq, w_ffn_down[l], TN_DOWN, tm=TM)
        proj_p3 = proj_p.reshape(pb, seq, n_main)
        p_new[0].append(hl_p[:, SUBLANES - 1, :])
        p_new[1].append(proj_p3[:, seq - (k_rg - 1):, :d_rnn])
        p_new[2].append(hfin_p.reshape(pb, heads, headdim, n_state))
        p_new[3].append(proj_p3[:, seq - (k_ssd - 1):, n_main - d_xbc:])
        tiles = seq // TM
        p_new[4].append(gst_p[tiles - 1::tiles, SUBLANES - (k_ffn - 1):, :])

        proj_s3 = proj_s.reshape(dec_seq, sb, n_main)
        rg_s, hl_s = _rglru_sample(proj_s3, swap(state_rglru_conv[l]), state_rglru_h[l], d_rnn,
                                   *rg_args, bb=bb)
        ydp, ecum, xdd, bc, cd = _ssd_sample_pre(
            proj_s3, swap(state_ssd_conv[l]), dt_s.reshape(dec_seq, sb, LANES), d_ssm, d_xbc, n_state,
            ssd_conv_w[l], row(ssd_conv_b[l]), dtb, alog, dexp, e_sel, g_sum, g_exp, bb=bb)
        bc_b = swap(bc)
        snew, yoff_b = _ssd_sample_state(state_ssd_h[l], bc_b[:, :, gn_w:], bc_b[:, :, :gn_w],
                                         swap(xdd), cd, bt=SAMPLE_BT)
        ssd_s = _ssd_sample_post(ydp, swap(yoff_b), ecum, proj_s3, gn, d_ssm, d_xbc, bb=bb)
        x1_s, h2_s, ss_s = _out_proj(rg_s.reshape(ms, d_rnn), ssd_s.reshape(ms, d_ssm), w_out_b, ys,
                                     row(g_ffn[l]), tm=ms, tn=TN_OUT)
        act_s, gst_s = _ffn_up_sample(h2_s, ss_s, w_up_ta, w_up_tb, ffn_conv_w[l], row(ffn_conv_b[l]),
                                      swap(state_ffn_conv[l]).reshape(-1, d_ff), sb)
        s_new[0].append(hl_s)
        s_new[1].append(swap(proj_s3[dec_seq - (k_rg - 1):, :, :d_rnn]))
        s_new[2].append(snew)
        s_new[3].append(swap(proj_s3[dec_seq - (k_ssd - 1):, :, n_main - d_xbc:]))
        s_new[4].append(swap(gst_s.reshape(k_ffn - 1, sb, d_ff)))

        last = l == depth - 1
        yp = _ffn_down(act_p, w_down_t, x1_p, row(g_final), last, tm=TM, tk=TK_DOWN)
        ys = _ffn_down(act_s, w_down_t, x1_s, row(g_final), last, tm=ms, tk=TK_DOWN)

    y_prompt = yp.reshape(pb, seq, d_model)
    y_sample = jnp.transpose(ys.reshape(dec_seq, sb, d_model), (1, 0, 2))
    return (y_prompt, y_sample,
            jnp.stack(p_new[0]), jnp.stack(p_new[1]), jnp.stack(p_new[2]), jnp.stack(p_new[3]),
            jnp.stack(p_new[4]),
            jnp.stack(s_new[0]), jnp.stack(s_new[1]), jnp.stack(s_new[2]), jnp.stack(s_new[3]),
            jnp.stack(s_new[4]))
```
